```python
import math
import jax
import jax.numpy as jnp
from jax import lax
import numpy as np

D_MODEL = 2048
BATCH = 8
SEQ = 2048
DEPTH = 2
DEC_BATCH = 128
DEC_SEQ = 8
PAST_LEN = 2048
PAGE_SIZE = 128

D_MIX = D_MODEL
POOL_DIM = D_MIX // 4
POOL_WINDOWS = (2, 4, 8, 16)
POOL_GROUPS = len(POOL_WINDOWS)
POOL_GDIM = POOL_DIM // POOL_GROUPS
POOL_BUF = max(POOL_WINDOWS) - 1
MLSTM_DIM = D_MIX // 4
MLSTM_HEADS = 4
MLSTM_HD = MLSTM_DIM // MLSTM_HEADS
MLSTM_CHUNK = 64
NSA_DIM = D_MIX - POOL_DIM - MLSTM_DIM
NSA_HD = 128
NSA_HEADS = NSA_DIM // NSA_HD
NSA_KV_HEADS = 2
NSA_GROUP = NSA_HEADS // NSA_KV_HEADS
NSA_KV_DIM = NSA_KV_HEADS * NSA_HD
CMP_LEN = 32
CMP_STRIDE = 16
CMP_SUB = CMP_LEN // CMP_STRIDE
SEL_BLOCK = 64
SEL_TOPN = 16
WINDOW = 512
Q_BLOCK = 128
FORCE_BONUS = 1.0e4
FFN_DIM = 5632
EPS = 1e-6

OFF_POOL = 0
OFF_MQ = OFF_POOL + POOL_DIM
OFF_MK = OFF_MQ + MLSTM_DIM
OFF_MV = OFF_MK + MLSTM_DIM
OFF_MO = OFF_MV + MLSTM_DIM
OFF_MI = OFF_MO + MLSTM_DIM
OFF_MF = OFF_MI + MLSTM_HEADS
OFF_NQ = OFF_MF + MLSTM_HEADS
OFF_NKV_CMP = OFF_NQ + NSA_DIM
OFF_NKV_SLC = OFF_NKV_CMP + 2 * NSA_KV_DIM
OFF_NKV_WIN = OFF_NKV_SLC + 2 * NSA_KV_DIM
OFF_NG = OFF_NKV_WIN + 2 * NSA_KV_DIM
N_IN = OFF_NG + 3 * NSA_HEADS

kernel_name = 'hybrid_pool_mlstm_nsa_decode_step'


def rmsnorm(x, g):
    xf = x.astype(jnp.float32)
    y = xf * lax.rsqrt(jnp.mean(xf * xf, axis=-1, keepdims=True) + EPS)
    return (y * g.astype(jnp.float32)).astype(x.dtype)


def swiglu(x, w_gate, w_up, w_down):
    return (jax.nn.silu(x @ w_gate) * (x @ w_up)) @ w_down


def masked_softmax(s, mask):
    s = jnp.where(mask, s, -jnp.inf)
    m = jnp.max(s, axis=-1, keepdims=True)
    m = jnp.where(jnp.isfinite(m), m, 0.0)
    e = jnp.where(mask, jnp.exp(s - m), 0.0)
    return e / jnp.maximum(jnp.sum(e, axis=-1, keepdims=True), 1e-30)


def alibi_slopes():
    return jnp.exp2(-(8.0 / NSA_HEADS) * jnp.arange(1, NSA_HEADS + 1, dtype=jnp.float32))


def head_norm(h, g):
    mu = jnp.mean(h, axis=-1, keepdims=True)
    var = jnp.mean(jnp.square(h - mu), axis=-1, keepdims=True)
    return (h - mu) * lax.rsqrt(var + EPS) * g.astype(jnp.float32).reshape(MLSTM_HEADS, MLSTM_HD)


def pool_mix(z, prev, pos0, w_pool, scale):
    B, T, _ = z.shape
    full = jnp.concatenate([prev, z], axis=1)
    cs = jnp.cumsum(full.astype(jnp.float32), axis=1)
    cs = jnp.concatenate([jnp.zeros((B, 1, POOL_DIM), jnp.float32), cs], axis=1)
    upto = cs[:, POOL_BUF + 1:]
    n_avail = (pos0 + jnp.arange(T) + 1)[None, :, None]
    means = []
    for g, w in enumerate(POOL_WINDOWS):
        lo, hi = g * POOL_GDIM, (g + 1) * POOL_GDIM
        s = upto[..., lo:hi] - cs[:, POOL_BUF + 1 - w: POOL_BUF + 1 - w + T, lo:hi]
        means.append(s / jnp.minimum(n_avail, w).astype(jnp.float32))
    d = (jnp.concatenate(means, axis=-1) - z.astype(jnp.float32)).astype(z.dtype)
    d = d.reshape(B, T, POOL_GROUPS, POOL_GDIM)
    y = jnp.einsum('btgc,gce->btge', d, w_pool).reshape(B, T, POOL_DIM) * scale
    return y, full[:, -POOL_BUF:]


def mlstm_mix(q, k, v, i_pre, f_pre, C0, n0, m0):
    B, T, H, D = q.shape
    f32 = jnp.float32
    L = math.gcd(T, MLSTM_CHUNK)
    NC = T // L

    def chunks(a):
        return a.astype(f32).reshape(B, NC, L, *a.shape[2:]).swapaxes(0, 1)

    xs = (chunks(q), chunks(k * (D ** -0.5)), chunks(v), chunks(i_pre), chunks(jax.nn.log_sigmoid(f_pre.astype(f32))))
    causal = jnp.tril(jnp.ones((L, L), dtype=bool))

    def step(carry, inp):
        C, n, m = carry
        qx, kx, vx, ix, lfx = inp
        b = jnp.cumsum(lfx, axis=1).transpose(0, 2, 1)
        it = ix.transpose(0, 2, 1)
        dmat = jnp.where(causal, b[..., :, None] - b[..., None, :] + it[..., None, :], -jnp.inf)
        inter = b + m[..., None]
        m_t = jnp.maximum(inter, jnp.max(dmat, axis=-1))
        S = jnp.einsum('blhd,bshd->bhls', qx, kx) * jnp.exp(dmat - m_t[..., None])
        a_inter = jnp.exp(inter - m_t)
        num = jnp.einsum('bhls,bshd->blhd', S, vx) + jnp.einsum('bhl,bhed,blhd->blhe', a_inter, C, qx)
        den = jnp.sum(S, axis=-1) + a_inter * jnp.einsum('bhd,blhd->bhl', n, qx)
        den = jnp.maximum(jnp.abs(den), jnp.exp(-m_t))
        h = num / den.transpose(0, 2, 1)[..., None]
        m_new = m_t[..., -1]
        decay = jnp.exp(b[..., -1] + m - m_new)
        w_s = jnp.exp(b[..., -1:] - b + it - m_new[..., None])
        C_new = decay[..., None, None] * C + jnp.einsum('bhs,bshe,bshd->bhed', w_s, vx, kx)
        n_new = decay[..., None] * n + jnp.einsum('bhs,bshd->bhd', w_s, kx)
        return (C_new, n_new, m_new), h

    (C, n, m), hs = lax.scan(step, (C0.astype(f32), n0.astype(f32), m0.astype(f32)), xs)
    return hs.swapaxes(0, 1).reshape(B, T, H, D), C, n, m


def compress_blocks(rows, pos_w, proj):
    B, Tk = rows.shape[:2]
    n_ch = Tk // CMP_STRIDE
    n_cmp = n_ch - CMP_SUB + 1
    ch = rows[:, : n_ch * CMP_STRIDE].reshape(B, n_ch, CMP_STRIDE, NSA_KV_HEADS, NSA_HD)
    pw = pos_w.reshape(CMP_SUB, CMP_STRIDE, NSA_KV_HEADS, NSA_HD)
    acc = jnp.einsum('bnjgd,jgd->bngd', ch[:, :n_cmp], pw[0])
    for r in range(1, CMP_SUB):
        acc = acc + jnp.einsum('bnjgd,jgd->bngd', ch[:, r:r + n_cmp], pw[r])
    return jnp.einsum('bngd,gde->bnge', acc, proj)


def nsa_mix(q, gates, k_cmp, v_cmp, k_slc, v_slc, k_win, v_win, q_pos0, win_pos0, cmp_pos_w, cmp_proj):
    B, Tq = q.shape[:2]
    Tk = k_slc.shape[1]
    KVH, G, HD = NSA_KV_HEADS, NSA_GROUP, NSA_HD
    f32 = jnp.float32
    slopes = alibi_slopes().reshape(KVH, G)
    qg = q.reshape(B, Tq, KVH, G, HD)
    t_pos = q_pos0 + jnp.arange(Tq)

    kc = compress_blocks(k_cmp, cmp_pos_w[0], cmp_proj[0])
    vc = compress_blocks(v_cmp, cmp_pos_w[1], cmp_proj[1])
    n_cmp = kc.shape[1]
    c_start = jnp.arange(n_cmp) * CMP_STRIDE
    dist_c = t_pos[:, None] - (c_start + (CMP_LEN - 1))[None, :]
    s_c = jnp.einsum('bqgrd,bngd->bgrqn', qg, kc).astype(f32) - slopes[:, :, None, None] * dist_c.astype(f32)
    p_c = masked_softmax(s_c, dist_c >= 0)
    o_cmp = jnp.einsum('bgrqn,bngd->bqgrd', p_c.astype(vc.dtype), vc)

    n_sel = -(-Tk // SEL_BLOCK)
    k_top = min(SEL_TOPN, n_sel)
    s_start = jnp.arange(n_sel) * SEL_BLOCK
    cover = ((c_start[:, None] < s_start[None, :] + SEL_BLOCK) & (c_start[:, None] + CMP_LEN > s_start[None, :])).astype(f32)
    imp = jnp.einsum('bgrqn,nj->bgqj', p_c, cover)
    cur = t_pos // SEL_BLOCK
    sel_ids = jnp.arange(n_sel)
    forced = (sel_ids[None, :] == 0) | (sel_ids[None, :] == cur[:, None]) | (sel_ids[None, :] == cur[:, None] - 1)
    score = jnp.where(s_start[None, :] <= t_pos[:, None], imp + jnp.where(forced, FORCE_BONUS, 0.0), -jnp.inf)
    _, sel_idx = lax.top_k(score, k_top)

    pad = n_sel * SEL_BLOCK - Tk
    kp = jnp.pad(k_slc, ((0, 0), (0, pad), (0, 0), (0, 0))).reshape(B, n_sel, SEL_BLOCK, KVH, HD).transpose(0, 3, 1, 2, 4)
    vp = jnp.pad(v_slc, ((0, 0), (0, pad), (0, 0), (0, 0))).reshape(B, n_sel, SEL_BLOCK, KVH, HD).transpose(0, 3, 1, 2, 4)
    kwp = jnp.pad(k_win, ((0, 0), (WINDOW, 0), (0, 0), (0, 0)))
    vwp = jnp.pad(v_win, ((0, 0), (WINDOW, 0), (0, 0), (0, 0)))
    QB = min(Q_BLOCK, Tq)
    nqb = Tq // QB
    band = WINDOW + QB
    qb = qg.reshape(B, nqb, QB, KVH, G, HD)
    idx_b = sel_idx.reshape(B, KVH, nqb, QB, k_top).transpose(0, 2, 1, 3, 4)
    t_b = t_pos.reshape(nqb, QB)
    w_start = q_pos0 - win_pos0 + jnp.arange(nqb) * QB
    head_ix = jnp.arange(KVH)[:, None, None]

    def block_attend(args):
        b, j = args
        qx = qb[b, j]
        ix = idx_b[b, j]
        tx = t_b[j]
        ks = kp[b][head_ix, ix]
        vs = vp[b][head_ix, ix]
        pos_s = ix[..., None] * SEL_BLOCK + jnp.arange(SEL_BLOCK)
        dist_s = tx[None, :, None, None] - pos_s
        s_s = jnp.einsum('qgrd,gqkld->grqkl', qx, ks).astype(f32) - slopes[:, :, None, None, None] * dist_s[:, None].astype(f32)
        p_s = masked_softmax(s_s.reshape(KVH, G, QB, k_top * SEL_BLOCK),
                             (dist_s >= 0)[:, None].reshape(KVH, 1, QB, k_top * SEL_BLOCK))
        o_s = jnp.einsum('grqkl,gqkld->qgrd', p_s.reshape(KVH, G, QB, k_top, SEL_BLOCK).astype(vs.dtype), vs)
        kw = lax.dynamic_slice_in_dim(kwp[b], w_start[j], band, axis=0)
        vw = lax.dynamic_slice_in_dim(vwp[b], w_start[j], band, axis=0)
        kpos_w = win_pos0 - WINDOW + w_start[j] + jnp.arange(band)
        dist_w = tx[:, None] - kpos_w[None, :]
        valid_w = (dist_w >= 0) & (dist_w <= WINDOW) & (kpos_w >= win_pos0)[None, :]
        s_w = jnp.einsum('qgrd,kgd->grqk', qx, kw).astype(f32) - slopes[:, :, None, None] * dist_w.astype(f32)
        p_w = masked_softmax(s_w, valid_w)
        o_w = jnp.einsum('grqk,kgd->qgrd', p_w.astype(vw.dtype), vw)
        return o_s, o_w

    items_b = jnp.repeat(jnp.arange(B), nqb)
    items_j = jnp.tile(jnp.arange(nqb), B)
    o_slc, o_win = lax.map(block_attend, (items_b, items_j))
    o_slc = o_slc.reshape(B, Tq, KVH, G, HD)
    o_win = o_win.reshape(B, Tq, KVH, G, HD)
    gr = gates.reshape(B, Tq, 3, KVH, G, 1)
    o = gr[:, :, 0] * o_cmp + gr[:, :, 1] * o_slc + gr[:, :, 2] * o_win
    return o.reshape(B, Tq, NSA_DIM)


def trunk_layer(x, pos0, pool_prev, C0, n0, m0, past_cmp, past_slc, win_prev,
                n1, wg1, wu1, wd1, nm, w_in_l, w_out_l, pool_w_l, pool_scale_l,
                if_bias_l, mnorm_l, cmp_pos_w_l, cmp_proj_l, n2, wg2, wu2, wd2):
    B, T, _ = x.shape
    x = x + 0.5 * swiglu(rmsnorm(x, n1), wg1, wu1, wd1)
    z = rmsnorm(x, nm) @ w_in_l

    y_pool, pool_new = pool_mix(z[..., OFF_POOL:OFF_MQ], pool_prev, pos0, pool_w_l, pool_scale_l)

    hs = (B, T, MLSTM_HEADS, MLSTM_HD)
    mq = z[..., OFF_MQ:OFF_MK].reshape(hs)
    mk = z[..., OFF_MK:OFF_MV].reshape(hs)
    mv = z[..., OFF_MV:OFF_MO].reshape(hs)
    mo = jax.nn.sigmoid(z[..., OFF_MO:OFF_MI].astype(jnp.float32)).reshape(hs)
    mi = z[..., OFF_MI:OFF_MF] + if_bias_l[:MLSTM_HEADS]
    mf = z[..., OFF_MF:OFF_NQ] + if_bias_l[MLSTM_HEADS:]
    h_til, C, n, m = mlstm_mix(mq, mk, mv, mi, mf, C0, n0, m0)
    y_mlstm = (mo * head_norm(h_til, mnorm_l)).reshape(B, T, MLSTM_DIM).astype(x.dtype)

    kv_shape = (B, T, 2, NSA_KV_HEADS, NSA_HD)
    nq = (z[..., OFF_NQ:OFF_NKV_CMP] * (NSA_HD ** -0.5)).reshape(B, T, NSA_HEADS, NSA_HD)
    kv_cmp = z[..., OFF_NKV_CMP:OFF_NKV_SLC].reshape(kv_shape)
    kv_slc = z[..., OFF_NKV_SLC:OFF_NKV_WIN].reshape(kv_shape)
    kv_win = z[..., OFF_NKV_WIN:OFF_NG].reshape(kv_shape)
    gates = jax.nn.sigmoid(z[..., OFF_NG:N_IN]).reshape(B, T, 3, NSA_HEADS)
    if past_cmp is None:
        all_cmp, all_slc, win_rows, win_pos0 = kv_cmp, kv_slc, kv_win, pos0
    else:
        all_cmp = jnp.concatenate([past_cmp, kv_cmp], axis=1)
        all_slc = jnp.concatenate([past_slc, kv_slc], axis=1)
        win_rows = jnp.concatenate([win_prev, kv_win], axis=1)
        win_pos0 = pos0 - win_prev.shape[1]
    y_nsa = nsa_mix(nq, gates, all_cmp[:, :, 0], all_cmp[:, :, 1], all_slc[:, :, 0], all_slc[:, :, 1],
                    win_rows[:, :, 0], win_rows[:, :, 1], pos0, win_pos0, cmp_pos_w_l, cmp_proj_l).astype(x.dtype)
    win_new = win_rows[:, -min(WINDOW, win_rows.shape[1]):]

    x = x + jnp.concatenate([y_pool, y_mlstm, y_nsa], axis=-1) @ w_out_l
    x = x + 0.5 * swiglu(rmsnorm(x, n2), wg2, wu2, wd2)
    return x, (kv_cmp, kv_slc, win_new, pool_new, C.astype(C0.dtype), n.astype(n0.dtype), m.astype(m0.dtype))


def gather_pages(pool_l, page_table):
    g = pool_l[page_table]
    return g.reshape(page_table.shape[0], -1, 2, NSA_KV_HEADS, NSA_HD)


def setup_inputs(seed: int = 0) -> dict:
    key = jax.random.key(seed)
    ks = jax.random.split(key, 32)
    f32 = jnp.float32

    def nrm(i, shape, scale=1.0):
        return jax.random.normal(ks[i], shape, f32) * scale

    n_pages = PAST_LEN // PAGE_SIZE
    n_pool = (DEC_BATCH * n_pages * 5) // 4
    win_buf = min(WINDOW, PAST_LEN)
    kv_row = (2, NSA_KV_HEADS, NSA_HD)
    page_table = jax.random.permutation(ks[0], n_pool)[: DEC_BATCH * n_pages].reshape(DEC_BATCH, n_pages).astype(jnp.int32)
    f_bias = jnp.linspace(3.0, 6.0, MLSTM_HEADS, dtype=f32)[None, :]
    mlstm_if_bias = jnp.concatenate([nrm(19, (DEPTH, MLSTM_HEADS), 0.1), f_bias + nrm(20, (DEPTH, MLSTM_HEADS), 0.1)], axis=-1)
    return {
        'x_prompt': nrm(1, (BATCH, SEQ, D_MODEL)),
        'x_sample': nrm(2, (DEC_BATCH, DEC_SEQ, D_MODEL)),
        'cache_kv_cmp': nrm(3, (DEPTH, n_pool, PAGE_SIZE) + kv_row),
        'cache_kv_slc': nrm(4, (DEPTH, n_pool, PAGE_SIZE) + kv_row),
        'state_kv_win': nrm(5, (DEPTH, DEC_BATCH, win_buf) + kv_row),
        'state_pool': nrm(6, (DEPTH, DEC_BATCH, POOL_BUF, POOL_DIM)),
        'state_mlstm_C': nrm(7, (DEPTH, DEC_BATCH, MLSTM_HEADS, MLSTM_HD, MLSTM_HD), 0.3),
        'state_mlstm_n': nrm(8, (DEPTH, DEC_BATCH, MLSTM_HEADS, MLSTM_HD), 0.3),
        'state_mlstm_m': nrm(9, (DEPTH, DEC_BATCH, MLSTM_HEADS)),
        'page_table': page_table,
        'ffn1_norm': 1.0 + nrm(10, (DEPTH, D_MODEL), 0.02),
        'ffn1_w_gate': nrm(11, (DEPTH, D_MODEL, FFN_DIM), D_MODEL ** -0.5),
        'ffn1_w_up': nrm(12, (DEPTH, D_MODEL, FFN_DIM), D_MODEL ** -0.5),
        'ffn1_w_down': nrm(13, (DEPTH, FFN_DIM, D_MODEL), FFN_DIM ** -0.5),
        'mix_norm': 1.0 + nrm(14, (DEPTH, D_MODEL), 0.02),
        'w_in': nrm(15, (DEPTH, D_MODEL, N_IN), D_MODEL ** -0.5),
        'w_out': nrm(16, (DEPTH, D_MIX, D_MODEL), D_MIX ** -0.5),
        'pool_w': nrm(17, (DEPTH, POOL_GROUPS, POOL_GDIM, POOL_GDIM), POOL_GDIM ** -0.5),
        'pool_scale': 1.0 + nrm(18, (DEPTH, POOL_DIM), 0.1),
        'mlstm_if_bias': mlstm_if_bias,
        'mlstm_norm': 1.0 + nrm(21, (DEPTH, MLSTM_DIM), 0.02),
        'nsa_cmp_pos_w': nrm(22, (DEPTH, 2, CMP_LEN, NSA_KV_HEADS, NSA_HD), CMP_LEN ** -0.5),
        'nsa_cmp_proj': nrm(23, (DEPTH, 2, NSA_KV_HEADS, NSA_HD, NSA_HD), NSA_HD ** -0.5),
        'ffn2_norm': 1.0 + nrm(24, (DEPTH, D_MODEL), 0.02),
        'ffn2_w_gate': nrm(25, (DEPTH, D_MODEL, FFN_DIM), D_MODEL ** -0.5),
        'ffn2_w_up': nrm(26, (DEPTH, D_MODEL, FFN_DIM), D_MODEL ** -0.5),
        'ffn2_w_down': nrm(27, (DEPTH, FFN_DIM, D_MODEL), FFN_DIM ** -0.5),
        'final_norm': 1.0 + nrm(28, (D_MODEL,), 0.02),
    }


def reference(x_prompt, x_sample, cache_kv_cmp, cache_kv_slc, state_kv_win, state_pool,
              state_mlstm_C, state_mlstm_n, state_mlstm_m, page_table,
              ffn1_norm, ffn1_w_gate, ffn1_w_up, ffn1_w_down, mix_norm, w_in, w_out,
              pool_w, pool_scale, mlstm_if_bias, mlstm_norm, nsa_cmp_pos_w, nsa_cmp_proj,
              ffn2_norm, ffn2_w_gate, ffn2_w_up, ffn2_w_down, final_norm):
    past_len = page_table.shape[1] * cache_kv_cmp.shape[2]
    xp, xs = x_prompt, x_sample
    bp = xp.shape[0]
    prompt_states, sample_states = [], []
    for l in range(DEPTH):
        lw = (ffn1_norm[l], ffn1_w_gate[l], ffn1_w_up[l], ffn1_w_down[l], mix_norm[l], w_in[l], w_out[l],
              pool_w[l], pool_scale[l], mlstm_if_bias[l], mlstm_norm[l], nsa_cmp_pos_w[l], nsa_cmp_proj[l],
              ffn2_norm[l], ffn2_w_gate[l], ffn2_w_up[l], ffn2_w_down[l])
        xp, st_p = trunk_layer(
            xp, 0,
            jnp.zeros((bp, POOL_BUF, POOL_DIM), xp.dtype),
            jnp.zeros((bp, MLSTM_HEADS, MLSTM_HD, MLSTM_HD), xp.dtype),
            jnp.zeros((bp, MLSTM_HEADS, MLSTM_HD), xp.dtype),
            jnp.zeros((bp, MLSTM_HEADS), xp.dtype),
            None, None, None, *lw)
        prompt_states.append(st_p)
        xs, st_s = trunk_layer(
            xs, past_len, state_pool[l], state_mlstm_C[l], state_mlstm_n[l], state_mlstm_m[l],
            gather_pages(cache_kv_cmp[l], page_table), gather_pages(cache_kv_slc[l], page_table),
            state_kv_win[l], *lw)
        sample_states.append(st_s)
    y_prompt = rmsnorm(xp, final_norm)
    y_sample = rmsnorm(xs, final_norm)
    p_kv_cmp, p_kv_slc, p_kv_win, p_pool, p_C, p_n, p_m = [jnp.stack(a) for a in zip(*prompt_states)]
    s_kv_cmp, s_kv_slc, s_kv_win, s_pool, s_C, s_n, s_m = [jnp.stack(a) for a in zip(*sample_states)]
    return (y_prompt, y_sample, p_kv_cmp, p_kv_slc, p_kv_win, p_pool, p_C, p_n, p_m,
            s_kv_cmp, s_kv_slc, s_kv_win, s_pool, s_C, s_n, s_m)
```

```python
import functools
import math

import jax
import jax.numpy as jnp
from jax import lax
from jax.experimental import pallas as pl
from jax.experimental.pallas import tpu as pltpu

F32 = jnp.float32
BF16 = jnp.bfloat16
EPS = 1e-6

VMEM_LIMIT_BYTES = 56 * 1024 * 1024
LANES = 128

POOL_WINDOWS = (2, 4, 8, 16)
POOL_BUF = 15
MLSTM_HEADS = 4
MLSTM_HD = 128
MLSTM_CHUNK = 64
NSA_HD = 128
NSA_HEADS = 8
NSA_KV_HEADS = 2
NSA_GROUP = 4
CMP_LEN = 32
CMP_STRIDE = 16
SEL_BLOCK = 64
SEL_TOPN = 16
WINDOW = 512
Q_BLOCK = 128
FORCE_BONUS = 1.0e4


def _pick_tile(n, pref):
    t = pref
    while t > 8 and n % t:
        t //= 2
    assert n % t == 0, (n, pref)
    return t


def _params(*sem):
    return pltpu.CompilerParams(dimension_semantics=sem, vmem_limit_bytes=VMEM_LIMIT_BYTES)


def _rms_rows(x, g):
    ms = jnp.mean(x * x, axis=-1, keepdims=True)
    return x * lax.rsqrt(ms + EPS) * g


def _ffn_body(x_ref, g_ref, wg_ref, wu_ref, wd_ref, gf_ref, o_ref, n_scr, *, final_norm):
    f = pl.program_id(1)

    @pl.when(f == 0)
    def _():
        x = x_ref[...]
        n_scr[...] = _rms_rows(x, g_ref[...]).astype(BF16)
        o_ref[...] = x

    n = n_scr[...]
    hg = jnp.dot(n, wg_ref[...], preferred_element_type=F32)
    hu = jnp.dot(n, wu_ref[...], preferred_element_type=F32)
    h = (hg * jax.nn.sigmoid(hg) * hu).astype(BF16)
    o_ref[...] += 0.5 * jnp.dot(h, wd_ref[...], preferred_element_type=F32)

    if final_norm:
        @pl.when(f == pl.num_programs(1) - 1)
        def _():
            o_ref[...] = _rms_rows(o_ref[...], gf_ref[...])


def ffn_half_step(x, g, wg, wu, wd, gf=None):
    m, d = x.shape
    fdim = wg.shape[1]
    tm = _pick_tile(m, 512)
    tf = _pick_tile(fdim, 512)
    final_norm = gf is not None
    if gf is None:
        gf = g
    return pl.pallas_call(
        functools.partial(_ffn_body, final_norm=final_norm),
        out_shape=jax.ShapeDtypeStruct((m, d), F32),
        grid=(m // tm, fdim // tf),
        in_specs=[
            pl.BlockSpec((tm, d), lambda i, f: (i, 0)),
            pl.BlockSpec((1, d), lambda i, f: (0, 0)),
            pl.BlockSpec((d, tf), lambda i, f: (0, f)),
            pl.BlockSpec((d, tf), lambda i, f: (0, f)),
            pl.BlockSpec((tf, d), lambda i, f: (f, 0)),
            pl.BlockSpec((1, d), lambda i, f: (0, 0)),
        ],
        out_specs=pl.BlockSpec((tm, d), lambda i, f: (i, 0)),
        scratch_shapes=[pltpu.VMEM((tm, d), BF16)],
        compiler_params=_params("parallel", "arbitrary"),
        name="ffn_half_step",
    )(x, g.reshape(1, d), wg, wu, wd, gf.reshape(1, d))


def _inproj_body(x_ref, g_ref, w_ref, o_ref, n_scr):
    @pl.when(pl.program_id(1) == 0)
    def _():
        n_scr[...] = _rms_rows(x_ref[...], g_ref[...]).astype(BF16)

    o_ref[...] = jnp.dot(n_scr[...], w_ref[...], preferred_element_type=F32)


def norm_project(x, g, w):
    m, d = x.shape
    n = w.shape[1]
    tm = _pick_tile(m, 1024)
    tn = _pick_tile(n, 512)
    return pl.pallas_call(
        _inproj_body,
        out_shape=jax.ShapeDtypeStruct((m, n), F32),
        grid=(m // tm, n // tn),
        in_specs=[
            pl.BlockSpec((tm, d), lambda i, j: (i, 0)),
            pl.BlockSpec((1, d), lambda i, j: (0, 0)),
            pl.BlockSpec((d, tn), lambda i, j: (0, j)),
        ],
        out_specs=pl.BlockSpec((tm, tn), lambda i, j: (i, j)),
        scratch_shapes=[pltpu.VMEM((tm, d), BF16)],
        compiler_params=_params("parallel", "arbitrary"),
        name="norm_project",
    )(x, g.reshape(1, d), w)


def _outproj_body(x_ref, ya_ref, yb_ref, yc_ref, wa_ref, wb_ref, wc_ref, o_ref):
    acc = x_ref[...]
    acc += jnp.dot(ya_ref[...].astype(BF16), wa_ref[...], preferred_element_type=F32)
    acc += jnp.dot(yb_ref[...].astype(BF16), wb_ref[...], preferred_element_type=F32)
    acc += jnp.dot(yc_ref[...].astype(BF16), wc_ref[...], preferred_element_type=F32)
    o_ref[...] = acc


def out_project(x, y_pool, y_mlstm, y_nsa, w_out):
    m, d = x.shape
    da, db, dc = y_pool.shape[1], y_mlstm.shape[1], y_nsa.shape[1]
    assert da == db and dc % da == 0
    tm = _pick_tile(m, 512)
    tn = _pick_tile(d, 1024)
    return pl.pallas_call(
        _outproj_body,
        out_shape=jax.ShapeDtypeStruct((m, d), F32),
        grid=(m // tm, d // tn),
        in_specs=[
            pl.BlockSpec((tm, tn), lambda i, j: (i, j)),
            pl.BlockSpec((tm, da), lambda i, j: (i, 0)),
            pl.BlockSpec((tm, db), lambda i, j: (i, 0)),
            pl.BlockSpec((tm, dc), lambda i, j: (i, 0)),
            pl.BlockSpec((da, tn), lambda i, j: (0, j)),
            pl.BlockSpec((db, tn), lambda i, j: (1, j)),
            pl.BlockSpec((dc, tn), lambda i, j: ((da + db) // dc, j)),
        ],
        out_specs=pl.BlockSpec((tm, tn), lambda i, j: (i, j)),
        compiler_params=_params("parallel", "arbitrary"),
        name="out_project",
    )(x, y_pool, y_mlstm, y_nsa, w_out, w_out, w_out)


def _pool_mix_jnp(z, prev, pos0, w_pool, scale):
    B, T, PD = z.shape
    gd = PD // 4
    full = jnp.concatenate([prev, z], axis=1)
    cs = jnp.cumsum(full.astype(F32), axis=1)
    cs = jnp.concatenate([jnp.zeros((B, 1, PD), F32), cs], axis=1)
    upto = cs[:, POOL_BUF + 1:]
    n_avail = (pos0 + jnp.arange(T) + 1)[None, :, None]
    means = []
    for g, w in enumerate(POOL_WINDOWS):
        lo, hi = g * gd, (g + 1) * gd
        s = upto[..., lo:hi] - cs[:, POOL_BUF + 1 - w: POOL_BUF + 1 - w + T, lo:hi]
        means.append(s / jnp.minimum(n_avail, w).astype(F32))
    dd = (jnp.concatenate(means, axis=-1) - z).reshape(B, T, 4, gd)
    y = jnp.einsum('btgc,gce->btge', dd, w_pool).reshape(B, T, PD) * scale
    return y, full[:, -POOL_BUF:]


def _mlstm_mix_jnp(q, k, v, i_pre, f_pre, C0, n0, m0):
    B, T, H, D = q.shape
    L = math.gcd(T, MLSTM_CHUNK)
    NC = T // L

    def chunks(a):
        return a.reshape(B, NC, L, *a.shape[2:]).swapaxes(0, 1)

    xs = (chunks(q), chunks(k * (D ** -0.5)), chunks(v), chunks(i_pre), chunks(jax.nn.log_sigmoid(f_pre)))
    causal = jnp.tril(jnp.ones((L, L), dtype=bool))

    def step(carry, inp):
        C, n, m = carry
        qx, kx, vx, ix, lfx = inp
        b = jnp.cumsum(lfx, axis=1).transpose(0, 2, 1)
        it = ix.transpose(0, 2, 1)
        dmat = jnp.where(causal, b[..., :, None] - b[..., None, :] + it[..., None, :], -jnp.inf)
        inter = b + m[..., None]
        m_t = jnp.maximum(inter, jnp.max(dmat, axis=-1))
        S = jnp.einsum('blhd,bshd->bhls', qx, kx) * jnp.exp(dmat - m_t[..., None])
        a_inter = jnp.exp(inter - m_t)
        num = jnp.einsum('bhls,bshd->blhd', S, vx) + jnp.einsum('bhl,bhed,blhd->blhe', a_inter, C, qx)
        den = jnp.sum(S, axis=-1) + a_inter * jnp.einsum('bhd,blhd->bhl', n, qx)
        den = jnp.maximum(jnp.abs(den), jnp.exp(-m_t))
        h = num / den.transpose(0, 2, 1)[..., None]
        m_new = m_t[..., -1]
        decay = jnp.exp(b[..., -1] + m - m_new)
        w_s = jnp.exp(b[..., -1:] - b + it - m_new[..., None])
        C_new = decay[..., None, None] * C + jnp.einsum('bhs,bshe,bshd->bhed', w_s, vx, kx)
        n_new = decay[..., None] * n + jnp.einsum('bhs,bshd->bhd', w_s, kx)
        return (C_new, n_new, m_new), h

    (C, n, m), hs = lax.scan(step, (C0, n0, m0), xs)
    return hs.swapaxes(0, 1).reshape(B, T, H, D), C, n, m


def _masked_softmax(s, mask):
    s = jnp.where(mask, s, -jnp.inf)
    m = jnp.max(s, axis=-1, keepdims=True)
    m = jnp.where(jnp.isfinite(m), m, 0.0)
    e = jnp.where(mask, jnp.exp(s - m), 0.0)
    return e / jnp.maximum(jnp.sum(e, axis=-1, keepdims=True), 1e-30)


def _compress_jnp(rows, pos_w, proj):
    B, Tk = rows.shape[:2]
    n_ch = Tk // CMP_STRIDE
    n_cmp = n_ch - 1
    ch = rows[:, : n_ch * CMP_STRIDE].reshape(B, n_ch, CMP_STRIDE, NSA_KV_HEADS, NSA_HD)
    pw = pos_w.reshape(2, CMP_STRIDE, NSA_KV_HEADS, NSA_HD)
    acc = jnp.einsum('bnjgd,jgd->bngd', ch[:, :n_cmp], pw[0]) + jnp.einsum('bnjgd,jgd->bngd', ch[:, 1:1 + n_cmp], pw[1])
    return jnp.einsum('bngd,gde->bnge', acc, proj)


def _nsa_mix_jnp(q, gates, k_cmp, v_cmp, k_slc, v_slc, k_win, v_win, q_pos0, win_pos0, cmp_pos_w, cmp_proj):
    B, Tq = q.shape[:2]
    Tk = k_slc.shape[1]
    KVH, G, HD = NSA_KV_HEADS, NSA_GROUP, NSA_HD
    slopes = jnp.exp2(-(8.0 / NSA_HEADS) * jnp.arange(1, NSA_HEADS + 1, dtype=F32)).reshape(KVH, G)
    qg = q.reshape(B, Tq, KVH, G, HD)
    t_pos = q_pos0 + jnp.arange(Tq)
    kc = _compress_jnp(k_cmp, cmp_pos_w[0], cmp_proj[0])
    vc = _compress_jnp(v_cmp, cmp_pos_w[1], cmp_proj[1])
    n_cmp = kc.shape[1]
    c_start = jnp.arange(n_cmp) * CMP_STRIDE
    dist_c = t_pos[:, None] - (c_start + (CMP_LEN - 1))[None, :]
    s_c = jnp.einsum('bqgrd,bngd->bgrqn', qg, kc) - slopes[:, :, None, None] * dist_c.astype(F32)
    p_c = _masked_softmax(s_c, dist_c >= 0)
    o_cmp = jnp.einsum('bgrqn,bngd->bqgrd', p_c, vc)
    n_sel = -(-Tk // SEL_BLOCK)
    k_top = min(SEL_TOPN, n_sel)
    s_start = jnp.arange(n_sel) * SEL_BLOCK
    cover = ((c_start[:, None] < s_start[None, :] + SEL_BLOCK) & (c_start[:, None] + CMP_LEN > s_start[None, :])).astype(F32)
    imp = jnp.einsum('bgrqn,nj->bgqj', p_c, cover)
    cur = t_pos // SEL_BLOCK
    sel_ids = jnp.arange(n_sel)
    forced = (sel_ids[None, :] == 0) | (sel_ids[None, :] == cur[:, None]) | (sel_ids[None, :] == cur[:, None] - 1)
    score = jnp.where(s_start[None, :] <= t_pos[:, None], imp + jnp.where(forced, FORCE_BONUS, 0.0), -jnp.inf)
    _, sel_idx = lax.top_k(score, k_top)
    pad = n_sel * SEL_BLOCK - Tk
    kp = jnp.pad(k_slc, ((0, 0), (0, pad), (0, 0), (0, 0))).reshape(B, n_sel, SEL_BLOCK, KVH, HD).transpose(0, 3, 1, 2, 4)
    vp = jnp.pad(v_slc, ((0, 0), (0, pad), (0, 0), (0, 0))).reshape(B, n_sel, SEL_BLOCK, KVH, HD).transpose(0, 3, 1, 2, 4)
    kwp = jnp.pad(k_win, ((0, 0), (WINDOW, 0), (0, 0), (0, 0)))
    vwp = jnp.pad(v_win, ((0, 0), (WINDOW, 0), (0, 0), (0, 0)))
    QB = min(Q_BLOCK, Tq)
    nqb = Tq // QB
    band = WINDOW + QB
    qb = qg.reshape(B, nqb, QB, KVH, G, HD)
    idx_b = sel_idx.reshape(B, KVH, nqb, QB, k_top).transpose(0, 2, 1, 3, 4)
    t_b = t_pos.reshape(nqb, QB)
    w_start = q_pos0 - win_pos0 + jnp.arange(nqb) * QB
    head_ix = jnp.arange(KVH)[:, None, None]

    def block_attend(args):
        b, j = args
        qx = qb[b, j]
        ix = idx_b[b, j]
        tx = t_b[j]
        ks = kp[b][head_ix, ix]
        vs = vp[b][head_ix, ix]
        pos_s = ix[..., None] * SEL_BLOCK + jnp.arange(SEL_BLOCK)
        dist_s = tx[None, :, None, None] - pos_s
        s_s = jnp.einsum('qgrd,gqkld->grqkl', qx, ks) - slopes[:, :, None, None, None] * dist_s[:, None].astype(F32)
        p_s = _masked_softmax(s_s.reshape(KVH, G, QB, k_top * SEL_BLOCK),
                              (dist_s >= 0)[:, None].reshape(KVH, 1, QB, k_top * SEL_BLOCK))
        o_s = jnp.einsum('grqkl,gqkld->qgrd', p_s.reshape(KVH, G, QB, k_top, SEL_BLOCK), vs)
        kw = lax.dynamic_slice_in_dim(kwp[b], w_start[j], band, axis=0)
        vw = lax.dynamic_slice_in_dim(vwp[b], w_start[j], band, axis=0)
        kpos_w = win_pos0 - WINDOW + w_start[j] + jnp.arange(band)
        dist_w = tx[:, None] - kpos_w[None, :]
        valid_w = (dist_w >= 0) & (dist_w <= WINDOW) & (kpos_w >= win_pos0)[None, :]
        s_w = jnp.einsum('qgrd,kgd->grqk', qx, kw) - slopes[:, :, None, None] * dist_w.astype(F32)
        p_w = _masked_softmax(s_w, valid_w)
        o_w = jnp.einsum('grqk,kgd->qgrd', p_w, vw)
        return o_s, o_w

    items_b = jnp.repeat(jnp.arange(B), nqb)
    items_j = jnp.tile(jnp.arange(nqb), B)
    o_slc, o_win = lax.map(block_attend, (items_b, items_j))
    o_slc = o_slc.reshape(B, Tq, KVH, G, HD)
    o_win = o_win.reshape(B, Tq, KVH, G, HD)
    gr = gates.reshape(B, Tq, 3, KVH, G, 1)
    o = gr[:, :, 0] * o_cmp + gr[:, :, 1] * o_slc + gr[:, :, 2] * o_win
    return o.reshape(B, Tq, NSA_HEADS * HD)


ZB_POOL, ZB_MQ, ZB_MK, ZB_MV, ZB_MO, ZB_NQ, ZB_CMP, ZB_SLC, ZB_WIN, ZB_END = (
    0, 512, 1024, 1536, 2048, 2560, 3584, 4096, 4608, 5120)
ZS_MI, ZS_MF, ZS_NG, ZS_END = 0, 4, 8, 32


def _split_w_in(w_in_l):
    big = jnp.concatenate([w_in_l[:, 0:2560], w_in_l[:, 2568:5128]], axis=1)
    small = jnp.concatenate([w_in_l[:, 2560:2568], w_in_l[:, 5128:5152]], axis=1)
    small = jnp.pad(small, ((0, 0), (0, LANES - small.shape[1])))
    return big.astype(BF16), small.astype(BF16)


def _group_mixers(zb, zs, pos0, pool_prev, C0, n0, m0, past_cmp, past_slc, win_prev,
                  pool_w_l, pool_scale_l, if_bias_l, mnorm_l, cmp_pos_w_l, cmp_proj_l):
    B, T = zb.shape[:2]
    y_pool, pool_new = _pool_mix_jnp(zb[..., ZB_POOL:ZB_MQ], pool_prev, pos0, pool_w_l, pool_scale_l)
    hs = (B, T, MLSTM_HEADS, MLSTM_HD)
    mq = zb[..., ZB_MQ:ZB_MK].reshape(hs)
    mk = zb[..., ZB_MK:ZB_MV].reshape(hs)
    mv = zb[..., ZB_MV:ZB_MO].reshape(hs)
    mo = jax.nn.sigmoid(zb[..., ZB_MO:ZB_NQ]).reshape(hs)
    mi = zs[..., ZS_MI:ZS_MF] + if_bias_l[:MLSTM_HEADS]
    mf = zs[..., ZS_MF:ZS_NG] + if_bias_l[MLSTM_HEADS:]
    h_til, C, n, m = _mlstm_mix_jnp(mq, mk, mv, mi, mf, C0, n0, m0)
    mu = jnp.mean(h_til, axis=-1, keepdims=True)
    var = jnp.mean(jnp.square(h_til - mu), axis=-1, keepdims=True)
    hn = (h_til - mu) * lax.rsqrt(var + EPS) * mnorm_l.reshape(MLSTM_HEADS, MLSTM_HD)
    y_mlstm = (mo * hn).reshape(B, T, MLSTM_HEADS * MLSTM_HD)
    kv_shape = (B, T, 2, NSA_KV_HEADS, NSA_HD)
    nq = (zb[..., ZB_NQ:ZB_CMP] * (NSA_HD ** -0.5)).reshape(B, T, NSA_HEADS, NSA_HD)
    kv_cmp = zb[..., ZB_CMP:ZB_SLC].reshape(kv_shape)
    kv_slc = zb[..., ZB_SLC:ZB_WIN].reshape(kv_shape)
    kv_win = zb[..., ZB_WIN:ZB_END].reshape(kv_shape)
    gates = jax.nn.sigmoid(zs[..., ZS_NG:ZS_END]).reshape(B, T, 3, NSA_HEADS)
    if past_cmp is None:
        all_cmp, all_slc, win_rows, win_pos0 = kv_cmp, kv_slc, kv_win, pos0
    else:
        all_cmp = jnp.concatenate([past_cmp, kv_cmp], axis=1)
        all_slc = jnp.concatenate([past_slc, kv_slc], axis=1)
        win_rows = jnp.concatenate([win_prev, kv_win], axis=1)
        win_pos0 = pos0 - win_prev.shape[1]
    y_nsa = _nsa_mix_jnp(nq, gates, all_cmp[:, :, 0], all_cmp[:, :, 1], all_slc[:, :, 0], all_slc[:, :, 1],
                         win_rows[:, :, 0], win_rows[:, :, 1], pos0, win_pos0, cmp_pos_w_l, cmp_proj_l)
    win_new = win_rows[:, -min(WINDOW, win_rows.shape[1]):]
    return (y_pool, y_mlstm, y_nsa), (kv_cmp, kv_slc, win_new, pool_new, C, n, m)


def kernel(x_prompt, x_sample, cache_kv_cmp, cache_kv_slc, state_kv_win, state_pool, state_mlstm_C, state_mlstm_n, state_mlstm_m, page_table, ffn1_norm, ffn1_w_gate, ffn1_w_up, ffn1_w_down, mix_norm, w_in, w_out, pool_w, pool_scale, mlstm_if_bias, mlstm_norm, nsa_cmp_pos_w, nsa_cmp_proj, ffn2_norm, ffn2_w_gate, ffn2_w_up, ffn2_w_down, final_norm):
    bp, tp, d = x_prompt.shape
    bs, ts, _ = x_sample.shape
    depth = w_in.shape[0]
    mp, ms = bp * tp, bs * ts
    past_len = page_table.shape[1] * cache_kv_cmp.shape[2]
    x = jnp.concatenate([x_prompt.reshape(mp, d), x_sample.reshape(ms, d)], axis=0)
    p_states, s_states = [], []
    for l in range(depth):
        x = ffn_half_step(x, ffn1_norm[l], ffn1_w_gate[l].astype(BF16), ffn1_w_up[l].astype(BF16),
                          ffn1_w_down[l].astype(BF16))
        w_big, w_small = _split_w_in(w_in[l])
        zb = norm_project(x, mix_norm[l], w_big)
        zs = norm_project(x, mix_norm[l], w_small)
        lw = (pool_w[l], pool_scale[l], mlstm_if_bias[l], mlstm_norm[l], nsa_cmp_pos_w[l], nsa_cmp_proj[l])
        zero = lambda *s: jnp.zeros(s, F32)
        yp, st_p = _group_mixers(
            zb[:mp].reshape(bp, tp, -1), zs[:mp].reshape(bp, tp, -1), 0,
            zero(bp, POOL_BUF, 512), zero(bp, MLSTM_HEADS, MLSTM_HD, MLSTM_HD), zero(bp, MLSTM_HEADS, MLSTM_HD),
            zero(bp, MLSTM_HEADS), None, None, None, *lw)
        g_cmp = cache_kv_cmp[l][page_table].reshape(bs, past_len, 2, NSA_KV_HEADS, NSA_HD)
        g_slc = cache_kv_slc[l][page_table].reshape(bs, past_len, 2, NSA_KV_HEADS, NSA_HD)
        ysm, st_s = _group_mixers(
            zb[mp:].reshape(bs, ts, -1), zs[mp:].reshape(bs, ts, -1), past_len,
            state_pool[l], state_mlstm_C[l], state_mlstm_n[l], state_mlstm_m[l],
            g_cmp, g_slc, state_kv_win[l], *lw)
        p_states.append(st_p)
        s_states.append(st_s)
        ys = [jnp.concatenate([a.reshape(mp, -1), b.reshape(ms, -1)], axis=0) for a, b in zip(yp, ysm)]
        x = out_project(x, ys[0], ys[1], ys[2], w_out[l].astype(BF16))
        x = ffn_half_step(x, ffn2_norm[l], ffn2_w_gate[l].astype(BF16), ffn2_w_up[l].astype(BF16),
                          ffn2_w_down[l].astype(BF16), gf=final_norm if l == depth - 1 else None)
    y_prompt = x[:mp].reshape(bp, tp, d)
    y_sample = x[mp:].reshape(bs, ts, d)
    p_out = [jnp.stack(a) for a in zip(*p_states)]
    s_out = [jnp.stack(a) for a in zip(*s_states)]
    return (y_prompt, y_sample, *p_out, *s_out)
```

```python
import functools
import math

import jax
import jax.numpy as jnp
from jax import lax
from jax.experimental import pallas as pl
from jax.experimental.pallas import tpu as pltpu

F32 = jnp.float32
BF16 = jnp.bfloat16
EPS = 1e-6

VMEM_LIMIT_BYTES = 56 * 1024 * 1024
LANES = 128

POOL_WINDOWS = (2, 4, 8, 16)
POOL_BUF = 15
MLSTM_HEADS = 4
MLSTM_HD = 128
MLSTM_CHUNK = 64
NSA_HD = 128
NSA_HEADS = 8
NSA_KV_HEADS = 2
NSA_GROUP = 4
CMP_LEN = 32
CMP_STRIDE = 16
SEL_BLOCK = 64
SEL_TOPN = 16
WINDOW = 512
FORCE_BONUS = 1.0e4

NEG_MASK = -1.0e30
NT = (((1,), (1,)), ((), ()))
TN = (((0,), (0,)), ((), ()))


def _pick_tile(n, pref):
    t = pref
    while t > 8 and n % t:
        t //= 2
    assert n % t == 0, (n, pref)
    return t


def _params(*sem):
    return pltpu.CompilerParams(dimension_semantics=sem, vmem_limit_bytes=VMEM_LIMIT_BYTES)


def _rms_rows(x, g):
    ms = jnp.mean(x * x, axis=-1, keepdims=True)
    return x * lax.rsqrt(ms + EPS) * g


def _ffn_body(x_ref, g_ref, wg_ref, wu_ref, wd_ref, gf_ref, o_ref, n_scr, *, final_norm):
    f = pl.program_id(1)

    @pl.when(f == 0)
    def _():
        x = x_ref[...]
        n_scr[...] = _rms_rows(x, g_ref[...]).astype(BF16)
        o_ref[...] = x

    n = n_scr[...]
    hg = jnp.dot(n, wg_ref[...], preferred_element_type=F32)
    hu = jnp.dot(n, wu_ref[...], preferred_element_type=F32)
    h = (hg * jax.nn.sigmoid(hg) * hu).astype(BF16)
    o_ref[...] += 0.5 * jnp.dot(h, wd_ref[...], preferred_element_type=F32)

    if final_norm:
        @pl.when(f == pl.num_programs(1) - 1)
        def _():
            o_ref[...] = _rms_rows(o_ref[...], gf_ref[...])


def ffn_half_step(x, g, wg, wu, wd, gf=None):
    m, d = x.shape
    fdim = wg.shape[1]
    tm = _pick_tile(m, 512)
    tf = _pick_tile(fdim, 512)
    final_norm = gf is not None
    if gf is None:
        gf = g
    return pl.pallas_call(
        functools.partial(_ffn_body, final_norm=final_norm),
        out_shape=jax.ShapeDtypeStruct((m, d), F32),
        grid=(m // tm, fdim // tf),
        in_specs=[
            pl.BlockSpec((tm, d), lambda i, f: (i, 0)),
            pl.BlockSpec((1, d), lambda i, f: (0, 0)),
            pl.BlockSpec((d, tf), lambda i, f: (0, f)),
            pl.BlockSpec((d, tf), lambda i, f: (0, f)),
            pl.BlockSpec((tf, d), lambda i, f: (f, 0)),
            pl.BlockSpec((1, d), lambda i, f: (0, 0)),
        ],
        out_specs=pl.BlockSpec((tm, d), lambda i, f: (i, 0)),
        scratch_shapes=[pltpu.VMEM((tm, d), BF16)],
        compiler_params=_params("parallel", "arbitrary"),
        name="ffn_half_step",
    )(x, g.reshape(1, d), wg, wu, wd, gf.reshape(1, d))


def _inproj_body(x_ref, g_ref, w_ref, o_ref, n_scr):
    @pl.when(pl.program_id(1) == 0)
    def _():
        n_scr[...] = _rms_rows(x_ref[...], g_ref[...]).astype(BF16)

    o_ref[...] = jnp.dot(n_scr[...], w_ref[...], preferred_element_type=F32)


def norm_project(x, g, w):
    m, d = x.shape
    n = w.shape[1]
    tm = _pick_tile(m, 1024)
    tn = _pick_tile(n, 512)
    return pl.pallas_call(
        _inproj_body,
        out_shape=jax.ShapeDtypeStruct((m, n), F32),
        grid=(m // tm, n // tn),
        in_specs=[
            pl.BlockSpec((tm, d), lambda i, j: (i, 0)),
            pl.BlockSpec((1, d), lambda i, j: (0, 0)),
            pl.BlockSpec((d, tn), lambda i, j: (0, j)),
        ],
        out_specs=pl.BlockSpec((tm, tn), lambda i, j: (i, j)),
        scratch_shapes=[pltpu.VMEM((tm, d), BF16)],
        compiler_params=_params("parallel", "arbitrary"),
        name="norm_project",
    )(x, g.reshape(1, d), w)


def _outproj_body(x_ref, ya_ref, yb_ref, yc_ref, wa_ref, wb_ref, wc_ref, o_ref):
    acc = x_ref[...]
    acc += jnp.dot(ya_ref[...].astype(BF16), wa_ref[...], preferred_element_type=F32)
    acc += jnp.dot(yb_ref[...].astype(BF16), wb_ref[...], preferred_element_type=F32)
    acc += jnp.dot(yc_ref[...].astype(BF16), wc_ref[...], preferred_element_type=F32)
    o_ref[...] = acc


def out_project(x, y_pool, y_mlstm, y_nsa, w_out):
    m, d = x.shape
    da, db, dc = y_pool.shape[1], y_mlstm.shape[1], y_nsa.shape[1]
    assert da == db and dc % da == 0
    tm = _pick_tile(m, 512)
    tn = _pick_tile(d, 1024)
    return pl.pallas_call(
        _outproj_body,
        out_shape=jax.ShapeDtypeStruct((m, d), F32),
        grid=(m // tm, d // tn),
        in_specs=[
            pl.BlockSpec((tm, tn), lambda i, j: (i, j)),
            pl.BlockSpec((tm, da), lambda i, j: (i, 0)),
            pl.BlockSpec((tm, db), lambda i, j: (i, 0)),
            pl.BlockSpec((tm, dc), lambda i, j: (i, 0)),
            pl.BlockSpec((da, tn), lambda i, j: (0, j)),
            pl.BlockSpec((db, tn), lambda i, j: (1, j)),
            pl.BlockSpec((dc, tn), lambda i, j: ((da + db) // dc, j)),
        ],
        out_specs=pl.BlockSpec((tm, tn), lambda i, j: (i, j)),
        compiler_params=_params("parallel", "arbitrary"),
        name="out_project",
    )(x, y_pool, y_mlstm, y_nsa, w_out, w_out, w_out)


POOL_HALO = 16


def _pool_group(load, g, n_avail, w_ref, sc_ref):
    w = POOL_WINDOWS[g]
    z = load(0)
    acc = z
    for j in range(1, w):
        acc = acc + load(j)
    d = acc / jnp.minimum(n_avail, w).astype(F32) - z
    lead = d.shape[:-1]
    gd = d.shape[-1]
    y = jnp.dot(d.reshape(-1, gd).astype(BF16), w_ref[g].astype(BF16), preferred_element_type=F32)
    return (y * sc_ref[:, g * gd:(g + 1) * gd]).reshape(*lead, gd)


def _pool_prompt_body(z_ref, w_ref, sc_ref, o_ref, full_scr, *, chunk):
    seq, pd = z_ref.shape
    gd = pd // len(POOL_WINDOWS)
    full_scr[0:POOL_HALO, :] = jnp.zeros((POOL_HALO, pd), F32)
    full_scr[POOL_HALO:POOL_HALO + seq, :] = z_ref[...]
    for c in range(seq // chunk):
        n_avail = c * chunk + 1 + lax.broadcasted_iota(jnp.int32, (chunk, gd), 0)
        for g in range(len(POOL_WINDOWS)):
            load = lambda j: full_scr[pl.ds(POOL_HALO + c * chunk - j, chunk), g * gd:(g + 1) * gd]
            o_ref[c * chunk:(c + 1) * chunk, g * gd:(g + 1) * gd] = _pool_group(load, g, n_avail, w_ref, sc_ref)


def pool_prompt(zb, pool_w, pool_scale, *, batch, seq, col):
    pd = pool_scale.shape[0]
    chunk = _pick_tile(seq, 256)
    return pl.pallas_call(
        functools.partial(_pool_prompt_body, chunk=chunk),
        out_shape=jax.ShapeDtypeStruct((batch * seq, pd), F32),
        grid=(batch,),
        in_specs=[
            pl.BlockSpec((seq, pd), lambda b: (b, col // pd)),
            pl.BlockSpec(pool_w.shape, lambda b: (0, 0, 0)),
            pl.BlockSpec((1, pd), lambda b: (0, 0)),
        ],
        out_specs=pl.BlockSpec((seq, pd), lambda b: (b, 0)),
        scratch_shapes=[pltpu.VMEM((POOL_HALO + seq, pd), F32)],
        compiler_params=_params("parallel"),
        name="pool_prompt",
    )(zb, pool_w, pool_scale.reshape(1, pd))


def _pool_sample_body(full_ref, w_ref, sc_ref, o_ref, *, pos0):
    bt, rows, pd = full_ref.shape
    ts = rows - POOL_HALO
    gd = pd // len(POOL_WINDOWS)
    n_avail = pos0 + 1 + lax.broadcasted_iota(jnp.int32, (bt, ts, gd), 1)
    for g in range(len(POOL_WINDOWS)):
        load = lambda j: full_ref[:, pl.ds(POOL_HALO - j, ts), g * gd:(g + 1) * gd]
        o_ref[:, g * gd:(g + 1) * gd] = _pool_group(load, g, n_avail, w_ref, sc_ref).reshape(bt * ts, gd)


def pool_sample(full, pool_w, pool_scale, *, pos0):
    batch, rows, pd = full.shape
    ts = rows - POOL_HALO
    assert ts % 8 == 0
    bt = math.gcd(batch, 32)
    return pl.pallas_call(
        functools.partial(_pool_sample_body, pos0=pos0),
        out_shape=jax.ShapeDtypeStruct((batch * ts, pd), F32),
        grid=(batch // bt,),
        in_specs=[
            pl.BlockSpec((bt, rows, pd), lambda b: (b, 0, 0)),
            pl.BlockSpec(pool_w.shape, lambda b: (0, 0, 0)),
            pl.BlockSpec((1, pd), lambda b: (0, 0)),
        ],
        out_specs=pl.BlockSpec((bt * ts, pd), lambda b: (b, 0)),
        compiler_params=_params("parallel"),
        name="pool_sample",
    )(full, pool_w, pool_scale.reshape(1, pd))


def _log_sigmoid(x):
    return jnp.minimum(x, 0.0) - jnp.log1p(jnp.exp(-jnp.abs(x)))


def _mlstm_body(q_ref, k_ref, v_ref, og_ref, g_ref, bias_ref, gn_ref, c0_ref, n0_ref, m0_ref,
                y_ref, c_ref, n_ref, m_ref):
    L = q_ref.shape[0]
    H, D = MLSTM_HEADS, MLSTM_HD
    hi = lax.Precision.HIGHEST

    @pl.when(pl.program_id(1) == 0)
    def _():
        c_ref[...] = c0_ref[...]
        n_ref[...] = n0_ref[...]
        m_ref[...] = m0_ref[...]

    gz = g_ref[...] + bias_ref[...]
    sel = (lax.broadcasted_iota(jnp.int32, (8, LANES), 0) == lax.broadcasted_iota(jnp.int32, (8, LANES), 1)).astype(F32)
    gz_rows = lax.dot_general(sel, gz, NT, precision=hi, preferred_element_type=F32)
    li = lax.broadcasted_iota(jnp.int32, (L, L), 0)
    si = lax.broadcasted_iota(jnp.int32, (L, L), 1)
    causal = li >= si
    b_cols = jnp.dot(causal.astype(F32), _log_sigmoid(gz), precision=hi, preferred_element_type=F32)
    b_rows = jnp.dot(_log_sigmoid(gz_rows), (li <= si).astype(F32), precision=hi, preferred_element_type=F32)
    m_all = m_ref[0]
    lane = lax.broadcasted_iota(jnp.int32, (1, LANES), 1)

    for h in range(H):
        cols = slice(h * D, (h + 1) * D)
        bc = b_cols[:, H + h:H + h + 1]
        ic = gz[:, h:h + 1]
        br = b_rows[H + h:H + h + 1, :]
        ir = gz_rows[h:h + 1, :]
        m_prev = m_all[:, h:h + 1]
        dmat = jnp.where(causal, bc - br + ir, NEG_MASK)
        inter = bc + m_prev
        m_t = jnp.maximum(inter, jnp.max(dmat, axis=1, keepdims=True))
        qh = q_ref[:, cols]
        kh = k_ref[:, cols] * (D ** -0.5)
        qb, kb, vb = qh.astype(BF16), kh.astype(BF16), v_ref[:, cols].astype(BF16)
        S = lax.dot_general(qb, kb, NT, preferred_element_type=F32) * jnp.exp(dmat - m_t)
        a_inter = jnp.exp(inter - m_t)
        ch = c_ref[0, h]
        nh = n_ref[0, h:h + 1, :]
        num = (jnp.dot(S.astype(BF16), vb, preferred_element_type=F32)
               + a_inter * lax.dot_general(qb, ch.astype(BF16), NT, preferred_element_type=F32))
        den = jnp.sum(S, axis=1, keepdims=True) + a_inter * jnp.sum(qh * nh, axis=1, keepdims=True)
        den = jnp.maximum(jnp.abs(den), jnp.exp(-m_t))
        hh = num / den
        m_new = m_t[L - 1:L, :]
        b_last = bc[L - 1:L, :]
        decay = jnp.exp(b_last + m_prev - m_new)
        w_col = jnp.exp(b_last - bc + ic - m_new)
        vw = (v_ref[:, cols] * w_col).astype(BF16)
        c_ref[0, h] = decay * ch + lax.dot_general(vw, kb, TN, preferred_element_type=F32)
        n_ref[0, h:h + 1, :] = decay * nh + jnp.sum(kh * w_col, axis=0, keepdims=True)
        m_all = jnp.where(lane == h, m_new, m_all)
        mu = jnp.mean(hh, axis=1, keepdims=True)
        var = jnp.mean(jnp.square(hh - mu), axis=1, keepdims=True)
        hn = (hh - mu) * lax.rsqrt(var + EPS) * gn_ref[:, cols]
        y_ref[:, cols] = jax.nn.sigmoid(og_ref[:, cols]) * hn
    m_ref[0] = m_all


def mlstm_mix(zb, zs, if_bias, mnorm, c0, n0, m0, *, row0, batch, seq, col_q):
    H, D = MLSTM_HEADS, MLSTM_HD
    dim = H * D
    L = math.gcd(seq, MLSTM_CHUNK)
    nc = seq // L
    assert L % 8 == 0 and row0 % L == 0 and col_q % dim == 0
    r0 = row0 // L
    cq = col_q // dim
    bias = jnp.pad(if_bias, (0, LANES - if_bias.shape[0])).reshape(1, LANES)
    m0p = jnp.pad(m0, ((0, 0), (0, LANES - H))).reshape(batch, 1, LANES)
    row = lambda b, c: r0 + b * nc + c
    y, c_out, n_out, m_out = pl.pallas_call(
        _mlstm_body,
        out_shape=(jax.ShapeDtypeStruct((batch * seq, dim), F32),
                   jax.ShapeDtypeStruct((batch, H, D, D), F32),
                   jax.ShapeDtypeStruct((batch, H, D), F32),
                   jax.ShapeDtypeStruct((batch, 1, LANES), F32)),
        grid=(batch, nc),
        in_specs=[
            pl.BlockSpec((L, dim), lambda b, c: (row(b, c), cq)),
            pl.BlockSpec((L, dim), lambda b, c: (row(b, c), cq + 1)),
            pl.BlockSpec((L, dim), lambda b, c: (row(b, c), cq + 2)),
            pl.BlockSpec((L, dim), lambda b, c: (row(b, c), cq + 3)),
            pl.BlockSpec((L, LANES), lambda b, c: (row(b, c), 0)),
            pl.BlockSpec((1, LANES), lambda b, c: (0, 0)),
            pl.BlockSpec((1, dim), lambda b, c: (0, 0)),
            pl.BlockSpec((1, H, D, D), lambda b, c: (b, 0, 0, 0)),
            pl.BlockSpec((1, H, D), lambda b, c: (b, 0, 0)),
            pl.BlockSpec((1, 1, LANES), lambda b, c: (b, 0, 0)),
        ],
        out_specs=(
            pl.BlockSpec((L, dim), lambda b, c: (b * nc + c, 0)),
            pl.BlockSpec((1, H, D, D), lambda b, c: (b, 0, 0, 0)),
            pl.BlockSpec((1, H, D), lambda b, c: (b, 0, 0)),
            pl.BlockSpec((1, 1, LANES), lambda b, c: (b, 0, 0)),
        ),
        compiler_params=_params("parallel", "arbitrary"),
        name="mlstm_mix",
    )(zb, zb, zb, zb, zs, bias, mnorm.reshape(1, dim), c0, n0, m0p)
    return y, c_out, n_out, m_out[:, 0, :H]


KEY_TILE = 128
SEL_COLS = 64
POS_HI, POS_LO = SEL_COLS, SEL_COLS + 1
NEG_SEL = -1.0e9


def _slope(h):
    return 2.0 ** (-(8.0 / NSA_HEADS) * (h + 1))


def _key_features(pos, onehot):
    lane = lax.broadcasted_iota(jnp.int32, pos.shape, 1)
    hi = lax.shift_right_logical(pos, 6)
    lo = jnp.bitwise_and(pos, SEL_BLOCK - 1)
    f = jnp.where(lane == POS_HI, hi.astype(F32), jnp.where(lane == POS_LO, lo.astype(F32), 0.0))
    if onehot:
        f = jnp.where(lane == hi, 1.0, f)
    return f


def _query_features(shape, h):
    lane = lax.broadcasted_iota(jnp.int32, shape, 1)
    return jnp.where(lane == POS_HI, SEL_BLOCK * _slope(h), jnp.where(lane == POS_LO, _slope(h), 0.0))


def _compress_block_rows(chunks_ref, pw_ref, kv, g):
    row_w = 4 * NSA_HD
    c0 = (kv * NSA_KV_HEADS + g) * NSA_HD
    cols = slice(g * NSA_HD, (g + 1) * NSA_HD)
    a0 = a1 = None
    for j in range(CMP_STRIDE):
        rows = chunks_ref[:, j * row_w + c0:j * row_w + c0 + NSA_HD]
        t0 = rows * pw_ref[kv, j:j + 1, cols]
        t1 = rows * pw_ref[kv, CMP_STRIDE + j:CMP_STRIDE + j + 1, cols]
        a0 = t0 if a0 is None else a0 + t0
        a1 = t1 if a1 is None else a1 + t1
    return a0, a1


def _finish_compress(a0, a1_scr, proj_ref, kcmp_aug, vcmp, kv, g, n_ch, ncp):
    acc = a0 + a1_scr[pl.ds(1, n_ch), :]
    c = jnp.dot(acc.astype(BF16), proj_ref[kv, g].astype(BF16), preferred_element_type=F32).astype(BF16)
    if kv == 0:
        kcmp_aug[g, 0:n_ch, 0:NSA_HD] = c
        n = lax.broadcasted_iota(jnp.int32, (ncp, LANES), 0)
        kcmp_aug[g, :, NSA_HD:2 * NSA_HD] = _key_features(n * CMP_STRIDE + (CMP_LEN - 1), False).astype(BF16)
    else:
        vcmp[g, 0:n_ch, :] = c


def _masked_softmax(s, mask):
    s = jnp.where(mask, s, NEG_MASK)
    m = jnp.max(s, axis=1, keepdims=True)
    e = jnp.where(mask, jnp.exp(s - m), 0.0)
    return e / jnp.maximum(jnp.sum(e, axis=1, keepdims=True), 1e-30)


def _select_blocks(psum, t0, n_cmp, n_sel):
    ncp = psum.shape[1]
    nsp = -(-n_sel // 8) * 8
    j = lax.broadcasted_iota(jnp.int32, (nsp, ncp), 0)
    n = lax.broadcasted_iota(jnp.int32, (nsp, ncp), 1)
    cover = ((n * CMP_STRIDE < j * SEL_BLOCK + SEL_BLOCK) & (n * CMP_STRIDE + CMP_LEN > j * SEL_BLOCK)
             & (n < n_cmp)).astype(F32)
    imp = lax.dot_general(cover, psum, NT, precision=lax.Precision.HIGHEST, preferred_element_type=F32)
    jq = lax.broadcasted_iota(jnp.int32, (nsp, LANES), 0)
    t = t0 + lax.broadcasted_iota(jnp.int32, (nsp, LANES), 1)
    cur = lax.shift_right_logical(t, 6)
    forced = (jq == 0) | (jq == cur) | (jq == cur - 1)
    valid = (jq * SEL_BLOCK <= t) & (jq < n_sel)
    score = jnp.where(valid, imp + jnp.where(forced, FORCE_BONUS, 0.0), -jnp.inf)
    jf = jq.astype(F32)
    sel = jnp.zeros((nsp, LANES), F32)
    for _ in range(min(SEL_TOPN, n_sel)):
        mx = jnp.max(score, axis=0, keepdims=True)
        first = jnp.min(jnp.where(score == mx, jf, 1.0e9), axis=0, keepdims=True)
        pick = jf == first
        sel = jnp.where(pick, 1.0, sel)
        score = jnp.where(pick, -jnp.inf, score)
    bias = jnp.where((sel > 0.5) | (jq >= n_sel), 0.0, NEG_SEL)
    bias = jnp.concatenate([bias, jnp.zeros((LANES - nsp, LANES), F32)], axis=0)
    return bias.T


def _cmp_branch(q_plain, kcmp_aug_g, vcmp_g, t_rows, n_cmp):
    s = lax.dot_general(q_plain, kcmp_aug_g, NT, preferred_element_type=F32)
    n = lax.broadcasted_iota(jnp.int32, s.shape, 1)
    mask = (n * CMP_STRIDE + (CMP_LEN - 1) <= t_rows) & (n < n_cmp)
    p = _masked_softmax(s, mask)
    return jnp.dot(p.astype(BF16), vcmp_g, preferred_element_type=F32), p


def _online_step(state, s, v):
    m, l, acc = state
    m_new = jnp.maximum(m, jnp.max(s, axis=1, keepdims=True))
    alpha = jnp.exp(m - m_new)
    p = jnp.exp(s - m_new)
    l = alpha * l + jnp.sum(p, axis=1, keepdims=True)
    acc = alpha * acc + jnp.dot(p.astype(BF16), v, preferred_element_type=F32)
    return m_new, l, acc


def _stack_heads(q_heads, feats):
    return jnp.concatenate(
        [jnp.concatenate([q, f.astype(BF16)], axis=1) for q, f in zip(q_heads, feats)], axis=0)


def _write_gated(o_ref, gates, g, rows, o_cmp, o_s, o_w):
    for r in range(NSA_GROUP):
        h = g * NSA_GROUP + r
        sl = slice(r * rows, (r + 1) * rows)
        c = 8 + h
        o = gates[:, c:c + 1] * o_cmp[sl] + gates[:, c + 8:c + 9] * o_s[sl] + gates[:, c + 16:c + 17] * o_w[sl]
        o_ref[:, h * NSA_HD:(h + 1) * NSA_HD] = o


def _nsa_prompt_body(q_ref, kc_ref, ks_ref, kw_ref, gate_ref, pw_ref, proj_ref, o_ref,
                     ks_aug, vs, kw_aug, vw, kcmp_aug, vcmp, a1_scr, *, seq):
    i = pl.program_id(1)
    tq = KEY_TILE
    n_ch = seq // CMP_STRIDE
    n_cmp = n_ch - 1
    n_sel = seq // SEL_BLOCK
    ncp = kcmp_aug.shape[1]
    G, HD = NSA_GROUP, NSA_HD

    @pl.when(i == 0)
    def _build():
        pos = lax.broadcasted_iota(jnp.int32, (seq, LANES), 0)
        f_sel = _key_features(pos, True).astype(BF16)
        f_win = _key_features(pos, False).astype(BF16)
        for g in range(NSA_KV_HEADS):
            ks_aug[g, :, 0:HD] = ks_ref[:, g * HD:(g + 1) * HD].astype(BF16)
            ks_aug[g, :, HD:2 * HD] = f_sel
            vs[g] = ks_ref[:, (2 + g) * HD:(3 + g) * HD].astype(BF16)
            kw_aug[g, :, 0:HD] = kw_ref[:, g * HD:(g + 1) * HD].astype(BF16)
            kw_aug[g, :, HD:2 * HD] = f_win
            vw[g] = kw_ref[:, (2 + g) * HD:(3 + g) * HD].astype(BF16)
        kcmp_aug[...] = jnp.zeros(kcmp_aug.shape, BF16)
        vcmp[...] = jnp.zeros(vcmp.shape, BF16)
        a1_scr[n_ch:n_ch + 8, :] = jnp.zeros((8, HD), F32)
        for kv in range(2):
            for g in range(NSA_KV_HEADS):
                a0, a1 = _compress_block_rows(kc_ref, pw_ref, kv, g)
                a1_scr[0:n_ch, :] = a1
                _finish_compress(a0, a1_scr, proj_ref, kcmp_aug, vcmp, kv, g, n_ch, ncp)

    t0 = i * tq
    R = G * tq
    row = lax.broadcasted_iota(jnp.int32, (R, LANES), 0)
    t_rows = t0 + jnp.bitwise_and(row, tq - 1)
    lane = lax.broadcasted_iota(jnp.int32, (R, LANES), 1)
    gates = jax.nn.sigmoid(gate_ref[...])
    scale = HD ** -0.5

    for g in range(NSA_KV_HEADS):
        q_heads = [(q_ref[:, (g * G + r) * HD:(g * G + r + 1) * HD] * scale).astype(BF16) for r in range(G)]
        feats = [_query_features((tq, LANES), g * G + r) for r in range(G)]
        q_plain = _stack_heads(q_heads, feats)

        t_c = t_rows if ncp == LANES else t0 + jnp.bitwise_and(lax.broadcasted_iota(jnp.int32, (R, ncp), 0), tq - 1)
        o_cmp, p_c = _cmp_branch(q_plain, kcmp_aug[g], vcmp[g], t_c, n_cmp)
        psum = p_c[0:tq] + p_c[tq:2 * tq] + p_c[2 * tq:3 * tq] + p_c[3 * tq:4 * tq]
        bias = _select_blocks(psum, t0, n_cmp, n_sel)
        q_sel = _stack_heads(q_heads, [f + bias for f in feats])

        init = (jnp.full((R, 1), NEG_MASK, F32), jnp.zeros((R, 1), F32), jnp.zeros((R, HD), F32))

        def sel_body(kt, st):
            off = pl.multiple_of(kt * KEY_TILE, KEY_TILE)
            s = lax.dot_general(q_sel, ks_aug[g, pl.ds(off, KEY_TILE), :], NT, preferred_element_type=F32)
            return _online_step(st, s, vs[g, pl.ds(off, KEY_TILE), :])

        st = lax.fori_loop(0, i, sel_body, init)
        off = pl.multiple_of(t0, KEY_TILE)
        s = lax.dot_general(q_sel, ks_aug[g, pl.ds(off, KEY_TILE), :], NT, preferred_element_type=F32)
        s = jnp.where(t0 + lane <= t_rows, s, NEG_MASK)
        _, l_s, acc_s = _online_step(st, s, vs[g, pl.ds(off, KEY_TILE), :])

        def win_body(kt, st):
            off = pl.multiple_of(kt * KEY_TILE, KEY_TILE)
            s = lax.dot_general(q_plain, kw_aug[g, pl.ds(off, KEY_TILE), :], NT, preferred_element_type=F32)
            kpos = kt * KEY_TILE + lane
            s = jnp.where((kpos <= t_rows) & (t_rows - kpos <= WINDOW), s, NEG_MASK)
            return _online_step(st, s, vw[g, pl.ds(off, KEY_TILE), :])

        _, l_w, acc_w = lax.fori_loop(jnp.maximum(i - WINDOW // KEY_TILE, 0), i + 1, win_body, init)

        _write_gated(o_ref, gates, g, tq, o_cmp, acc_s / l_s, acc_w / l_w)


def nsa_prompt(zb, zs, kv_cmp, cmp_pos_w, cmp_proj, *, batch, seq, col_q, col_slc, col_win):
    assert seq % KEY_TILE == 0 and seq // SEL_BLOCK <= SEL_COLS
    nq = seq // KEY_TILE
    n_ch = seq // CMP_STRIDE
    ncp = -(-n_ch // LANES) * LANES
    qw, kvw = NSA_HEADS * NSA_HD, 4 * NSA_HD
    pw = cmp_pos_w.reshape(2, CMP_LEN, 2 * NSA_HD)
    return pl.pallas_call(
        functools.partial(_nsa_prompt_body, seq=seq),
        out_shape=jax.ShapeDtypeStruct((batch * seq, qw), F32),
        grid=(batch, nq),
        in_specs=[
            pl.BlockSpec((KEY_TILE, qw), lambda b, i: (b * nq + i, col_q // qw)),
            pl.BlockSpec((n_ch, CMP_STRIDE * kvw), lambda b, i: (b, 0)),
            pl.BlockSpec((seq, kvw), lambda b, i: (b, col_slc // kvw)),
            pl.BlockSpec((seq, kvw), lambda b, i: (b, col_win // kvw)),
            pl.BlockSpec((KEY_TILE, LANES), lambda b, i: (b * nq + i, 0)),
            pl.BlockSpec((2, CMP_LEN, 2 * NSA_HD), lambda b, i: (0, 0, 0)),
            pl.BlockSpec((2, NSA_KV_HEADS, NSA_HD, NSA_HD), lambda b, i: (0, 0, 0, 0)),
        ],
        out_specs=pl.BlockSpec((KEY_TILE, qw), lambda b, i: (b * nq + i, 0)),
        scratch_shapes=[
            pltpu.VMEM((NSA_KV_HEADS, seq, 2 * NSA_HD), BF16),
            pltpu.VMEM((NSA_KV_HEADS, seq, NSA_HD), BF16),
            pltpu.VMEM((NSA_KV_HEADS, seq, 2 * NSA_HD), BF16),
            pltpu.VMEM((NSA_KV_HEADS, seq, NSA_HD), BF16),
            pltpu.VMEM((NSA_KV_HEADS, ncp, 2 * NSA_HD), BF16),
            pltpu.VMEM((NSA_KV_HEADS, ncp, NSA_HD), BF16),
            pltpu.VMEM((n_ch + 8, NSA_HD), F32),
        ],
        compiler_params=_params("parallel", "arbitrary"),
        name="nsa_prompt",
    )(zb, kv_cmp.reshape(-1, CMP_STRIDE * kvw), zb, zb, zs, pw, cmp_proj)


def _nsa_sample_body(pt_ref, q_ref, ksn_ref, kwn_ref, gate_ref, wprev_ref, pw_ref, proj_ref, *rest,
                     ts, past, n_pages):
    del pt_ref
    cmp_pages = rest[:n_pages]
    slc_pages = rest[n_pages:2 * n_pages]
    o_ref = rest[2 * n_pages]
    ks_aug, vs, kw_aug, vw, kcmp_aug, vcmp, a0_scr, a1_scr = rest[2 * n_pages + 1:]
    G, HD = NSA_GROUP, NSA_HD
    page = slc_pages[0].shape[0]
    kp = ks_aug.shape[1]
    wprev = wprev_ref.shape[0]
    wp = kw_aug.shape[1]
    win_pos0 = past - wprev
    n_ch = (past + ts) // CMP_STRIDE
    n_cmp = n_ch - 1
    n_sel = -(-(past + ts) // SEL_BLOCK)
    ncp = kcmp_aug.shape[1]
    ch_per_page = page // CMP_STRIDE

    @pl.when(pl.program_id(0) == 0)
    def _constants():
        pos = lax.broadcasted_iota(jnp.int32, (kp, LANES), 0)
        f_sel = _key_features(pos, True).astype(BF16)
        posw = win_pos0 + lax.broadcasted_iota(jnp.int32, (wp, LANES), 0)
        f_win = _key_features(posw, False).astype(BF16)
        for g in range(NSA_KV_HEADS):
            ks_aug[g, :, HD:2 * HD] = f_sel
            kw_aug[g, :, HD:2 * HD] = f_win
        kcmp_aug[...] = jnp.zeros(kcmp_aug.shape, BF16)
        vcmp[...] = jnp.zeros(vcmp.shape, BF16)
        a1_scr[...] = jnp.zeros(a1_scr.shape, F32)

    def with_tail(new_rows):
        return jnp.concatenate([new_rows, jnp.zeros((KEY_TILE - ts, HD), F32)], axis=0).astype(BF16)

    for g in range(NSA_KV_HEADS):
        for p in range(n_pages):
            rows = slice(p * page, (p + 1) * page)
            ks_aug[g, rows, 0:HD] = slc_pages[p][:, g * HD:(g + 1) * HD].astype(BF16)
            vs[g, rows, :] = slc_pages[p][:, (2 + g) * HD:(3 + g) * HD].astype(BF16)
        ks_aug[g, past:past + KEY_TILE, 0:HD] = with_tail(ksn_ref[:, g * HD:(g + 1) * HD])
        vs[g, past:past + KEY_TILE, :] = with_tail(ksn_ref[:, (2 + g) * HD:(3 + g) * HD])
        kw_aug[g, 0:wprev, 0:HD] = wprev_ref[:, g * HD:(g + 1) * HD].astype(BF16)
        vw[g, 0:wprev, :] = wprev_ref[:, (2 + g) * HD:(3 + g) * HD].astype(BF16)
        kw_aug[g, wprev:wprev + KEY_TILE, 0:HD] = with_tail(kwn_ref[:, g * HD:(g + 1) * HD])
        vw[g, wprev:wprev + KEY_TILE, :] = with_tail(kwn_ref[:, (2 + g) * HD:(3 + g) * HD])

    for kv in range(2):
        for g in range(NSA_KV_HEADS):
            for p in range(n_pages):
                a0, a1 = _compress_block_rows(cmp_pages[p], pw_ref, kv, g)
                a0_scr[p * ch_per_page:(p + 1) * ch_per_page, :] = a0
                a1_scr[p * ch_per_page:(p + 1) * ch_per_page, :] = a1
            _finish_compress(a0_scr[0:n_ch, :], a1_scr, proj_ref, kcmp_aug, vcmp, kv, g, n_ch, ncp)

    R = G * ts
    gates = jax.nn.sigmoid(gate_ref[...])
    scale = HD ** -0.5

    def t_of(shape):
        return past + jnp.bitwise_and(lax.broadcasted_iota(jnp.int32, shape, 0), ts - 1)

    def softmax_pv(s, v):
        m = jnp.max(s, axis=1, keepdims=True)
        e = jnp.exp(s - m)
        return jnp.dot(e.astype(BF16), v, preferred_element_type=F32) / jnp.sum(e, axis=1, keepdims=True)

    for g in range(NSA_KV_HEADS):
        q_heads = [(q_ref[:, (g * G + r) * HD:(g * G + r + 1) * HD] * scale).astype(BF16) for r in range(G)]
        feats = [_query_features((ts, LANES), g * G + r) for r in range(G)]
        q_plain = _stack_heads(q_heads, feats)
        o_cmp, p_c = _cmp_branch(q_plain, kcmp_aug[g], vcmp[g], t_of((R, ncp)), n_cmp)
        psum = p_c[0:ts] + p_c[ts:2 * ts] + p_c[2 * ts:3 * ts] + p_c[3 * ts:4 * ts]
        psum = jnp.concatenate([psum, jnp.zeros((LANES - ts, ncp), F32)], axis=0)
        bias = _select_blocks(psum, past, n_cmp, n_sel)[0:ts]
        q_sel = _stack_heads(q_heads, [f + bias for f in feats])

        s = lax.dot_general(q_sel, ks_aug[g], NT, preferred_element_type=F32)
        s = jnp.where(lax.broadcasted_iota(jnp.int32, (R, kp), 1) <= t_of((R, kp)), s, NEG_MASK)
        o_s = softmax_pv(s, vs[g])

        s = lax.dot_general(q_plain, kw_aug[g], NT, preferred_element_type=F32)
        idx = lax.broadcasted_iota(jnp.int32, (R, wp), 1)
        dist = t_of((R, wp)) - (win_pos0 + idx)
        s = jnp.where((idx < wprev + ts) & (dist >= 0) & (dist <= WINDOW), s, NEG_MASK)
        o_w = softmax_pv(s, vw[g])

        _write_gated(o_ref, gates, g, ts, o_cmp, o_s, o_w)


def nsa_sample(zb, zs, cache_cmp, cache_slc, state_win, page_table, cmp_pos_w, cmp_proj, *,
               layer, row0, batch, ts, col_q, col_slc, col_win):
    depth, n_pool, page = cache_cmp.shape[:3]
    n_pages = page_table.shape[1]
    past = n_pages * page
    wprev = state_win.shape[2]
    assert ts & (ts - 1) == 0 and ts <= KEY_TILE and row0 % ts == 0
    assert past % KEY_TILE == 0 and (past + ts) // CMP_STRIDE == past // CMP_STRIDE
    assert page % CMP_STRIDE == 0 and -(-(past + ts) // SEL_BLOCK) <= SEL_COLS and wprev % 16 == 0
    qw, kvw = NSA_HEADS * NSA_HD, 4 * NSA_HD
    n_ch = past // CMP_STRIDE
    ncp = -(-n_ch // LANES) * LANES
    r0 = row0 // ts
    pw = cmp_pos_w.reshape(2, CMP_LEN, 2 * NSA_HD)
    cmp_view = cache_cmp.reshape(depth, n_pool, page // CMP_STRIDE, CMP_STRIDE * kvw)
    slc_view = cache_slc.reshape(depth, n_pool, page, kvw)
    win_view = state_win.reshape(depth, batch, wprev, kvw)

    def page_map(p):
        return lambda b, pt: (layer, pt[b * n_pages + p], 0, 0)

    in_specs = [
        pl.BlockSpec((ts, qw), lambda b, pt: (r0 + b, col_q // qw)),
        pl.BlockSpec((ts, kvw), lambda b, pt: (r0 + b, col_slc // kvw)),
        pl.BlockSpec((ts, kvw), lambda b, pt: (r0 + b, col_win // kvw)),
        pl.BlockSpec((ts, LANES), lambda b, pt: (r0 + b, 0)),
        pl.BlockSpec((None, None, wprev, kvw), lambda b, pt: (layer, b, 0, 0)),
        pl.BlockSpec((2, CMP_LEN, 2 * NSA_HD), lambda b, pt: (0, 0, 0)),
        pl.BlockSpec((2, NSA_KV_HEADS, NSA_HD, NSA_HD), lambda b, pt: (0, 0, 0, 0)),
    ]
    in_specs += [pl.BlockSpec((None, None, page // CMP_STRIDE, CMP_STRIDE * kvw), page_map(p)) for p in range(n_pages)]
    in_specs += [pl.BlockSpec((None, None, page, kvw), page_map(p)) for p in range(n_pages)]
    return pl.pallas_call(
        functools.partial(_nsa_sample_body, ts=ts, past=past, n_pages=n_pages),
        out_shape=jax.ShapeDtypeStruct((batch * ts, qw), F32),
        grid_spec=pltpu.PrefetchScalarGridSpec(
            num_scalar_prefetch=1,
            grid=(batch,),
            in_specs=in_specs,
            out_specs=pl.BlockSpec((ts, qw), lambda b, pt: (b, 0)),
            scratch_shapes=[
                pltpu.VMEM((NSA_KV_HEADS, past + KEY_TILE, 2 * NSA_HD), BF16),
                pltpu.VMEM((NSA_KV_HEADS, past + KEY_TILE, NSA_HD), BF16),
                pltpu.VMEM((NSA_KV_HEADS, wprev + KEY_TILE, 2 * NSA_HD), BF16),
                pltpu.VMEM((NSA_KV_HEADS, wprev + KEY_TILE, NSA_HD), BF16),
                pltpu.VMEM((NSA_KV_HEADS, ncp, 2 * NSA_HD), BF16),
                pltpu.VMEM((NSA_KV_HEADS, ncp, NSA_HD), BF16),
                pltpu.VMEM((n_ch, NSA_HD), F32),
                pltpu.VMEM((n_ch + 8, NSA_HD), F32),
            ],
        ),
        compiler_params=_params("arbitrary"),
        name="nsa_sample",
    )(page_table.reshape(-1), zb, zb, zb, zs, win_view, pw, cmp_proj,
      *([cmp_view] * n_pages), *([slc_view] * n_pages))


ZB_NQ, ZB_POOL, ZB_MQ, ZB_MK, ZB_MV, ZB_MO, ZB_CMP, ZB_SLC, ZB_WIN, ZB_END = (
    0, 1024, 1536, 2048, 2560, 3072, 3584, 4096, 4608, 5120)
W_POOL, W_MI, W_NQ, W_CMP, W_NG, W_END = 0, 2560, 2568, 3592, 5128, 5152


def _split_w_in(w_in_l):
    big = jnp.concatenate([w_in_l[:, W_NQ:W_CMP], w_in_l[:, W_POOL:W_MI], w_in_l[:, W_CMP:W_NG]], axis=1)
    small = jnp.concatenate([w_in_l[:, W_MI:W_NQ], w_in_l[:, W_NG:W_END]], axis=1)
    small = jnp.pad(small, ((0, 0), (0, LANES - small.shape[1])))
    return big.astype(BF16), small.astype(BF16)


def kernel(x_prompt, x_sample, cache_kv_cmp, cache_kv_slc, state_kv_win, state_pool, state_mlstm_C, state_mlstm_n, state_mlstm_m, page_table, ffn1_norm, ffn1_w_gate, ffn1_w_up, ffn1_w_down, mix_norm, w_in, w_out, pool_w, pool_scale, mlstm_if_bias, mlstm_norm, nsa_cmp_pos_w, nsa_cmp_proj, ffn2_norm, ffn2_w_gate, ffn2_w_up, ffn2_w_down, final_norm):
    bp, tp, d = x_prompt.shape
    bs, ts, _ = x_sample.shape
    depth = w_in.shape[0]
    mp, ms = bp * tp, bs * ts
    past_len = page_table.shape[1] * cache_kv_cmp.shape[2]
    pd = pool_scale.shape[1]
    kv_row = (2, NSA_KV_HEADS, NSA_HD)
    x = jnp.concatenate([x_prompt.reshape(mp, d), x_sample.reshape(ms, d)], axis=0)
    zeros = lambda *s: jnp.zeros(s, F32)
    outs = [[] for _ in range(14)]
    for l in range(depth):
        x = ffn_half_step(x, ffn1_norm[l], ffn1_w_gate[l].astype(BF16), ffn1_w_up[l].astype(BF16),
                          ffn1_w_down[l].astype(BF16))
        w_big, w_small = _split_w_in(w_in[l])
        zb = norm_project(x, mix_norm[l], w_big)
        zs = norm_project(x, mix_norm[l], w_small)
        kv_cmp = zb[:, ZB_CMP:ZB_SLC]
        kv_slc = zb[:, ZB_SLC:ZB_WIN]
        kv_win = zb[:, ZB_WIN:ZB_END]
        z_pool = zb[:, ZB_POOL:ZB_MQ]

        yp_pool = pool_prompt(zb, pool_w[l], pool_scale[l], batch=bp, seq=tp, col=ZB_POOL)
        yp_m, p_c, p_n, p_m = mlstm_mix(
            zb, zs, mlstm_if_bias[l], mlstm_norm[l], zeros(bp, MLSTM_HEADS, MLSTM_HD, MLSTM_HD),
            zeros(bp, MLSTM_HEADS, MLSTM_HD), zeros(bp, MLSTM_HEADS), row0=0, batch=bp, seq=tp, col_q=ZB_MQ)
        yp_nsa = nsa_prompt(zb, zs, kv_cmp, nsa_cmp_pos_w[l], nsa_cmp_proj[l], batch=bp, seq=tp,
                            col_q=ZB_NQ, col_slc=ZB_SLC, col_win=ZB_WIN)

        pool_full = jnp.concatenate([zeros(bs, POOL_HALO - POOL_BUF, pd), state_pool[l],
                                     z_pool[mp:].reshape(bs, ts, pd)], axis=1)
        ys_pool = pool_sample(pool_full, pool_w[l], pool_scale[l], pos0=past_len)
        ys_m, s_c, s_n, s_m = mlstm_mix(
            zb, zs, mlstm_if_bias[l], mlstm_norm[l], state_mlstm_C[l], state_mlstm_n[l], state_mlstm_m[l],
            row0=mp, batch=bs, seq=ts, col_q=ZB_MQ)
        ys_nsa = nsa_sample(zb, zs, cache_kv_cmp, cache_kv_slc, state_kv_win, page_table,
                            nsa_cmp_pos_w[l], nsa_cmp_proj[l], layer=l, row0=mp, batch=bs, ts=ts,
                            col_q=ZB_NQ, col_slc=ZB_SLC, col_win=ZB_WIN)

        x = out_project(x, jnp.concatenate([yp_pool, ys_pool]), jnp.concatenate([yp_m, ys_m]),
                        jnp.concatenate([yp_nsa, ys_nsa]), w_out[l].astype(BF16))
        x = ffn_half_step(x, ffn2_norm[l], ffn2_w_gate[l].astype(BF16), ffn2_w_up[l].astype(BF16),
                          ffn2_w_down[l].astype(BF16), gf=final_norm if l == depth - 1 else None)

        wp_rows = min(WINDOW, tp)
        win_s = jnp.concatenate([state_kv_win[l], kv_win[mp:].reshape(bs, ts, *kv_row)], axis=1)
        pool_p = jnp.concatenate([zeros(bp, POOL_BUF, pd), z_pool[:mp].reshape(bp, tp, pd)], axis=1)
        layer_out = (
            kv_cmp[:mp].reshape(bp, tp, *kv_row), kv_slc[:mp].reshape(bp, tp, *kv_row),
            kv_win[:mp].reshape(bp, tp, *kv_row)[:, tp - wp_rows:], pool_p[:, -POOL_BUF:], p_c, p_n, p_m,
            kv_cmp[mp:].reshape(bs, ts, *kv_row), kv_slc[mp:].reshape(bs, ts, *kv_row),
            win_s[:, -min(WINDOW, win_s.shape[1]):], pool_full[:, -POOL_BUF:], s_c, s_n, s_m)
        for acc, a in zip(outs, layer_out):
            acc.append(a)
    y_prompt = x[:mp].reshape(bp, tp, d)
    y_sample = x[mp:].reshape(bs, ts, d)
    return (y_prompt, y_sample, *[jnp.stack(a) for a in outs])
```

```python
import functools
import math

import jax
import jax.numpy as jnp
from jax import lax
from jax.experimental import pallas as pl
from jax.experimental.pallas import tpu as pltpu

F32 = jnp.float32
BF16 = jnp.bfloat16
EPS = 1e-6

VMEM_LIMIT_BYTES = 56 * 1024 * 1024
LANES = 128

POOL_WINDOWS = (2, 4, 8, 16)
POOL_BUF = 15
MLSTM_HEADS = 4
MLSTM_HD = 128
MLSTM_CHUNK = 64
NSA_HD = 128
NSA_HEADS = 8
NSA_KV_HEADS = 2
NSA_GROUP = 4
CMP_LEN = 32
CMP_STRIDE = 16
SEL_BLOCK = 64
SEL_TOPN = 16
WINDOW = 512
FORCE_BONUS = 1.0e4

NEG_MASK = -1.0e30
NT = (((1,), (1,)), ((), ()))
TN = (((0,), (0,)), ((), ()))


def _pick_tile(n, pref):
    t = pref
    while t > 8 and n % t:
        t //= 2
    assert n % t == 0, (n, pref)
    return t


def _params(*sem):
    return pltpu.CompilerParams(dimension_semantics=sem, vmem_limit_bytes=VMEM_LIMIT_BYTES)


def _rms_rows(x, g):
    ms = jnp.mean(x * x, axis=-1, keepdims=True)
    return x * lax.rsqrt(ms + EPS) * g


def _ffn_body(x_ref, g_ref, wg_ref, wu_ref, wd_ref, gf_ref, o_ref, n_scr, *, final_norm):
    f = pl.program_id(1)

    @pl.when(f == 0)
    def _():
        x = x_ref[...]
        n_scr[...] = _rms_rows(x, g_ref[...]).astype(BF16)
        o_ref[...] = x

    n = n_scr[...]
    hg = jnp.dot(n, wg_ref[...], preferred_element_type=F32)
    hu = jnp.dot(n, wu_ref[...], preferred_element_type=F32)
    h = (hg * jax.nn.sigmoid(hg) * hu).astype(BF16)
    o_ref[...] += 0.5 * jnp.dot(h, wd_ref[...], preferred_element_type=F32)

    if final_norm:
        @pl.when(f == pl.num_programs(1) - 1)
        def _():
            o_ref[...] = _rms_rows(o_ref[...], gf_ref[...])


def ffn_half_step(x, g, wg, wu, wd, gf=None):
    m, d = x.shape
    fdim = wg.shape[1]
    tm = _pick_tile(m, 512)
    tf = _pick_tile(fdim, 512)
    final_norm = gf is not None
    if gf is None:
        gf = g
    return pl.pallas_call(
        functools.partial(_ffn_body, final_norm=final_norm),
        out_shape=jax.ShapeDtypeStruct((m, d), F32),
        grid=(m // tm, fdim // tf),
        in_specs=[
            pl.BlockSpec((tm, d), lambda i, f: (i, 0)),
            pl.BlockSpec((1, d), lambda i, f: (0, 0)),
            pl.BlockSpec((d, tf), lambda i, f: (0, f)),
            pl.BlockSpec((d, tf), lambda i, f: (0, f)),
            pl.BlockSpec((tf, d), lambda i, f: (f, 0)),
            pl.BlockSpec((1, d), lambda i, f: (0, 0)),
        ],
        out_specs=pl.BlockSpec((tm, d), lambda i, f: (i, 0)),
        scratch_shapes=[pltpu.VMEM((tm, d), BF16)],
        compiler_params=_params("parallel", "arbitrary"),
        name="ffn_half_step",
    )(x, g.reshape(1, d), wg, wu, wd, gf.reshape(1, d))


def _inproj_body(x_ref, g_ref, w_ref, o_ref, n_scr):
    @pl.when(pl.program_id(1) == 0)
    def _():
        n_scr[...] = _rms_rows(x_ref[...], g_ref[...]).astype(BF16)

    o_ref[...] = jnp.dot(n_scr[...], w_ref[...], preferred_element_type=F32)


def norm_project(x, g, w):
    m, d = x.shape
    n = w.shape[1]
    tm = _pick_tile(m, 1024)
    tn = _pick_tile(n, 512)
    return pl.pallas_call(
        _inproj_body,
        out_shape=jax.ShapeDtypeStruct((m, n), F32),
        grid=(m // tm, n // tn),
        in_specs=[
            pl.BlockSpec((tm, d), lambda i, j: (i, 0)),
            pl.BlockSpec((1, d), lambda i, j: (0, 0)),
            pl.BlockSpec((d, tn), lambda i, j: (0, j)),
        ],
        out_specs=pl.BlockSpec((tm, tn), lambda i, j: (i, j)),
        scratch_shapes=[pltpu.VMEM((tm, d), BF16)],
        compiler_params=_params("parallel", "arbitrary"),
        name="norm_project",
    )(x, g.reshape(1, d), w)


def _outproj_body(x_ref, ya_ref, yb_ref, yc_ref, wa_ref, wb_ref, wc_ref, o_ref):
    acc = x_ref[...]
    acc += jnp.dot(ya_ref[...].astype(BF16), wa_ref[...], preferred_element_type=F32)
    acc += jnp.dot(yb_ref[...].astype(BF16), wb_ref[...], preferred_element_type=F32)
    acc += jnp.dot(yc_ref[...].astype(BF16), wc_ref[...], preferred_element_type=F32)
    o_ref[...] = acc


def out_project(x, y_pool, y_mlstm, y_nsa, w_out):
    m, d = x.shape
    da, db, dc = y_pool.shape[1], y_mlstm.shape[1], y_nsa.shape[1]
    assert da == db and dc % da == 0
    tm = _pick_tile(m, 512)
    tn = _pick_tile(d, 1024)
    return pl.pallas_call(
        _outproj_body,
        out_shape=jax.ShapeDtypeStruct((m, d), F32),
        grid=(m // tm, d // tn),
        in_specs=[
            pl.BlockSpec((tm, tn), lambda i, j: (i, j)),
            pl.BlockSpec((tm, da), lambda i, j: (i, 0)),
            pl.BlockSpec((tm, db), lambda i, j: (i, 0)),
            pl.BlockSpec((tm, dc), lambda i, j: (i, 0)),
            pl.BlockSpec((da, tn), lambda i, j: (0, j)),
            pl.BlockSpec((db, tn), lambda i, j: (1, j)),
            pl.BlockSpec((dc, tn), lambda i, j: ((da + db) // dc, j)),
        ],
        out_specs=pl.BlockSpec((tm, tn), lambda i, j: (i, j)),
        compiler_params=_params("parallel", "arbitrary"),
        name="out_project",
    )(x, y_pool, y_mlstm, y_nsa, w_out, w_out, w_out)


POOL_HALO = 16


def _pool_group(load, g, n_avail, w_ref, sc_ref):
    w = POOL_WINDOWS[g]
    z = load(0)
    acc = z
    for j in range(1, w):
        acc = acc + load(j)
    d = acc / jnp.minimum(n_avail, w).astype(F32) - z
    lead = d.shape[:-1]
    gd = d.shape[-1]
    y = jnp.dot(d.reshape(-1, gd).astype(BF16), w_ref[g].astype(BF16), preferred_element_type=F32)
    return (y * sc_ref[:, g * gd:(g + 1) * gd]).reshape(*lead, gd)


def _pool_prompt_body(z_ref, w_ref, sc_ref, o_ref, full_scr, *, chunk):
    seq, pd = z_ref.shape
    gd = pd // len(POOL_WINDOWS)
    full_scr[0:POOL_HALO, :] = jnp.zeros((POOL_HALO, pd), F32)
    full_scr[POOL_HALO:POOL_HALO + seq, :] = z_ref[...]
    for c in range(seq // chunk):
        n_avail = c * chunk + 1 + lax.broadcasted_iota(jnp.int32, (chunk, gd), 0)
        for g in range(len(POOL_WINDOWS)):
            load = lambda j: full_scr[pl.ds(POOL_HALO + c * chunk - j, chunk), g * gd:(g + 1) * gd]
            o_ref[c * chunk:(c + 1) * chunk, g * gd:(g + 1) * gd] = _pool_group(load, g, n_avail, w_ref, sc_ref)


def pool_prompt(zb, pool_w, pool_scale, *, batch, seq, col):
    pd = pool_scale.shape[0]
    chunk = _pick_tile(seq, 256)
    return pl.pallas_call(
        functools.partial(_pool_prompt_body, chunk=chunk),
        out_shape=jax.ShapeDtypeStruct((batch * seq, pd), F32),
        grid=(batch,),
        in_specs=[
            pl.BlockSpec((seq, pd), lambda b: (b, col // pd)),
            pl.BlockSpec(pool_w.shape, lambda b: (0, 0, 0)),
            pl.BlockSpec((1, pd), lambda b: (0, 0)),
        ],
        out_specs=pl.BlockSpec((seq, pd), lambda b: (b, 0)),
        scratch_shapes=[pltpu.VMEM((POOL_HALO + seq, pd), F32)],
        compiler_params=_params("parallel"),
        name="pool_prompt",
    )(zb, pool_w, pool_scale.reshape(1, pd))


def _pool_sample_body(full_ref, w_ref, sc_ref, o_ref, *, pos0):
    bt, rows, pd = full_ref.shape
    ts = rows - POOL_HALO
    gd = pd // len(POOL_WINDOWS)
    n_avail = pos0 + 1 + lax.broadcasted_iota(jnp.int32, (bt, ts, gd), 1)
    for g in range(len(POOL_WINDOWS)):
        load = lambda j: full_ref[:, pl.ds(POOL_HALO - j, ts), g * gd:(g + 1) * gd]
        o_ref[:, g * gd:(g + 1) * gd] = _pool_group(load, g, n_avail, w_ref, sc_ref).reshape(bt * ts, gd)


def pool_sample(full, pool_w, pool_scale, *, pos0):
    batch, rows, pd = full.shape
    ts = rows - POOL_HALO
    assert ts % 8 == 0
    bt = math.gcd(batch, 32)
    return pl.pallas_call(
        functools.partial(_pool_sample_body, pos0=pos0),
        out_shape=jax.ShapeDtypeStruct((batch * ts, pd), F32),
        grid=(batch // bt,),
        in_specs=[
            pl.BlockSpec((bt, rows, pd), lambda b: (b, 0, 0)),
            pl.BlockSpec(pool_w.shape, lambda b: (0, 0, 0)),
            pl.BlockSpec((1, pd), lambda b: (0, 0)),
        ],
        out_specs=pl.BlockSpec((bt * ts, pd), lambda b: (b, 0)),
        compiler_params=_params("parallel"),
        name="pool_sample",
    )(full, pool_w, pool_scale.reshape(1, pd))


def _log_sigmoid(x):
    return jnp.minimum(x, 0.0) - jnp.log1p(jnp.exp(-jnp.abs(x)))


def _mlstm_body(q_ref, k_ref, v_ref, og_ref, g_ref, bias_ref, gn_ref, c0_ref, n0_ref, m0_ref,
                y_ref, c_ref, n_ref, m_ref):
    L = q_ref.shape[0]
    H, D = MLSTM_HEADS, MLSTM_HD
    hi = lax.Precision.HIGHEST

    @pl.when(pl.program_id(1) == 0)
    def _():
        c_ref[...] = c0_ref[...]
        n_ref[...] = n0_ref[...]
        m_ref[...] = m0_ref[...]

    gz = g_ref[...] + bias_ref[...]
    sel = (lax.broadcasted_iota(jnp.int32, (8, LANES), 0) == lax.broadcasted_iota(jnp.int32, (8, LANES), 1)).astype(F32)
    gz_rows = lax.dot_general(sel, gz, NT, precision=hi, preferred_element_type=F32)
    li = lax.broadcasted_iota(jnp.int32, (L, L), 0)
    si = lax.broadcasted_iota(jnp.int32, (L, L), 1)
    causal = li >= si
    b_cols = jnp.dot(causal.astype(F32), _log_sigmoid(gz), precision=hi, preferred_element_type=F32)
    b_rows = jnp.dot(_log_sigmoid(gz_rows), (li <= si).astype(F32), precision=hi, preferred_element_type=F32)
    m_all = m_ref[0]
    lane = lax.broadcasted_iota(jnp.int32, (1, LANES), 1)

    for h in range(H):
        cols = slice(h * D, (h + 1) * D)
        bc = b_cols[:, H + h:H + h + 1]
        ic = gz[:, h:h + 1]
        br = b_rows[H + h:H + h + 1, :]
        ir = gz_rows[h:h + 1, :]
        m_prev = m_all[:, h:h + 1]
        dmat = jnp.where(causal, bc - br + ir, NEG_MASK)
        inter = bc + m_prev
        m_t = jnp.maximum(inter, jnp.max(dmat, axis=1, keepdims=True))
        qh = q_ref[:, cols]
        kh = k_ref[:, cols] * (D ** -0.5)
        qb, kb, vb = qh.astype(BF16), kh.astype(BF16), v_ref[:, cols].astype(BF16)
        S = lax.dot_general(qb, kb, NT, preferred_element_type=F32) * jnp.exp(dmat - m_t)
        a_inter = jnp.exp(inter - m_t)
        ch = c_ref[0, h]
        nh = n_ref[0, h:h + 1, :]
        num = (jnp.dot(S.astype(BF16), vb, preferred_element_type=F32)
               + a_inter * lax.dot_general(qb, ch.astype(BF16), NT, preferred_element_type=F32))
        den = jnp.sum(S, axis=1, keepdims=True) + a_inter * jnp.sum(qh * nh, axis=1, keepdims=True)
        den = jnp.maximum(jnp.abs(den), jnp.exp(-m_t))
        hh = num / den
        m_new = m_t[L - 1:L, :]
        b_last = bc[L - 1:L, :]
        decay = jnp.exp(b_last + m_prev - m_new)
        w_col = jnp.exp(b_last - bc + ic - m_new)
        vw = (v_ref[:, cols] * w_col).astype(BF16)
        c_ref[0, h] = decay * ch + lax.dot_general(vw, kb, TN, preferred_element_type=F32)
        n_ref[0, h:h + 1, :] = decay * nh + jnp.sum(kh * w_col, axis=0, keepdims=True)
        m_all = jnp.where(lane == h, m_new, m_all)
        mu = jnp.mean(hh, axis=1, keepdims=True)
        var = jnp.mean(jnp.square(hh - mu), axis=1, keepdims=True)
        hn = (hh - mu) * lax.rsqrt(var + EPS) * gn_ref[:, cols]
        y_ref[:, cols] = jax.nn.sigmoid(og_ref[:, cols]) * hn
    m_ref[0] = m_all


def mlstm_mix(zb, zs, if_bias, mnorm, c0, n0, m0, *, row0, batch, seq, col_q):
    H, D = MLSTM_HEADS, MLSTM_HD
    dim = H * D
    L = math.gcd(seq, MLSTM_CHUNK)
    nc = seq // L
    assert L % 8 == 0 and row0 % L == 0 and col_q % dim == 0
    r0 = row0 // L
    cq = col_q // dim
    bias = jnp.pad(if_bias, (0, LANES - if_bias.shape[0])).reshape(1, LANES)
    m0p = jnp.pad(m0, ((0, 0), (0, LANES - H))).reshape(batch, 1, LANES)
    row = lambda b, c: r0 + b * nc + c
    y, c_out, n_out, m_out = pl.pallas_call(
        _mlstm_body,
        out_shape=(jax.ShapeDtypeStruct((batch * seq, dim), F32),
                   jax.ShapeDtypeStruct((batch, H, D, D), F32),
                   jax.ShapeDtypeStruct((batch, H, D), F32),
                   jax.ShapeDtypeStruct((batch, 1, LANES), F32)),
        grid=(batch, nc),
        in_specs=[
            pl.BlockSpec((L, dim), lambda b, c: (row(b, c), cq)),
            pl.BlockSpec((L, dim), lambda b, c: (row(b, c), cq + 1)),
            pl.BlockSpec((L, dim), lambda b, c: (row(b, c), cq + 2)),
            pl.BlockSpec((L, dim), lambda b, c: (row(b, c), cq + 3)),
            pl.BlockSpec((L, LANES), lambda b, c: (row(b, c), 0)),
            pl.BlockSpec((1, LANES), lambda b, c: (0, 0)),
            pl.BlockSpec((1, dim), lambda b, c: (0, 0)),
            pl.BlockSpec((1, H, D, D), lambda b, c: (b, 0, 0, 0)),
            pl.BlockSpec((1, H, D), lambda b, c: (b, 0, 0)),
            pl.BlockSpec((1, 1, LANES), lambda b, c: (b, 0, 0)),
        ],
        out_specs=(
            pl.BlockSpec((L, dim), lambda b, c: (b * nc + c, 0)),
            pl.BlockSpec((1, H, D, D), lambda b, c: (b, 0, 0, 0)),
            pl.BlockSpec((1, H, D), lambda b, c: (b, 0, 0)),
            pl.BlockSpec((1, 1, LANES), lambda b, c: (b, 0, 0)),
        ),
        compiler_params=_params("parallel", "arbitrary"),
        name="mlstm_mix",
    )(zb, zb, zb, zb, zs, bias, mnorm.reshape(1, dim), c0, n0, m0p)
    return y, c_out, n_out, m_out[:, 0, :H]


KEY_TILE = 128
KEY_CHUNK = 256
SEL_COLS = 64
POS_HI, POS_LO = SEL_COLS, SEL_COLS + 1
NEG_SEL = -1.0e9


def _slope(h):
    return 2.0 ** (-(8.0 / NSA_HEADS) * (h + 1))


def _key_features(pos, onehot):
    lane = lax.broadcasted_iota(jnp.int32, pos.shape, 1)
    hi = lax.shift_right_logical(pos, 6)
    lo = jnp.bitwise_and(pos, SEL_BLOCK - 1)
    f = jnp.where(lane == POS_HI, hi.astype(F32), jnp.where(lane == POS_LO, lo.astype(F32), 0.0))
    if onehot:
        f = jnp.where(lane == hi, 1.0, f)
    return f


def _query_features(shape, h):
    lane = lax.broadcasted_iota(jnp.int32, shape, 1)
    return jnp.where(lane == POS_HI, SEL_BLOCK * _slope(h), jnp.where(lane == POS_LO, _slope(h), 0.0))


def _compress_block_rows(load, pw_ref, kv, g):
    cols = slice(g * NSA_HD, (g + 1) * NSA_HD)
    a0 = a1 = None
    for j in range(CMP_STRIDE):
        rows = load(j)
        t0 = rows * pw_ref[kv, j:j + 1, cols]
        t1 = rows * pw_ref[kv, CMP_STRIDE + j:CMP_STRIDE + j + 1, cols]
        a0 = t0 if a0 is None else a0 + t0
        a1 = t1 if a1 is None else a1 + t1
    return a0, a1


def _finish_compress(a0, a1_scr, proj_ref, kcmp_aug, vcmp, kv, g, n_ch, ncp, v_transposed):
    acc = a0 + a1_scr[pl.ds(1, n_ch), :]
    c = jnp.dot(acc.astype(BF16), proj_ref[kv, g].astype(BF16), preferred_element_type=F32)
    if kv == 0:
        kcmp_aug[g, 0:n_ch, 0:NSA_HD] = c.astype(BF16)
        n = lax.broadcasted_iota(jnp.int32, (ncp, LANES), 0)
        kcmp_aug[g, :, NSA_HD:2 * NSA_HD] = _key_features(n * CMP_STRIDE + (CMP_LEN - 1), False).astype(BF16)
    elif v_transposed:
        if ncp > n_ch:
            c = jnp.concatenate([c, jnp.zeros((ncp - n_ch, NSA_HD), F32)], axis=0)
        for blk in range(ncp // LANES):
            vcmp[g, :, blk * LANES:(blk + 1) * LANES] = c[blk * LANES:(blk + 1) * LANES].T.astype(BF16)
    else:
        vcmp[g, 0:n_ch, :] = c.astype(BF16)


def _masked_softmax(s, mask):
    s = jnp.where(mask, s, NEG_MASK)
    m = jnp.max(s, axis=1, keepdims=True)
    e = jnp.where(mask, jnp.exp(s - m), 0.0)
    return e / jnp.maximum(jnp.sum(e, axis=1, keepdims=True), 1e-30)


def _select_blocks(psum, t0, n_cmp, n_sel, queries_on_lanes):
    ncp = psum.shape[0] if queries_on_lanes else psum.shape[1]
    nsp = -(-n_sel // 8) * 8
    j = lax.broadcasted_iota(jnp.int32, (nsp, ncp), 0)
    n = lax.broadcasted_iota(jnp.int32, (nsp, ncp), 1)
    cover = ((n * CMP_STRIDE < j * SEL_BLOCK + SEL_BLOCK) & (n * CMP_STRIDE + CMP_LEN > j * SEL_BLOCK)
             & (n < n_cmp)).astype(F32)
    if queries_on_lanes:
        imp = jnp.dot(cover, psum, precision=lax.Precision.HIGHEST, preferred_element_type=F32)
    else:
        imp = lax.dot_general(cover, psum, NT, precision=lax.Precision.HIGHEST, preferred_element_type=F32)
    jq = lax.broadcasted_iota(jnp.int32, (nsp, LANES), 0)
    t = t0 + lax.broadcasted_iota(jnp.int32, (nsp, LANES), 1)
    cur = lax.shift_right_logical(t, 6)
    forced = (jq == 0) | (jq == cur) | (jq == cur - 1)
    valid = (jq * SEL_BLOCK <= t) & (jq < n_sel)
    score = jnp.where(valid, imp + jnp.where(forced, FORCE_BONUS, 0.0), -jnp.inf)
    jf = jq.astype(F32)
    sel = jnp.zeros((nsp, LANES), F32)
    for _ in range(min(SEL_TOPN, n_sel)):
        mx = jnp.max(score, axis=0, keepdims=True)
        first = jnp.min(jnp.where(score == mx, jf, 1.0e9), axis=0, keepdims=True)
        pick = jf == first
        sel = jnp.where(pick, 1.0, sel)
        score = jnp.where(pick, -jnp.inf, score)
    bias = jnp.where((sel > 0.5) | (jq >= n_sel), 0.0, NEG_SEL)
    bias = jnp.concatenate([bias, jnp.zeros((LANES - nsp, LANES), F32)], axis=0)
    return bias.T


def _cmp_branch(q_plain, kcmp_aug_g, vcmp_g, t_rows, n_cmp):
    s = lax.dot_general(q_plain, kcmp_aug_g, NT, preferred_element_type=F32)
    n = lax.broadcasted_iota(jnp.int32, s.shape, 1)
    mask = (n * CMP_STRIDE + (CMP_LEN - 1) <= t_rows) & (n < n_cmp)
    p = _masked_softmax(s, mask)
    return jnp.dot(p.astype(BF16), vcmp_g, preferred_element_type=F32), p


def _online_step_t(state, s_t, v_t):
    m, l, acc = state
    m_new = jnp.maximum(m, jnp.max(s_t, axis=0, keepdims=True))
    alpha = jnp.exp(m - m_new)
    p = jnp.exp(s_t - m_new)
    l = alpha * l + jnp.sum(p, axis=0, keepdims=True)
    acc = alpha * acc + jnp.dot(v_t, p.astype(BF16), preferred_element_type=F32)
    return m_new, l, acc


def _stack_heads(q_heads, feats):
    return jnp.concatenate(
        [jnp.concatenate([q, f.astype(BF16)], axis=1) for q, f in zip(q_heads, feats)], axis=0)


def _write_gated(o_ref, gates, g, rows, o_cmp, o_s, o_w):
    for r in range(NSA_GROUP):
        h = g * NSA_GROUP + r
        sl = slice(r * rows, (r + 1) * rows)
        c = 8 + h
        o = gates[:, c:c + 1] * o_cmp[sl] + gates[:, c + 8:c + 9] * o_s[sl] + gates[:, c + 16:c + 17] * o_w[sl]
        o_ref[:, h * NSA_HD:(h + 1) * NSA_HD] = o


def _nsa_prompt_body(q_ref, kc_ref, ks_ref, kw_ref, gate_ref, pw_ref, proj_ref, o_ref,
                     ks_aug, vs_t, kw_aug, vw_t, kcmp_aug, vcmp_t, a1_scr, *, seq):
    i = pl.program_id(1)
    tq = KEY_TILE
    n_ch = seq // CMP_STRIDE
    n_cmp = n_ch - 1
    n_sel = seq // SEL_BLOCK
    ncp = kcmp_aug.shape[1]
    G, HD = NSA_GROUP, NSA_HD
    Q = G * tq

    @pl.when(i == 0)
    def _build():
        pos = lax.broadcasted_iota(jnp.int32, (seq, LANES), 0)
        f_sel = _key_features(pos, True).astype(BF16)
        f_win = _key_features(pos, False).astype(BF16)
        for g in range(NSA_KV_HEADS):
            ks_aug[g, :, 0:HD] = ks_ref[:, g * HD:(g + 1) * HD].astype(BF16)
            ks_aug[g, :, HD:2 * HD] = f_sel
            kw_aug[g, :, 0:HD] = kw_ref[:, g * HD:(g + 1) * HD].astype(BF16)
            kw_aug[g, :, HD:2 * HD] = f_win
            for kt in range(seq // KEY_TILE):
                rows = slice(kt * KEY_TILE, (kt + 1) * KEY_TILE)
                vs_t[g, :, rows] = ks_ref[rows, (2 + g) * HD:(3 + g) * HD].T.astype(BF16)
                vw_t[g, :, rows] = kw_ref[rows, (2 + g) * HD:(3 + g) * HD].T.astype(BF16)
        kcmp_aug[...] = jnp.zeros(kcmp_aug.shape, BF16)
        a1_scr[n_ch:n_ch + 8, :] = jnp.zeros((8, HD), F32)
        for kv in range(2):
            for g in range(NSA_KV_HEADS):
                c0 = (kv * NSA_KV_HEADS + g) * HD
                a0, a1 = _compress_block_rows(
                    lambda j: kc_ref[:, j * 4 * HD + c0:j * 4 * HD + c0 + HD], pw_ref, kv, g)
                a1_scr[0:n_ch, :] = a1
                _finish_compress(a0, a1_scr, proj_ref, kcmp_aug, vcmp_t, kv, g, n_ch, ncp, True)

    t0 = i * tq
    key = lax.broadcasted_iota(jnp.int32, (KEY_CHUNK, Q), 0)
    t_cols = t0 + jnp.bitwise_and(lax.broadcasted_iota(jnp.int32, (KEY_CHUNK, Q), 1), tq - 1)
    gates_t = jax.nn.sigmoid(gate_ref[...]).T
    scale = HD ** -0.5

    for g in range(NSA_KV_HEADS):
        q_heads = [(q_ref[:, (g * G + r) * HD:(g * G + r + 1) * HD] * scale).astype(BF16) for r in range(G)]
        feats = [_query_features((tq, LANES), g * G + r) for r in range(G)]
        q_plain = _stack_heads(q_heads, feats)

        s_t = lax.dot_general(kcmp_aug[g], q_plain, NT, preferred_element_type=F32)
        n = lax.broadcasted_iota(jnp.int32, (ncp, Q), 0)
        t_c = t0 + jnp.bitwise_and(lax.broadcasted_iota(jnp.int32, (ncp, Q), 1), tq - 1)
        mask = (n * CMP_STRIDE + (CMP_LEN - 1) <= t_c) & (n < n_cmp)
        s_t = jnp.where(mask, s_t, NEG_MASK)
        e = jnp.where(mask, jnp.exp(s_t - jnp.max(s_t, axis=0, keepdims=True)), 0.0)
        p_c = e / jnp.maximum(jnp.sum(e, axis=0, keepdims=True), 1e-30)
        o_cmp = jnp.dot(vcmp_t[g], p_c.astype(BF16), preferred_element_type=F32)
        psum = p_c[:, 0:tq] + p_c[:, tq:2 * tq] + p_c[:, 2 * tq:3 * tq] + p_c[:, 3 * tq:4 * tq]
        bias = _select_blocks(psum, t0, n_cmp, n_sel, True)
        q_sel = _stack_heads(q_heads, [f + bias for f in feats])

        init = (jnp.full((1, Q), NEG_MASK, F32), jnp.zeros((1, Q), F32), jnp.zeros((HD, Q), F32))

        def attend(k_aug, v_t, q, first, valid):
            def body(c, st):
                off = pl.multiple_of(c * KEY_CHUNK, KEY_CHUNK)
                s_t = lax.dot_general(k_aug[g, pl.ds(off, KEY_CHUNK), :], q, NT, preferred_element_type=F32)
                s_t = jnp.where(valid(off + key), s_t, NEG_MASK)
                return _online_step_t(st, s_t, v_t[g, :, pl.ds(off, KEY_CHUNK)])

            _, l, acc = lax.fori_loop(first, t0 // KEY_CHUNK + 1, body, init)
            return acc / l

        o_s = attend(ks_aug, vs_t, q_sel, 0, lambda kpos: kpos <= t_cols)
        o_w = attend(kw_aug, vw_t, q_plain, jnp.maximum(t0 - WINDOW, 0) // KEY_CHUNK,
                     lambda kpos: (kpos <= t_cols) & (t_cols - kpos <= WINDOW))

        for r in range(G):
            h = g * G + r
            cols = slice(r * tq, (r + 1) * tq)
            c = 8 + h
            o_t = (gates_t[c:c + 1, :] * o_cmp[:, cols] + gates_t[c + 8:c + 9, :] * o_s[:, cols]
                   + gates_t[c + 16:c + 17, :] * o_w[:, cols])
            o_ref[:, h * HD:(h + 1) * HD] = o_t.T


def nsa_prompt(zb, zs, kv_cmp, cmp_pos_w, cmp_proj, *, batch, seq, col_q, col_slc, col_win):
    assert seq % KEY_CHUNK == 0 and seq // SEL_BLOCK <= SEL_COLS
    nq = seq // KEY_TILE
    n_ch = seq // CMP_STRIDE
    ncp = -(-n_ch // LANES) * LANES
    qw, kvw = NSA_HEADS * NSA_HD, 4 * NSA_HD
    pw = cmp_pos_w.reshape(2, CMP_LEN, 2 * NSA_HD)
    return pl.pallas_call(
        functools.partial(_nsa_prompt_body, seq=seq),
        out_shape=jax.ShapeDtypeStruct((batch * seq, qw), F32),
        grid=(batch, nq),
        in_specs=[
            pl.BlockSpec((KEY_TILE, qw), lambda b, i: (b * nq + i, col_q // qw)),
            pl.BlockSpec((n_ch, CMP_STRIDE * kvw), lambda b, i: (b, 0)),
            pl.BlockSpec((seq, kvw), lambda b, i: (b, col_slc // kvw)),
            pl.BlockSpec((seq, kvw), lambda b, i: (b, col_win // kvw)),
            pl.BlockSpec((KEY_TILE, LANES), lambda b, i: (b * nq + i, 0)),
            pl.BlockSpec((2, CMP_LEN, 2 * NSA_HD), lambda b, i: (0, 0, 0)),
            pl.BlockSpec((2, NSA_KV_HEADS, NSA_HD, NSA_HD), lambda b, i: (0, 0, 0, 0)),
        ],
        out_specs=pl.BlockSpec((KEY_TILE, qw), lambda b, i: (b * nq + i, 0)),
        scratch_shapes=[
            pltpu.VMEM((NSA_KV_HEADS, seq, 2 * NSA_HD), BF16),
            pltpu.VMEM((NSA_KV_HEADS, NSA_HD, seq), BF16),
            pltpu.VMEM((NSA_KV_HEADS, seq, 2 * NSA_HD), BF16),
            pltpu.VMEM((NSA_KV_HEADS, NSA_HD, seq), BF16),
            pltpu.VMEM((NSA_KV_HEADS, ncp, 2 * NSA_HD), BF16),
            pltpu.VMEM((NSA_KV_HEADS, NSA_HD, ncp), BF16),
            pltpu.VMEM((n_ch + 8, NSA_HD), F32),
        ],
        compiler_params=_params("parallel", "arbitrary"),
        name="nsa_prompt",
    )(zb, kv_cmp.reshape(-1, CMP_STRIDE * kvw), zb, zb, zs, pw, cmp_proj)


def _nsa_sample_body(pt_ref, q_ref, ksn_ref, kwn_ref, gate_ref, wprev_ref, pw_ref, proj_ref, *rest,
                     ts, past, n_pages):
    del pt_ref
    cmp_pages = rest[:n_pages]
    slc_pages = rest[n_pages:2 * n_pages]
    o_ref = rest[2 * n_pages]
    ks_aug, vs, kw_aug, vw, kcmp_aug, vcmp, a0_scr, a1_scr = rest[2 * n_pages + 1:]
    G, HD = NSA_GROUP, NSA_HD
    page = slc_pages[0].shape[0] // 4
    kp = ks_aug.shape[1]
    wprev = wprev_ref.shape[0] // 4
    wp = kw_aug.shape[1]
    win_pos0 = past - wprev
    n_ch = (past + ts) // CMP_STRIDE
    n_cmp = n_ch - 1
    n_sel = -(-(past + ts) // SEL_BLOCK)
    ncp = kcmp_aug.shape[1]
    ch_per_page = page // CMP_STRIDE

    @pl.when(pl.program_id(0) == 0)
    def _constants():
        pos = lax.broadcasted_iota(jnp.int32, (kp, LANES), 0)
        f_sel = _key_features(pos, True).astype(BF16)
        posw = win_pos0 + lax.broadcasted_iota(jnp.int32, (wp, LANES), 0)
        f_win = _key_features(posw, False).astype(BF16)
        for g in range(NSA_KV_HEADS):
            ks_aug[g, :, HD:2 * HD] = f_sel
            kw_aug[g, :, HD:2 * HD] = f_win
        kcmp_aug[...] = jnp.zeros(kcmp_aug.shape, BF16)
        vcmp[...] = jnp.zeros(vcmp.shape, BF16)
        a1_scr[...] = jnp.zeros(a1_scr.shape, F32)

    def with_tail(new_rows):
        return jnp.concatenate([new_rows, jnp.zeros((KEY_TILE - ts, HD), F32)], axis=0).astype(BF16)

    for g in range(NSA_KV_HEADS):
        for p in range(n_pages):
            rows = slice(p * page, (p + 1) * page)
            ks_aug[g, rows, 0:HD] = slc_pages[p][pl.ds(g, page, stride=4), :].astype(BF16)
            vs[g, rows, :] = slc_pages[p][pl.ds(2 + g, page, stride=4), :].astype(BF16)
        ks_aug[g, past:past + KEY_TILE, 0:HD] = with_tail(ksn_ref[:, g * HD:(g + 1) * HD])
        vs[g, past:past + KEY_TILE, :] = with_tail(ksn_ref[:, (2 + g) * HD:(3 + g) * HD])
        kw_aug[g, 0:wprev, 0:HD] = wprev_ref[pl.ds(g, wprev, stride=4), :].astype(BF16)
        vw[g, 0:wprev, :] = wprev_ref[pl.ds(2 + g, wprev, stride=4), :].astype(BF16)
        kw_aug[g, wprev:wprev + KEY_TILE, 0:HD] = with_tail(kwn_ref[:, g * HD:(g + 1) * HD])
        vw[g, wprev:wprev + KEY_TILE, :] = with_tail(kwn_ref[:, (2 + g) * HD:(3 + g) * HD])

    for kv in range(2):
        for g in range(NSA_KV_HEADS):
            for p in range(n_pages):
                c = kv * NSA_KV_HEADS + g
                a0, a1 = _compress_block_rows(
                    lambda j: cmp_pages[p][pl.ds(j * 4 + c, ch_per_page, stride=4 * CMP_STRIDE), :], pw_ref, kv, g)
                a0_scr[p * ch_per_page:(p + 1) * ch_per_page, :] = a0
                a1_scr[p * ch_per_page:(p + 1) * ch_per_page, :] = a1
            _finish_compress(a0_scr[0:n_ch, :], a1_scr, proj_ref, kcmp_aug, vcmp, kv, g, n_ch, ncp, False)

    R = G * ts
    gates = jax.nn.sigmoid(gate_ref[...])
    scale = HD ** -0.5

    def t_of(shape):
        return past + jnp.bitwise_and(lax.broadcasted_iota(jnp.int32, shape, 0), ts - 1)

    def softmax_pv(s, v):
        m = jnp.max(s, axis=1, keepdims=True)
        e = jnp.exp(s - m)
        return jnp.dot(e.astype(BF16), v, preferred_element_type=F32) / jnp.sum(e, axis=1, keepdims=True)

    for g in range(NSA_KV_HEADS):
        q_heads = [(q_ref[:, (g * G + r) * HD:(g * G + r + 1) * HD] * scale).astype(BF16) for r in range(G)]
        feats = [_query_features((ts, LANES), g * G + r) for r in range(G)]
        q_plain = _stack_heads(q_heads, feats)
        o_cmp, p_c = _cmp_branch(q_plain, kcmp_aug[g], vcmp[g], t_of((R, ncp)), n_cmp)
        psum = p_c[0:ts] + p_c[ts:2 * ts] + p_c[2 * ts:3 * ts] + p_c[3 * ts:4 * ts]
        psum = jnp.concatenate([psum, jnp.zeros((LANES - ts, ncp), F32)], axis=0)
        bias = _select_blocks(psum, past, n_cmp, n_sel, False)[0:ts]
        q_sel = _stack_heads(q_heads, [f + bias for f in feats])

        s = lax.dot_general(q_sel, ks_aug[g], NT, preferred_element_type=F32)
        s = jnp.where(lax.broadcasted_iota(jnp.int32, (R, kp), 1) <= t_of((R, kp)), s, NEG_MASK)
        o_s = softmax_pv(s, vs[g])

        s = lax.dot_general(q_plain, kw_aug[g], NT, preferred_element_type=F32)
        idx = lax.broadcasted_iota(jnp.int32, (R, wp), 1)
        dist = t_of((R, wp)) - (win_pos0 + idx)
        s = jnp.where((idx < wprev + ts) & (dist >= 0) & (dist <= WINDOW), s, NEG_MASK)
        o_w = softmax_pv(s, vw[g])

        _write_gated(o_ref, gates, g, ts, o_cmp, o_s, o_w)


def nsa_sample(zb, zs, cache_cmp, cache_slc, state_win, page_table, cmp_pos_w, cmp_proj, *,
               layer, row0, batch, ts, col_q, col_slc, col_win):
    depth, n_pool, page = cache_cmp.shape[:3]
    n_pages = page_table.shape[1]
    past = n_pages * page
    wprev = state_win.shape[2]
    assert ts & (ts - 1) == 0 and ts <= KEY_TILE and row0 % ts == 0
    assert past % KEY_TILE == 0 and (past + ts) // CMP_STRIDE == past // CMP_STRIDE
    assert page % CMP_STRIDE == 0 and -(-(past + ts) // SEL_BLOCK) <= SEL_COLS and wprev % 16 == 0
    qw, kvw = NSA_HEADS * NSA_HD, 4 * NSA_HD
    n_ch = past // CMP_STRIDE
    ncp = -(-n_ch // LANES) * LANES
    r0 = row0 // ts
    pw = cmp_pos_w.reshape(2, CMP_LEN, 2 * NSA_HD)
    cmp_view = cache_cmp.reshape(depth, n_pool, page * 4, NSA_HD)
    slc_view = cache_slc.reshape(depth, n_pool, page * 4, NSA_HD)
    win_view = state_win.reshape(depth, batch, wprev * 4, NSA_HD)

    def page_map(p):
        return lambda b, pt: (layer, pt[b * n_pages + p], 0, 0)

    in_specs = [
        pl.BlockSpec((ts, qw), lambda b, pt: (r0 + b, col_q // qw)),
        pl.BlockSpec((ts, kvw), lambda b, pt: (r0 + b, col_slc // kvw)),
        pl.BlockSpec((ts, kvw), lambda b, pt: (r0 + b, col_win // kvw)),
        pl.BlockSpec((ts, LANES), lambda b, pt: (r0 + b, 0)),
        pl.BlockSpec((None, None, wprev * 4, NSA_HD), lambda b, pt: (layer, b, 0, 0)),
        pl.BlockSpec((2, CMP_LEN, 2 * NSA_HD), lambda b, pt: (0, 0, 0)),
        pl.BlockSpec((2, NSA_KV_HEADS, NSA_HD, NSA_HD), lambda b, pt: (0, 0, 0, 0)),
    ]
    in_specs += [pl.BlockSpec((None, None, page * 4, NSA_HD), page_map(p % n_pages)) for p in range(2 * n_pages)]
    return pl.pallas_call(
        functools.partial(_nsa_sample_body, ts=ts, past=past, n_pages=n_pages),
        out_shape=jax.ShapeDtypeStruct((batch * ts, qw), F32),
        grid_spec=pltpu.PrefetchScalarGridSpec(
            num_scalar_prefetch=1,
            grid=(batch,),
            in_specs=in_specs,
            out_specs=pl.BlockSpec((ts, qw), lambda b, pt: (b, 0)),
            scratch_shapes=[
                pltpu.VMEM((NSA_KV_HEADS, past + KEY_TILE, 2 * NSA_HD), BF16),
                pltpu.VMEM((NSA_KV_HEADS, past + KEY_TILE, NSA_HD), BF16),
                pltpu.VMEM((NSA_KV_HEADS, wprev + KEY_TILE, 2 * NSA_HD), BF16),
                pltpu.VMEM((NSA_KV_HEADS, wprev + KEY_TILE, NSA_HD), BF16),
                pltpu.VMEM((NSA_KV_HEADS, ncp, 2 * NSA_HD), BF16),
                pltpu.VMEM((NSA_KV_HEADS, ncp, NSA_HD), BF16),
                pltpu.VMEM((n_ch, NSA_HD), F32),
                pltpu.VMEM((n_ch + 8, NSA_HD), F32),
            ],
        ),
        compiler_params=_params("arbitrary"),
        name="nsa_sample",
    )(page_table.reshape(-1), zb, zb, zb, zs, win_view, pw, cmp_proj,
      *([cmp_view] * n_pages), *([slc_view] * n_pages))


ZB_NQ, ZB_POOL, ZB_MQ, ZB_MK, ZB_MV, ZB_MO, ZB_CMP, ZB_SLC, ZB_WIN, ZB_END = (
    0, 1024, 1536, 2048, 2560, 3072, 3584, 4096, 4608, 5120)
W_POOL, W_MI, W_NQ, W_CMP, W_NG, W_END = 0, 2560, 2568, 3592, 5128, 5152


def _split_w_in(w_in_l):
    big = jnp.concatenate([w_in_l[:, W_NQ:W_CMP], w_in_l[:, W_POOL:W_MI], w_in_l[:, W_CMP:W_NG]], axis=1)
    small = jnp.concatenate([w_in_l[:, W_MI:W_NQ], w_in_l[:, W_NG:W_END]], axis=1)
    small = jnp.pad(small, ((0, 0), (0, LANES - small.shape[1])))
    return big.astype(BF16), small.astype(BF16)


def kernel(x_prompt, x_sample, cache_kv_cmp, cache_kv_slc, state_kv_win, state_pool, state_mlstm_C, state_mlstm_n, state_mlstm_m, page_table, ffn1_norm, ffn1_w_gate, ffn1_w_up, ffn1_w_down, mix_norm, w_in, w_out, pool_w, pool_scale, mlstm_if_bias, mlstm_norm, nsa_cmp_pos_w, nsa_cmp_proj, ffn2_norm, ffn2_w_gate, ffn2_w_up, ffn2_w_down, final_norm):
    bp, tp, d = x_prompt.shape
    bs, ts, _ = x_sample.shape
    depth = w_in.shape[0]
    mp, ms = bp * tp, bs * ts
    past_len = page_table.shape[1] * cache_kv_cmp.shape[2]
    pd = pool_scale.shape[1]
    kv_row = (2, NSA_KV_HEADS, NSA_HD)
    x = jnp.concatenate([x_prompt.reshape(mp, d), x_sample.reshape(ms, d)], axis=0)
    zeros = lambda *s: jnp.zeros(s, F32)
    outs = [[] for _ in range(14)]
    for l in range(depth):
        x = ffn_half_step(x, ffn1_norm[l], ffn1_w_gate[l].astype(BF16), ffn1_w_up[l].astype(BF16),
                          ffn1_w_down[l].astype(BF16))
        w_big, w_small = _split_w_in(w_in[l])
        zb = norm_project(x, mix_norm[l], w_big)
        zs = norm_project(x, mix_norm[l], w_small)
        kv_cmp = zb[:, ZB_CMP:ZB_SLC]
        kv_slc = zb[:, ZB_SLC:ZB_WIN]
        kv_win = zb[:, ZB_WIN:ZB_END]
        z_pool = zb[:, ZB_POOL:ZB_MQ]

        yp_pool = pool_prompt(zb, pool_w[l], pool_scale[l], batch=bp, seq=tp, col=ZB_POOL)
        yp_m, p_c, p_n, p_m = mlstm_mix(
            zb, zs, mlstm_if_bias[l], mlstm_norm[l], zeros(bp, MLSTM_HEADS, MLSTM_HD, MLSTM_HD),
            zeros(bp, MLSTM_HEADS, MLSTM_HD), zeros(bp, MLSTM_HEADS), row0=0, batch=bp, seq=tp, col_q=ZB_MQ)
        yp_nsa = nsa_prompt(zb, zs, kv_cmp, nsa_cmp_pos_w[l], nsa_cmp_proj[l], batch=bp, seq=tp,
                            col_q=ZB_NQ, col_slc=ZB_SLC, col_win=ZB_WIN)

        pool_full = jnp.concatenate([zeros(bs, POOL_HALO - POOL_BUF, pd), state_pool[l],
                                     z_pool[mp:].reshape(bs, ts, pd)], axis=1)
        ys_pool = pool_sample(pool_full, pool_w[l], pool_scale[l], pos0=past_len)
        ys_m, s_c, s_n, s_m = mlstm_mix(
            zb, zs, mlstm_if_bias[l], mlstm_norm[l], state_mlstm_C[l], state_mlstm_n[l], state_mlstm_m[l],
            row0=mp, batch=bs, seq=ts, col_q=ZB_MQ)
        ys_nsa = nsa_sample(zb, zs, cache_kv_cmp, cache_kv_slc, state_kv_win, page_table,
                            nsa_cmp_pos_w[l], nsa_cmp_proj[l], layer=l, row0=mp, batch=bs, ts=ts,
                            col_q=ZB_NQ, col_slc=ZB_SLC, col_win=ZB_WIN)

        x = out_project(x, jnp.concatenate([yp_pool, ys_pool]), jnp.concatenate([yp_m, ys_m]),
                        jnp.concatenate([yp_nsa, ys_nsa]), w_out[l].astype(BF16))
        x = ffn_half_step(x, ffn2_norm[l], ffn2_w_gate[l].astype(BF16), ffn2_w_up[l].astype(BF16),
                          ffn2_w_down[l].astype(BF16), gf=final_norm if l == depth - 1 else None)

        wp_rows = min(WINDOW, tp)
        win_s = jnp.concatenate([state_kv_win[l], kv_win[mp:].reshape(bs, ts, *kv_row)], axis=1)
        pool_p = jnp.concatenate([zeros(bp, POOL_BUF, pd), z_pool[:mp].reshape(bp, tp, pd)], axis=1)
        layer_out = (
            kv_cmp[:mp].reshape(bp, tp, *kv_row), kv_slc[:mp].reshape(bp, tp, *kv_row),
            kv_win[:mp].reshape(bp, tp, *kv_row)[:, tp - wp_rows:], pool_p[:, -POOL_BUF:], p_c, p_n, p_m,
            kv_cmp[mp:].reshape(bs, ts, *kv_row), kv_slc[mp:].reshape(bs, ts, *kv_row),
            win_s[:, -min(WINDOW, win_s.shape[1]):], pool_full[:, -POOL_BUF:], s_c, s_n, s_m)
        for acc, a in zip(outs, layer_out):
            acc.append(a)
    y_prompt = x[:mp].reshape(bp, tp, d)
    y_sample = x[mp:].reshape(bs, ts, d)
    return (y_prompt, y_sample, *[jnp.stack(a) for a in outs])
```

```python
import functools
import math

import jax
import jax.numpy as jnp
from jax import lax
from jax.experimental import pallas as pl
from jax.experimental.pallas import tpu as pltpu

F32 = jnp.float32
BF16 = jnp.bfloat16
EPS = 1e-6

VMEM_LIMIT_BYTES = 56 * 1024 * 1024
LANES = 128

POOL_WINDOWS = (2, 4, 8, 16)
POOL_BUF = 15
MLSTM_HEADS = 4
MLSTM_HD = 128
MLSTM_CHUNK = 64
NSA_HD = 128
NSA_HEADS = 8
NSA_KV_HEADS = 2
NSA_GROUP = 4
CMP_LEN = 32
CMP_STRIDE = 16
SEL_BLOCK = 64
SEL_TOPN = 16
WINDOW = 512
FORCE_BONUS = 1.0e4

NEG_MASK = -1.0e30
NT = (((1,), (1,)), ((), ()))
TN = (((0,), (0,)), ((), ()))


def _pick_tile(n, pref):
    t = pref
    while t > 8 and n % t:
        t //= 2
    assert n % t == 0, (n, pref)
    return t


def _params(*sem):
    return pltpu.CompilerParams(dimension_semantics=sem, vmem_limit_bytes=VMEM_LIMIT_BYTES)


def _drop_alias_ref(body, index, *refs):
    return body(*refs[:index], *refs[index + 1:])


def _rms_rows(x, g):
    ms = jnp.mean(x * x, axis=-1, keepdims=True)
    return x * lax.rsqrt(ms + EPS) * g


def _ffn_body(x_ref, g_ref, wg_ref, wu_ref, wd_ref, gf_ref, o_ref, n_scr, *, final_norm):
    f = pl.program_id(1)

    @pl.when(f == 0)
    def _():
        x = x_ref[...]
        n_scr[...] = _rms_rows(x, g_ref[...]).astype(BF16)
        o_ref[...] = x

    n = n_scr[...]
    hg = jnp.dot(n, wg_ref[...], preferred_element_type=F32)
    hu = jnp.dot(n, wu_ref[...], preferred_element_type=F32)
    h = (hg * jax.nn.sigmoid(hg) * hu).astype(BF16)
    o_ref[...] += 0.5 * jnp.dot(h, wd_ref[...], preferred_element_type=F32)

    if final_norm:
        @pl.when(f == pl.num_programs(1) - 1)
        def _():
            o_ref[...] = _rms_rows(o_ref[...], gf_ref[...])


def ffn_half_step(x, g, wg, wu, wd, gf=None):
    m, d = x.shape
    fdim = wg.shape[1]
    tm = _pick_tile(m, 512)
    tf = _pick_tile(fdim, 512)
    final_norm = gf is not None
    if gf is None:
        gf = g
    return pl.pallas_call(
        functools.partial(_ffn_body, final_norm=final_norm),
        out_shape=jax.ShapeDtypeStruct((m, d), F32),
        grid=(m // tm, fdim // tf),
        in_specs=[
            pl.BlockSpec((tm, d), lambda i, f: (i, 0)),
            pl.BlockSpec((1, d), lambda i, f: (0, 0)),
            pl.BlockSpec((d, tf), lambda i, f: (0, f)),
            pl.BlockSpec((d, tf), lambda i, f: (0, f)),
            pl.BlockSpec((tf, d), lambda i, f: (f, 0)),
            pl.BlockSpec((1, d), lambda i, f: (0, 0)),
        ],
        out_specs=pl.BlockSpec((tm, d), lambda i, f: (i, 0)),
        scratch_shapes=[pltpu.VMEM((tm, d), BF16)],
        compiler_params=_params("parallel", "arbitrary"),
        name="ffn_half_step",
    )(x, g.reshape(1, d), wg, wu, wd, gf.reshape(1, d))


def _inproj_body(x_ref, g_ref, w_ref, o_ref, n_scr):
    @pl.when(pl.program_id(1) == 0)
    def _():
        n_scr[...] = _rms_rows(x_ref[...], g_ref[...]).astype(BF16)

    o_ref[...] = jnp.dot(n_scr[...], w_ref[...], preferred_element_type=F32)


def norm_project(x, g, w):
    m, d = x.shape
    n = w.shape[1]
    tm = _pick_tile(m, 1024)
    tn = _pick_tile(n, 1024)
    return pl.pallas_call(
        _inproj_body,
        out_shape=jax.ShapeDtypeStruct((m, n), F32),
        grid=(m // tm, n // tn),
        in_specs=[
            pl.BlockSpec((tm, d), lambda i, j: (i, 0)),
            pl.BlockSpec((1, d), lambda i, j: (0, 0)),
            pl.BlockSpec((d, tn), lambda i, j: (0, j)),
        ],
        out_specs=pl.BlockSpec((tm, tn), lambda i, j: (i, j)),
        scratch_shapes=[pltpu.VMEM((tm, d), BF16)],
        compiler_params=_params("parallel", "arbitrary"),
        name="norm_project",
    )(x, g.reshape(1, d), w)


def _outproj_body(x_ref, ya_ref, yb_ref, yc_ref, wa_ref, wb_ref, wc_ref, o_ref):
    acc = x_ref[...]
    acc += jnp.dot(ya_ref[...].astype(BF16), wa_ref[...], preferred_element_type=F32)
    acc += jnp.dot(yb_ref[...].astype(BF16), wb_ref[...], preferred_element_type=F32)
    acc += jnp.dot(yc_ref[...].astype(BF16), wc_ref[...], preferred_element_type=F32)
    o_ref[...] = acc


def out_project(x, y_pool, y_mlstm, y_nsa, w_out):
    m, d = x.shape
    da, db, dc = y_pool.shape[1], y_mlstm.shape[1], y_nsa.shape[1]
    assert da == db and dc % da == 0
    tm = _pick_tile(m, 512)
    tn = d
    return pl.pallas_call(
        _outproj_body,
        out_shape=jax.ShapeDtypeStruct((m, d), F32),
        grid=(m // tm, d // tn),
        in_specs=[
            pl.BlockSpec((tm, tn), lambda i, j: (i, j)),
            pl.BlockSpec((tm, da), lambda i, j: (i, 0)),
            pl.BlockSpec((tm, db), lambda i, j: (i, 0)),
            pl.BlockSpec((tm, dc), lambda i, j: (i, 0)),
            pl.BlockSpec((da, tn), lambda i, j: (0, j)),
            pl.BlockSpec((db, tn), lambda i, j: (1, j)),
            pl.BlockSpec((dc, tn), lambda i, j: ((da + db) // dc, j)),
        ],
        out_specs=pl.BlockSpec((tm, tn), lambda i, j: (i, j)),
        compiler_params=_params("parallel", "arbitrary"),
        name="out_project",
    )(x, y_pool, y_mlstm, y_nsa, w_out, w_out, w_out)


POOL_HALO = 16


def _pool_group(load, g, n_avail, w_ref, sc_ref):
    w = POOL_WINDOWS[g]
    z = load(0)
    acc = z
    for j in range(1, w):
        acc = acc + load(j)
    d = acc / jnp.minimum(n_avail, w).astype(F32) - z
    lead = d.shape[:-1]
    gd = d.shape[-1]
    y = jnp.dot(d.reshape(-1, gd).astype(BF16), w_ref[g].astype(BF16), preferred_element_type=F32)
    return (y * sc_ref[:, g * gd:(g + 1) * gd]).reshape(*lead, gd)


def _pool_prompt_body(z_ref, w_ref, sc_ref, o_ref, full_scr, *, chunk):
    seq, pd = z_ref.shape
    gd = pd // len(POOL_WINDOWS)
    full_scr[0:POOL_HALO, :] = jnp.zeros((POOL_HALO, pd), F32)
    full_scr[POOL_HALO:POOL_HALO + seq, :] = z_ref[...]
    for c in range(seq // chunk):
        n_avail = c * chunk + 1 + lax.broadcasted_iota(jnp.int32, (chunk, gd), 0)
        for g in range(len(POOL_WINDOWS)):
            load = lambda j: full_scr[pl.ds(POOL_HALO + c * chunk - j, chunk), g * gd:(g + 1) * gd]
            o_ref[c * chunk:(c + 1) * chunk, g * gd:(g + 1) * gd] = _pool_group(load, g, n_avail, w_ref, sc_ref)


def pool_prompt(zb, pool_w, pool_scale, *, batch, seq, col, out_rows=None):
    pd = pool_scale.shape[0]
    chunk = _pick_tile(seq, 256)
    return pl.pallas_call(
        functools.partial(_pool_prompt_body, chunk=chunk),
        out_shape=jax.ShapeDtypeStruct((out_rows or batch * seq, pd), F32),
        grid=(batch,),
        in_specs=[
            pl.BlockSpec((seq, pd), lambda b: (b, col // pd)),
            pl.BlockSpec(pool_w.shape, lambda b: (0, 0, 0)),
            pl.BlockSpec((1, pd), lambda b: (0, 0)),
        ],
        out_specs=pl.BlockSpec((seq, pd), lambda b: (b, 0)),
        scratch_shapes=[pltpu.VMEM((POOL_HALO + seq, pd), F32)],
        compiler_params=_params("parallel"),
        name="pool_prompt",
    )(zb, pool_w, pool_scale.reshape(1, pd))


def _pool_sample_body(full_ref, w_ref, sc_ref, o_ref, *, pos0):
    bt, rows, pd = full_ref.shape
    ts = rows - POOL_HALO
    gd = pd // len(POOL_WINDOWS)
    n_avail = pos0 + 1 + lax.broadcasted_iota(jnp.int32, (bt, ts, gd), 1)
    for g in range(len(POOL_WINDOWS)):
        load = lambda j: full_ref[:, pl.ds(POOL_HALO - j, ts), g * gd:(g + 1) * gd]
        o_ref[:, g * gd:(g + 1) * gd] = _pool_group(load, g, n_avail, w_ref, sc_ref).reshape(bt * ts, gd)


def pool_sample(full, pool_w, pool_scale, y_init, *, pos0, row0):
    batch, rows, pd = full.shape
    ts = rows - POOL_HALO
    bt = math.gcd(batch, 32)
    assert ts % 8 == 0 and row0 % (bt * ts) == 0
    r0 = row0 // (bt * ts)
    return pl.pallas_call(
        functools.partial(_drop_alias_ref, functools.partial(_pool_sample_body, pos0=pos0), 3),
        out_shape=jax.ShapeDtypeStruct(y_init.shape, F32),
        grid=(batch // bt,),
        in_specs=[
            pl.BlockSpec((bt, rows, pd), lambda b: (b, 0, 0)),
            pl.BlockSpec(pool_w.shape, lambda b: (0, 0, 0)),
            pl.BlockSpec((1, pd), lambda b: (0, 0)),
            pl.BlockSpec(memory_space=pl.ANY),
        ],
        out_specs=pl.BlockSpec((bt * ts, pd), lambda b: (r0 + b, 0)),
        input_output_aliases={3: 0},
        compiler_params=_params("parallel"),
        name="pool_sample",
    )(full, pool_w, pool_scale.reshape(1, pd), y_init)


def _log_sigmoid(x):
    return jnp.minimum(x, 0.0) - jnp.log1p(jnp.exp(-jnp.abs(x)))


def _mlstm_body(q_ref, k_ref, v_ref, og_ref, g_ref, bias_ref, gn_ref, c0_ref, n0_ref, m0_ref,
                y_ref, c_ref, n_ref, m_ref, *, L):
    nseq = q_ref.shape[0] // L
    H, D = MLSTM_HEADS, MLSTM_HD
    hi = lax.Precision.HIGHEST

    @pl.when(pl.program_id(1) == 0)
    def _():
        c_ref[...] = c0_ref[...]
        n_ref[...] = n0_ref[...]
        m_ref[...] = m0_ref[...]

    sel = (lax.broadcasted_iota(jnp.int32, (8, LANES), 0) == lax.broadcasted_iota(jnp.int32, (8, LANES), 1)).astype(F32)
    li = lax.broadcasted_iota(jnp.int32, (L, L), 0)
    si = lax.broadcasted_iota(jnp.int32, (L, L), 1)
    causal = li >= si
    lane = lax.broadcasted_iota(jnp.int32, (1, LANES), 1)
    chains = [(s, h) for s in range(nseq) for h in range(H)]

    gates = []
    for s in range(nseq):
        rows = slice(s * L, (s + 1) * L)
        gz = g_ref[rows, :] + bias_ref[...]
        gz_rows = lax.dot_general(sel, gz, NT, precision=hi, preferred_element_type=F32)
        b_cols = jnp.dot(causal.astype(F32), _log_sigmoid(gz), precision=hi, preferred_element_type=F32)
        b_rows = jnp.dot(_log_sigmoid(gz_rows), (li <= si).astype(F32), precision=hi, preferred_element_type=F32)
        gates.append((gz, gz_rows, b_cols, b_rows, m_ref[s]))

    qk, qc, state = {}, {}, {}
    for s, h in chains:
        rows, cols = slice(s * L, (s + 1) * L), slice(h * D, (h + 1) * D)
        qh = q_ref[rows, cols]
        kh = k_ref[rows, cols] * (D ** -0.5)
        ch = c_ref[s, h]
        nh = n_ref[s, h:h + 1, :]
        qb, kb = qh.astype(BF16), kh.astype(BF16)
        qk[s, h] = lax.dot_general(qb, kb, NT, preferred_element_type=F32)
        qc[s, h] = lax.dot_general(qb, ch.astype(BF16), NT, preferred_element_type=F32)
        state[s, h] = (qh, kh, kb, ch, nh)

    sm, stats = {}, {}
    for s, h in chains:
        gz, gz_rows, b_cols, b_rows, m_all = gates[s]
        bc = b_cols[:, H + h:H + h + 1]
        ic = gz[:, h:h + 1]
        br = b_rows[H + h:H + h + 1, :]
        ir = gz_rows[h:h + 1, :]
        m_prev = m_all[:, h:h + 1]
        dmat = jnp.where(causal, bc - br + ir, NEG_MASK)
        inter = bc + m_prev
        m_t = jnp.maximum(inter, jnp.max(dmat, axis=1, keepdims=True))
        sm[s, h] = qk[s, h] * jnp.exp(dmat - m_t)
        m_new = m_t[L - 1:L, :]
        b_last = bc[L - 1:L, :]
        stats[s, h] = (jnp.exp(inter - m_t), m_t, m_new, jnp.exp(b_last + m_prev - m_new),
                       jnp.exp(b_last - bc + ic - m_new))

    num, c_new = {}, {}
    for s, h in chains:
        rows, cols = slice(s * L, (s + 1) * L), slice(h * D, (h + 1) * D)
        a_inter, _, _, decay, w_col = stats[s, h]
        qh, kh, kb, ch, nh = state[s, h]
        vh = v_ref[rows, cols]
        num[s, h] = jnp.dot(sm[s, h].astype(BF16), vh.astype(BF16), preferred_element_type=F32) + a_inter * qc[s, h]
        c_new[s, h] = decay * ch + lax.dot_general((vh * w_col).astype(BF16), kb, TN, preferred_element_type=F32)

    m_out = [gates[s][4] for s in range(nseq)]
    for s, h in chains:
        rows, cols = slice(s * L, (s + 1) * L), slice(h * D, (h + 1) * D)
        a_inter, m_t, m_new, decay, w_col = stats[s, h]
        qh, kh, kb, ch, nh = state[s, h]
        den = jnp.sum(sm[s, h], axis=1, keepdims=True) + a_inter * jnp.sum(qh * nh, axis=1, keepdims=True)
        den = jnp.maximum(jnp.abs(den), jnp.exp(-m_t))
        hh = num[s, h] / den
        mu = jnp.mean(hh, axis=1, keepdims=True)
        var = jnp.mean(jnp.square(hh - mu), axis=1, keepdims=True)
        hn = (hh - mu) * lax.rsqrt(var + EPS) * gn_ref[:, cols]
        y_ref[rows, cols] = jax.nn.sigmoid(og_ref[rows, cols]) * hn
        c_ref[s, h] = c_new[s, h]
        n_ref[s, h:h + 1, :] = decay * nh + jnp.sum(kh * w_col, axis=0, keepdims=True)
        m_out[s] = jnp.where(lane == h, m_new, m_out[s])
    for s in range(nseq):
        m_ref[s] = m_out[s]


def mlstm_mix(zb, zs, if_bias, mnorm, c0, n0, m0, *, row0, batch, seq, col_q, y_init=None, out_rows=None):
    H, D = MLSTM_HEADS, MLSTM_HD
    dim = H * D
    L = math.gcd(seq, MLSTM_CHUNK)
    nc = seq // L
    nseq = math.gcd(batch, min(4, MLSTM_CHUNK // L)) if nc == 1 else 1
    rows = nseq * L
    assert L % 8 == 0 and row0 % rows == 0 and col_q % dim == 0
    r0 = row0 // rows
    cq = col_q // dim
    bias = jnp.pad(if_bias, (0, LANES - if_bias.shape[0])).reshape(1, LANES)
    m0p = jnp.pad(m0, ((0, 0), (0, LANES - H))).reshape(batch, 1, LANES)
    row = lambda b, c: r0 + b * nc + c
    y_rows = (out_rows or batch * seq) if y_init is None else y_init.shape[0]
    y_r0 = 0 if y_init is None else r0
    in_specs = [
        pl.BlockSpec((rows, dim), lambda b, c: (row(b, c), cq)),
        pl.BlockSpec((rows, dim), lambda b, c: (row(b, c), cq + 1)),
        pl.BlockSpec((rows, dim), lambda b, c: (row(b, c), cq + 2)),
        pl.BlockSpec((rows, dim), lambda b, c: (row(b, c), cq + 3)),
        pl.BlockSpec((rows, LANES), lambda b, c: (row(b, c), 0)),
        pl.BlockSpec((1, LANES), lambda b, c: (0, 0)),
        pl.BlockSpec((1, dim), lambda b, c: (0, 0)),
        pl.BlockSpec((nseq, H, D, D), lambda b, c: (b, 0, 0, 0)),
        pl.BlockSpec((nseq, H, D), lambda b, c: (b, 0, 0)),
        pl.BlockSpec((nseq, 1, LANES), lambda b, c: (b, 0, 0)),
    ]
    args = [zb, zb, zb, zb, zs, bias, mnorm.reshape(1, dim), c0, n0, m0p]
    aliases = {}
    body = functools.partial(_mlstm_body, L=L)
    if y_init is not None:
        in_specs.append(pl.BlockSpec(memory_space=pl.ANY))
        args.append(y_init)
        aliases = {len(args) - 1: 0}
        body = functools.partial(_drop_alias_ref, body, len(args) - 1)
    y, c_out, n_out, m_out = pl.pallas_call(
        body,
        out_shape=(jax.ShapeDtypeStruct((y_rows, dim), F32),
                   jax.ShapeDtypeStruct((batch, H, D, D), F32),
                   jax.ShapeDtypeStruct((batch, H, D), F32),
                   jax.ShapeDtypeStruct((batch, 1, LANES), F32)),
        grid=(batch // nseq, nc),
        in_specs=in_specs,
        out_specs=(
            pl.BlockSpec((rows, dim), lambda b, c: (y_r0 + b * nc + c, 0)),
            pl.BlockSpec((nseq, H, D, D), lambda b, c: (b, 0, 0, 0)),
            pl.BlockSpec((nseq, H, D), lambda b, c: (b, 0, 0)),
            pl.BlockSpec((nseq, 1, LANES), lambda b, c: (b, 0, 0)),
        ),
        input_output_aliases=aliases,
        compiler_params=_params("parallel", "arbitrary"),
        name="mlstm_mix",
    )(*args)
    return y, c_out, n_out, m_out[:, 0, :H]


KEY_TILE = 128
KEY_CHUNK = 256
SEL_COLS = 64
POS_HI, POS_LO = SEL_COLS, SEL_COLS + 1
NEG_SEL = -1.0e9


def _slope(h):
    return 2.0 ** (-(8.0 / NSA_HEADS) * (h + 1))


def _key_features(pos, onehot):
    lane = lax.broadcasted_iota(jnp.int32, pos.shape, 1)
    hi = lax.shift_right_logical(pos, 6)
    lo = jnp.bitwise_and(pos, SEL_BLOCK - 1)
    f = jnp.where(lane == POS_HI, hi.astype(F32), jnp.where(lane == POS_LO, lo.astype(F32), 0.0))
    if onehot:
        f = jnp.where(lane == hi, 1.0, f)
    return f


def _query_features(shape, h):
    lane = lax.broadcasted_iota(jnp.int32, shape, 1)
    return jnp.where(lane == POS_HI, SEL_BLOCK * _slope(h), jnp.where(lane == POS_LO, _slope(h), 0.0))


def _compress_block_rows(load, pw_ref, kv, g):
    cols = slice(g * NSA_HD, (g + 1) * NSA_HD)
    a0 = a1 = None
    for j in range(CMP_STRIDE):
        rows = load(j)
        t0 = rows * pw_ref[kv, j:j + 1, cols]
        t1 = rows * pw_ref[kv, CMP_STRIDE + j:CMP_STRIDE + j + 1, cols]
        a0 = t0 if a0 is None else a0 + t0
        a1 = t1 if a1 is None else a1 + t1
    return a0, a1


def _finish_compress(acc, proj_ref, kcmp_aug, vcmp, kv, g, n_ch, ncp, v_transposed):
    c = jnp.dot(acc.astype(BF16), proj_ref[kv, g].astype(BF16), preferred_element_type=F32)
    if kv == 0:
        kcmp_aug[g, 0:n_ch, 0:NSA_HD] = c.astype(BF16)
        n = lax.broadcasted_iota(jnp.int32, (ncp, LANES), 0)
        kcmp_aug[g, :, NSA_HD:2 * NSA_HD] = _key_features(n * CMP_STRIDE + (CMP_LEN - 1), False).astype(BF16)
    elif v_transposed:
        if ncp > n_ch:
            c = jnp.concatenate([c, jnp.zeros((ncp - n_ch, NSA_HD), F32)], axis=0)
        for blk in range(ncp // LANES):
            vcmp[g, :, blk * LANES:(blk + 1) * LANES] = c[blk * LANES:(blk + 1) * LANES].T.astype(BF16)
    else:
        vcmp[g, 0:n_ch, :] = c.astype(BF16)


def _masked_softmax(s, mask):
    s = jnp.where(mask, s, NEG_MASK)
    m = jnp.max(s, axis=1, keepdims=True)
    e = jnp.where(mask, jnp.exp(s - m), 0.0)
    return e / jnp.maximum(jnp.sum(e, axis=1, keepdims=True), 1e-30)


def _select_blocks(psum, t0, n_cmp, n_sel, queries_on_lanes):
    ncp = psum.shape[0] if queries_on_lanes else psum.shape[1]
    nsp = -(-n_sel // 8) * 8
    j = lax.broadcasted_iota(jnp.int32, (nsp, ncp), 0)
    n = lax.broadcasted_iota(jnp.int32, (nsp, ncp), 1)
    cover = ((n * CMP_STRIDE < j * SEL_BLOCK + SEL_BLOCK) & (n * CMP_STRIDE + CMP_LEN > j * SEL_BLOCK)
             & (n < n_cmp)).astype(F32)
    if queries_on_lanes:
        imp = jnp.dot(cover, psum, precision=lax.Precision.HIGHEST, preferred_element_type=F32)
    else:
        imp = lax.dot_general(cover, psum, NT, precision=lax.Precision.HIGHEST, preferred_element_type=F32)
    jq = lax.broadcasted_iota(jnp.int32, (nsp, LANES), 0)
    t = t0 + lax.broadcasted_iota(jnp.int32, (nsp, LANES), 1)
    cur = lax.shift_right_logical(t, 6)
    forced = (jq == 0) | (jq == cur) | (jq == cur - 1)
    valid = (jq * SEL_BLOCK <= t) & (jq < n_sel)
    score = jnp.where(valid, imp + jnp.where(forced, FORCE_BONUS, 0.0), -jnp.inf)
    jf = jq.astype(F32)
    sel = jnp.zeros((nsp, LANES), F32)
    for _ in range(min(SEL_TOPN, n_sel)):
        mx = jnp.max(score, axis=0, keepdims=True)
        first = jnp.min(jnp.where(score == mx, jf, 1.0e9), axis=0, keepdims=True)
        pick = jf == first
        sel = jnp.where(pick, 1.0, sel)
        score = jnp.where(pick, -jnp.inf, score)
    bias = jnp.where((sel > 0.5) | (jq >= n_sel), 0.0, NEG_SEL)
    bias = jnp.concatenate([bias, jnp.zeros((LANES - nsp, LANES), F32)], axis=0)
    return bias.T


def _cmp_branch(q_plain, kcmp_aug_g, vcmp_g, t_rows, n_cmp):
    s = lax.dot_general(q_plain, kcmp_aug_g, NT, preferred_element_type=F32)
    n = lax.broadcasted_iota(jnp.int32, s.shape, 1)
    mask = (n * CMP_STRIDE + (CMP_LEN - 1) <= t_rows) & (n < n_cmp)
    p = _masked_softmax(s, mask)
    return jnp.dot(p.astype(BF16), vcmp_g, preferred_element_type=F32), p


def _online_step_t(state, s_t, v_t):
    m, l, acc = state
    m_new = jnp.maximum(m, jnp.max(s_t, axis=0, keepdims=True))
    alpha = jnp.exp(m - m_new)
    p = jnp.exp(s_t - m_new)
    l = alpha * l + jnp.sum(p, axis=0, keepdims=True)
    acc = alpha * acc + jnp.dot(v_t, p.astype(BF16), preferred_element_type=F32)
    return m_new, l, acc


def _stack_heads(q_heads, feats):
    return jnp.concatenate(
        [jnp.concatenate([q, f.astype(BF16)], axis=1) for q, f in zip(q_heads, feats)], axis=0)


def _write_gated(o_ref, gates, g, rows, o_cmp, o_s, o_w):
    for r in range(NSA_GROUP):
        h = g * NSA_GROUP + r
        sl = slice(r * rows, (r + 1) * rows)
        c = 8 + h
        o = gates[:, c:c + 1] * o_cmp[sl] + gates[:, c + 8:c + 9] * o_s[sl] + gates[:, c + 16:c + 17] * o_w[sl]
        o_ref[:, h * NSA_HD:(h + 1) * NSA_HD] = o


def _nsa_prompt_body(q_ref, kc_ref, ks_ref, kw_ref, gate_ref, pw_ref, proj_ref, o_ref,
                     ks_aug, vs_t, kw_aug, vw_t, kcmp_aug, vcmp_t, a1_scr, *, seq):
    i = pl.program_id(1)
    tq = KEY_TILE
    n_ch = seq // CMP_STRIDE
    n_cmp = n_ch - 1
    n_sel = seq // SEL_BLOCK
    ncp = kcmp_aug.shape[1]
    G, HD = NSA_GROUP, NSA_HD
    Q = G * tq

    @pl.when(i == 0)
    def _build():
        pos = lax.broadcasted_iota(jnp.int32, (seq, LANES), 0)
        f_sel = _key_features(pos, True).astype(BF16)
        f_win = _key_features(pos, False).astype(BF16)
        for g in range(NSA_KV_HEADS):
            ks_aug[g, :, 0:HD] = ks_ref[:, g * HD:(g + 1) * HD].astype(BF16)
            ks_aug[g, :, HD:2 * HD] = f_sel
            kw_aug[g, :, 0:HD] = kw_ref[:, g * HD:(g + 1) * HD].astype(BF16)
            kw_aug[g, :, HD:2 * HD] = f_win
            for kt in range(seq // KEY_TILE):
                rows = slice(kt * KEY_TILE, (kt + 1) * KEY_TILE)
                vs_t[g, :, rows] = ks_ref[rows, (2 + g) * HD:(3 + g) * HD].T.astype(BF16)
                vw_t[g, :, rows] = kw_ref[rows, (2 + g) * HD:(3 + g) * HD].T.astype(BF16)
        kcmp_aug[...] = jnp.zeros(kcmp_aug.shape, BF16)
        a1_scr[n_ch:n_ch + 8, :] = jnp.zeros((8, HD), F32)
        for kv in range(2):
            for g in range(NSA_KV_HEADS):
                c0 = (kv * NSA_KV_HEADS + g) * HD
                a0, a1 = _compress_block_rows(
                    lambda j: kc_ref[:, j * 4 * HD + c0:j * 4 * HD + c0 + HD], pw_ref, kv, g)
                a1_scr[0:n_ch, :] = a1
                _finish_compress(a0 + a1_scr[pl.ds(1, n_ch), :], proj_ref, kcmp_aug, vcmp_t, kv, g, n_ch, ncp, True)

    t0 = i * tq
    key = lax.broadcasted_iota(jnp.int32, (KEY_CHUNK, Q), 0)
    t_cols = t0 + jnp.bitwise_and(lax.broadcasted_iota(jnp.int32, (KEY_CHUNK, Q), 1), tq - 1)
    gates_t = jax.nn.sigmoid(gate_ref[...]).T
    scale = HD ** -0.5

    n = lax.broadcasted_iota(jnp.int32, (ncp, Q), 0)
    t_c = t0 + jnp.bitwise_and(lax.broadcasted_iota(jnp.int32, (ncp, Q), 1), tq - 1)
    mask = (n * CMP_STRIDE + (CMP_LEN - 1) <= t_c) & (n < n_cmp)
    kv_groups = range(NSA_KV_HEADS)
    q_heads = [[(q_ref[:, (g * G + r) * HD:(g * G + r + 1) * HD] * scale).astype(BF16) for r in range(G)]
               for g in kv_groups]
    feats = [[_query_features((tq, LANES), g * G + r) for r in range(G)] for g in kv_groups]
    q_plain = [_stack_heads(q_heads[g], feats[g]) for g in kv_groups]
    s_c = [jnp.where(mask, lax.dot_general(kcmp_aug[g], q_plain[g], NT, preferred_element_type=F32), NEG_MASK)
           for g in kv_groups]
    e_c = [jnp.where(mask, jnp.exp(s - jnp.max(s, axis=0, keepdims=True)), 0.0) for s in s_c]
    p_c = [e / jnp.maximum(jnp.sum(e, axis=0, keepdims=True), 1e-30) for e in e_c]
    o_cmp = [jnp.dot(vcmp_t[g], p_c[g].astype(BF16), preferred_element_type=F32) for g in kv_groups]
    psum = [p[:, 0:tq] + p[:, tq:2 * tq] + p[:, 2 * tq:3 * tq] + p[:, 3 * tq:4 * tq] for p in p_c]
    bias = [_select_blocks(psum[g], t0, n_cmp, n_sel, True) for g in kv_groups]
    q_sel = [_stack_heads(q_heads[g], [f + bias[g] for f in feats[g]]) for g in kv_groups]

    last = t0 // KEY_CHUNK
    first_w = jnp.maximum(t0 - WINDOW, 0) // KEY_CHUNK
    init = (jnp.full((1, Q), NEG_MASK, F32), jnp.zeros((1, Q), F32), jnp.zeros((HD, Q), F32))

    def scores(c, k_aug, q, g, valid):
        off = pl.multiple_of(c * KEY_CHUNK, KEY_CHUNK)
        s_t = lax.dot_general(k_aug[g, pl.ds(off, KEY_CHUNK), :], q, NT, preferred_element_type=F32)
        return s_t if valid is None else jnp.where(valid(off + key), s_t, NEG_MASK)

    def values(c, v_t, g):
        return v_t[g, :, pl.ds(pl.multiple_of(c * KEY_CHUNK, KEY_CHUNK), KEY_CHUNK)]

    def causal(kpos):
        return kpos <= t_cols

    def band(kpos):
        return (kpos <= t_cols) & (t_cols - kpos <= WINDOW)

    def early(c, sel):
        s = [scores(c, ks_aug, q_sel[g], g, None) for g in kv_groups]
        return tuple(_online_step_t(sel[g], s[g], values(c, vs_t, g)) for g in kv_groups)

    def late(c, sts):
        sel, win = sts
        s_sel = [scores(c, ks_aug, q_sel[g], g, causal) for g in kv_groups]
        s_win = [scores(c, kw_aug, q_plain[g], g, band) for g in kv_groups]
        sel = tuple(_online_step_t(sel[g], s_sel[g], values(c, vs_t, g)) for g in kv_groups)
        win = tuple(_online_step_t(win[g], s_win[g], values(c, vw_t, g)) for g in kv_groups)
        return sel, win

    inits = (init,) * NSA_KV_HEADS
    sel = lax.fori_loop(0, first_w, early, inits)
    sel, win = lax.fori_loop(first_w, last + 1, late, (sel, inits))

    for g in kv_groups:
        o_s = sel[g][2] / sel[g][1]
        o_w = win[g][2] / win[g][1]
        for r in range(G):
            h = g * G + r
            cols = slice(r * tq, (r + 1) * tq)
            c = 8 + h
            o_t = (gates_t[c:c + 1, :] * o_cmp[g][:, cols] + gates_t[c + 8:c + 9, :] * o_s[:, cols]
                   + gates_t[c + 16:c + 17, :] * o_w[:, cols])
            o_ref[:, h * HD:(h + 1) * HD] = o_t.T


def nsa_prompt(zb, zs, kv_cmp, cmp_pos_w, cmp_proj, *, batch, seq, col_q, col_slc, col_win, out_rows=None):
    assert seq % KEY_CHUNK == 0 and seq // SEL_BLOCK <= SEL_COLS
    nq = seq // KEY_TILE
    n_ch = seq // CMP_STRIDE
    ncp = -(-n_ch // LANES) * LANES
    qw, kvw = NSA_HEADS * NSA_HD, 4 * NSA_HD
    pw = cmp_pos_w.reshape(2, CMP_LEN, 2 * NSA_HD)
    return pl.pallas_call(
        functools.partial(_nsa_prompt_body, seq=seq),
        out_shape=jax.ShapeDtypeStruct((out_rows or batch * seq, qw), F32),
        grid=(batch, nq),
        in_specs=[
            pl.BlockSpec((KEY_TILE, qw), lambda b, i: (b * nq + i, col_q // qw)),
            pl.BlockSpec((n_ch, CMP_STRIDE * kvw), lambda b, i: (b, 0)),
            pl.BlockSpec((seq, kvw), lambda b, i: (b, col_slc // kvw)),
            pl.BlockSpec((seq, kvw), lambda b, i: (b, col_win // kvw)),
            pl.BlockSpec((KEY_TILE, LANES), lambda b, i: (b * nq + i, 0)),
            pl.BlockSpec((2, CMP_LEN, 2 * NSA_HD), lambda b, i: (0, 0, 0)),
            pl.BlockSpec((2, NSA_KV_HEADS, NSA_HD, NSA_HD), lambda b, i: (0, 0, 0, 0)),
        ],
        out_specs=pl.BlockSpec((KEY_TILE, qw), lambda b, i: (b * nq + i, 0)),
        scratch_shapes=[
            pltpu.VMEM((NSA_KV_HEADS, seq, 2 * NSA_HD), BF16),
            pltpu.VMEM((NSA_KV_HEADS, NSA_HD, seq), BF16),
            pltpu.VMEM((NSA_KV_HEADS, seq, 2 * NSA_HD), BF16),
            pltpu.VMEM((NSA_KV_HEADS, NSA_HD, seq), BF16),
            pltpu.VMEM((NSA_KV_HEADS, ncp, 2 * NSA_HD), BF16),
            pltpu.VMEM((NSA_KV_HEADS, NSA_HD, ncp), BF16),
            pltpu.VMEM((n_ch + 8, NSA_HD), F32),
        ],
        compiler_params=_params("parallel", "arbitrary"),
        name="nsa_prompt",
    )(zb, kv_cmp.reshape(-1, CMP_STRIDE * kvw), zb, zb, zs, pw, cmp_proj)


def _nsa_sample_body(pt_ref, q_ref, ksn_ref, kwn_ref, gate_ref, wprev_ref, pw_ref, proj_ref, *rest,
                     ts, past, n_pages):
    del pt_ref
    cmp_pages = rest[:n_pages]
    slc_pages = rest[n_pages:2 * n_pages]
    o_ref = rest[2 * n_pages + 1]
    ks_aug, vs, kw_aug, vw, kcmp_aug, vcmp, acc_scr = rest[2 * n_pages + 2:]
    G, HD = NSA_GROUP, NSA_HD
    page = slc_pages[0].shape[0] // 4
    kp = ks_aug.shape[1]
    wprev = wprev_ref.shape[0] // 4
    wp = kw_aug.shape[1]
    win_pos0 = past - wprev
    n_ch = (past + ts) // CMP_STRIDE
    n_cmp = n_ch - 1
    n_sel = -(-(past + ts) // SEL_BLOCK)
    ncp = kcmp_aug.shape[1]
    ch_per_page = page // CMP_STRIDE

    @pl.when(pl.program_id(0) == 0)
    def _constants():
        pos = lax.broadcasted_iota(jnp.int32, (kp, LANES), 0)
        f_sel = _key_features(pos, True).astype(BF16)
        posw = win_pos0 + lax.broadcasted_iota(jnp.int32, (wp, LANES), 0)
        f_win = _key_features(posw, False).astype(BF16)
        for g in range(NSA_KV_HEADS):
            ks_aug[g, :, HD:2 * HD] = f_sel
            kw_aug[g, :, HD:2 * HD] = f_win
        kcmp_aug[...] = jnp.zeros(kcmp_aug.shape, BF16)
        vcmp[...] = jnp.zeros(vcmp.shape, BF16)

    def with_tail(new_rows):
        return jnp.concatenate([new_rows, jnp.zeros((KEY_TILE - ts, HD), F32)], axis=0).astype(BF16)

    for g in range(NSA_KV_HEADS):
        for p in range(n_pages):
            rows = slice(p * page, (p + 1) * page)
            ks_aug[g, rows, 0:HD] = slc_pages[p][pl.ds(g, page, stride=4), :].astype(BF16)
            vs[g, rows, :] = slc_pages[p][pl.ds(2 + g, page, stride=4), :].astype(BF16)
        ks_aug[g, past:past + KEY_TILE, 0:HD] = with_tail(ksn_ref[:, g * HD:(g + 1) * HD])
        vs[g, past:past + KEY_TILE, :] = with_tail(ksn_ref[:, (2 + g) * HD:(3 + g) * HD])
        kw_aug[g, 0:wprev, 0:HD] = wprev_ref[pl.ds(g, wprev, stride=4), :].astype(BF16)
        vw[g, 0:wprev, :] = wprev_ref[pl.ds(2 + g, wprev, stride=4), :].astype(BF16)
        kw_aug[g, wprev:wprev + KEY_TILE, 0:HD] = with_tail(kwn_ref[:, g * HD:(g + 1) * HD])
        vw[g, wprev:wprev + KEY_TILE, :] = with_tail(kwn_ref[:, (2 + g) * HD:(3 + g) * HD])

    for kv in range(2):
        for g in range(NSA_KV_HEADS):
            c = kv * NSA_KV_HEADS + g
            cols = slice(g * HD, (g + 1) * HD)
            w0 = jnp.concatenate([pw_ref[kv, 0:CMP_STRIDE, cols]] * ch_per_page, axis=0)
            w1 = jnp.concatenate([pw_ref[kv, CMP_STRIDE:CMP_LEN, cols]] * ch_per_page, axis=0)
            for p in range(n_pages):
                x = cmp_pages[p][pl.ds(c, page, stride=4), :]
                if p + 1 < n_pages:
                    nxt = cmp_pages[p + 1][pl.ds(c, CMP_STRIDE, stride=4), :] * w1[0:CMP_STRIDE]
                else:
                    nxt = jnp.zeros((CMP_STRIDE, HD), F32)
                z = x * w0 + jnp.concatenate([(x * w1)[CMP_STRIDE:], nxt], axis=0)
                acc_scr[c, p * ch_per_page:(p + 1) * ch_per_page, :] = jnp.sum(
                    z.reshape(ch_per_page, CMP_STRIDE, HD), axis=1)
            _finish_compress(acc_scr[c, 0:n_ch, :], proj_ref, kcmp_aug, vcmp, kv, g, n_ch, ncp, False)

    R = G * ts
    gates = jax.nn.sigmoid(gate_ref[...])
    scale = HD ** -0.5

    def t_of(shape):
        return past + jnp.bitwise_and(lax.broadcasted_iota(jnp.int32, shape, 0), ts - 1)

    def softmax_pv(s, v):
        m = jnp.max(s, axis=1, keepdims=True)
        e = jnp.exp(s - m)
        return jnp.dot(e.astype(BF16), v, preferred_element_type=F32) / jnp.sum(e, axis=1, keepdims=True)

    for g in range(NSA_KV_HEADS):
        q_heads = [(q_ref[:, (g * G + r) * HD:(g * G + r + 1) * HD] * scale).astype(BF16) for r in range(G)]
        feats = [_query_features((ts, LANES), g * G + r) for r in range(G)]
        q_plain = _stack_heads(q_heads, feats)
        o_cmp, p_c = _cmp_branch(q_plain, kcmp_aug[g], vcmp[g], t_of((R, ncp)), n_cmp)
        psum = p_c[0:ts] + p_c[ts:2 * ts] + p_c[2 * ts:3 * ts] + p_c[3 * ts:4 * ts]
        psum = jnp.concatenate([psum, jnp.zeros((LANES - ts, ncp), F32)], axis=0)
        bias = _select_blocks(psum, past, n_cmp, n_sel, False)[0:ts]
        q_sel = _stack_heads(q_heads, [f + bias for f in feats])

        s = lax.dot_general(q_sel, ks_aug[g], NT, preferred_element_type=F32)
        s = jnp.where(lax.broadcasted_iota(jnp.int32, (R, kp), 1) <= t_of((R, kp)), s, NEG_MASK)
        o_s = softmax_pv(s, vs[g])

        s = lax.dot_general(q_plain, kw_aug[g], NT, preferred_element_type=F32)
        idx = lax.broadcasted_iota(jnp.int32, (R, wp), 1)
        dist = t_of((R, wp)) - (win_pos0 + idx)
        s = jnp.where((idx < wprev + ts) & (dist >= 0) & (dist <= WINDOW), s, NEG_MASK)
        o_w = softmax_pv(s, vw[g])

        _write_gated(o_ref, gates, g, ts, o_cmp, o_s, o_w)


def nsa_sample(zb, zs, cache_cmp, cache_slc, state_win, page_table, cmp_pos_w, cmp_proj, y_init, *,
               layer, row0, batch, ts, col_q, col_slc, col_win):
    depth, n_pool, page = cache_cmp.shape[:3]
    n_pages = page_table.shape[1]
    past = n_pages * page
    wprev = state_win.shape[2]
    assert ts & (ts - 1) == 0 and ts <= KEY_TILE and row0 % ts == 0
    assert past % KEY_TILE == 0 and (past + ts) // CMP_STRIDE == past // CMP_STRIDE
    assert page % CMP_STRIDE == 0 and -(-(past + ts) // SEL_BLOCK) <= SEL_COLS and wprev % 16 == 0
    qw, kvw = NSA_HEADS * NSA_HD, 4 * NSA_HD
    n_ch = past // CMP_STRIDE
    ncp = -(-n_ch // LANES) * LANES
    r0 = row0 // ts
    pw = cmp_pos_w.reshape(2, CMP_LEN, 2 * NSA_HD)
    cmp_view = cache_cmp.reshape(depth, n_pool, page * 4, NSA_HD)
    slc_view = cache_slc.reshape(depth, n_pool, page * 4, NSA_HD)
    win_view = state_win.reshape(depth, batch, wprev * 4, NSA_HD)

    def page_map(p):
        return lambda b, pt: (layer, pt[b * n_pages + p], 0, 0)

    in_specs = [
        pl.BlockSpec((ts, qw), lambda b, pt: (r0 + b, col_q // qw)),
        pl.BlockSpec((ts, kvw), lambda b, pt: (r0 + b, col_slc // kvw)),
        pl.BlockSpec((ts, kvw), lambda b, pt: (r0 + b, col_win // kvw)),
        pl.BlockSpec((ts, LANES), lambda b, pt: (r0 + b, 0)),
        pl.BlockSpec((None, None, wprev * 4, NSA_HD), lambda b, pt: (layer, b, 0, 0)),
        pl.BlockSpec((2, CMP_LEN, 2 * NSA_HD), lambda b, pt: (0, 0, 0)),
        pl.BlockSpec((2, NSA_KV_HEADS, NSA_HD, NSA_HD), lambda b, pt: (0, 0, 0, 0)),
    ]
    in_specs += [pl.BlockSpec((None, None, page * 4, NSA_HD), page_map(p % n_pages)) for p in range(2 * n_pages)]
    in_specs.append(pl.BlockSpec(memory_space=pl.ANY))
    return pl.pallas_call(
        functools.partial(_nsa_sample_body, ts=ts, past=past, n_pages=n_pages),
        out_shape=jax.ShapeDtypeStruct(y_init.shape, F32),
        grid_spec=pltpu.PrefetchScalarGridSpec(
            num_scalar_prefetch=1,
            grid=(batch,),
            in_specs=in_specs,
            out_specs=pl.BlockSpec((ts, qw), lambda b, pt: (r0 + b, 0)),
            scratch_shapes=[
                pltpu.VMEM((NSA_KV_HEADS, past + KEY_TILE, 2 * NSA_HD), BF16),
                pltpu.VMEM((NSA_KV_HEADS, past + KEY_TILE, NSA_HD), BF16),
                pltpu.VMEM((NSA_KV_HEADS, wprev + KEY_TILE, 2 * NSA_HD), BF16),
                pltpu.VMEM((NSA_KV_HEADS, wprev + KEY_TILE, NSA_HD), BF16),
                pltpu.VMEM((NSA_KV_HEADS, ncp, 2 * NSA_HD), BF16),
                pltpu.VMEM((NSA_KV_HEADS, ncp, NSA_HD), BF16),
                pltpu.VMEM((2 * NSA_KV_HEADS, n_ch, NSA_HD), F32),
            ],
        ),
        input_output_aliases={8 + 2 * n_pages: 0},
        compiler_params=_params("arbitrary"),
        name="nsa_sample",
    )(page_table.reshape(-1), zb, zb, zb, zs, win_view, pw, cmp_proj,
      *([cmp_view] * n_pages), *([slc_view] * n_pages), y_init)


ZB_NQ, ZB_POOL, ZB_MQ, ZB_MK, ZB_MV, ZB_MO, ZB_CMP, ZB_SLC, ZB_WIN, ZB_END = (
    0, 1024, 1536, 2048, 2560, 3072, 3584, 4096, 4608, 5120)
W_POOL, W_MI, W_NQ, W_CMP, W_NG, W_END = 0, 2560, 2568, 3592, 5128, 5152


def _split_w_in(w_in_l):
    big = jnp.concatenate([w_in_l[:, W_NQ:W_CMP], w_in_l[:, W_POOL:W_MI], w_in_l[:, W_CMP:W_NG]], axis=1)
    small = jnp.concatenate([w_in_l[:, W_MI:W_NQ], w_in_l[:, W_NG:W_END]], axis=1)
    small = jnp.pad(small, ((0, 0), (0, LANES - small.shape[1])))
    return big.astype(BF16), small.astype(BF16)


def kernel(x_prompt, x_sample, cache_kv_cmp, cache_kv_slc, state_kv_win, state_pool, state_mlstm_C, state_mlstm_n, state_mlstm_m, page_table, ffn1_norm, ffn1_w_gate, ffn1_w_up, ffn1_w_down, mix_norm, w_in, w_out, pool_w, pool_scale, mlstm_if_bias, mlstm_norm, nsa_cmp_pos_w, nsa_cmp_proj, ffn2_norm, ffn2_w_gate, ffn2_w_up, ffn2_w_down, final_norm):
    bp, tp, d = x_prompt.shape
    bs, ts, _ = x_sample.shape
    depth = w_in.shape[0]
    mp, ms = bp * tp, bs * ts
    m_all = mp + ms
    past_len = page_table.shape[1] * cache_kv_cmp.shape[2]
    pd = pool_scale.shape[1]
    kv_row = (2, NSA_KV_HEADS, NSA_HD)
    x = jnp.concatenate([x_prompt.reshape(mp, d), x_sample.reshape(ms, d)], axis=0)
    zeros = lambda *s: jnp.zeros(s, F32)
    outs = [[] for _ in range(14)]
    for l in range(depth):
        x = ffn_half_step(x, ffn1_norm[l], ffn1_w_gate[l].astype(BF16), ffn1_w_up[l].astype(BF16),
                          ffn1_w_down[l].astype(BF16))
        w_big, w_small = _split_w_in(w_in[l])
        zb = norm_project(x, mix_norm[l], w_big)
        zs = norm_project(x, mix_norm[l], w_small)
        kv_cmp = zb[:, ZB_CMP:ZB_SLC]
        kv_slc = zb[:, ZB_SLC:ZB_WIN]
        kv_win = zb[:, ZB_WIN:ZB_END]
        z_pool = zb[:, ZB_POOL:ZB_MQ]

        y_pool = pool_prompt(zb, pool_w[l], pool_scale[l], batch=bp, seq=tp, col=ZB_POOL, out_rows=m_all)
        y_m, p_c, p_n, p_m = mlstm_mix(
            zb, zs, mlstm_if_bias[l], mlstm_norm[l], zeros(bp, MLSTM_HEADS, MLSTM_HD, MLSTM_HD),
            zeros(bp, MLSTM_HEADS, MLSTM_HD), zeros(bp, MLSTM_HEADS), row0=0, batch=bp, seq=tp, col_q=ZB_MQ,
            out_rows=m_all)
        y_nsa = nsa_prompt(zb, zs, kv_cmp, nsa_cmp_pos_w[l], nsa_cmp_proj[l], batch=bp, seq=tp,
                           col_q=ZB_NQ, col_slc=ZB_SLC, col_win=ZB_WIN, out_rows=m_all)

        pool_full = jnp.concatenate([zeros(bs, POOL_HALO - POOL_BUF, pd), state_pool[l],
                                     z_pool[mp:].reshape(bs, ts, pd)], axis=1)
        y_pool = pool_sample(pool_full, pool_w[l], pool_scale[l], y_pool, pos0=past_len, row0=mp)
        y_m, s_c, s_n, s_m = mlstm_mix(
            zb, zs, mlstm_if_bias[l], mlstm_norm[l], state_mlstm_C[l], state_mlstm_n[l], state_mlstm_m[l],
            row0=mp, batch=bs, seq=ts, col_q=ZB_MQ, y_init=y_m)
        y_nsa = nsa_sample(zb, zs, cache_kv_cmp, cache_kv_slc, state_kv_win, page_table,
                           nsa_cmp_pos_w[l], nsa_cmp_proj[l], y_nsa, layer=l, row0=mp, batch=bs, ts=ts,
                           col_q=ZB_NQ, col_slc=ZB_SLC, col_win=ZB_WIN)

        x = out_project(x, y_pool, y_m, y_nsa, w_out[l].astype(BF16))
        x = ffn_half_step(x, ffn2_norm[l], ffn2_w_gate[l].astype(BF16), ffn2_w_up[l].astype(BF16),
                          ffn2_w_down[l].astype(BF16), gf=final_norm if l == depth - 1 else None)

        wp_rows = min(WINDOW, tp)
        win_s = jnp.concatenate([state_kv_win[l], kv_win[mp:].reshape(bs, ts, *kv_row)], axis=1)
        pool_p = jnp.concatenate([zeros(bp, POOL_BUF, pd), z_pool[:mp].reshape(bp, tp, pd)], axis=1)
        layer_out = (
            kv_cmp[:mp].reshape(bp, tp, *kv_row), kv_slc[:mp].reshape(bp, tp, *kv_row),
            kv_win[:mp].reshape(bp, tp, *kv_row)[:, tp - wp_rows:], pool_p[:, -POOL_BUF:], p_c, p_n, p_m,
            kv_cmp[mp:].reshape(bs, ts, *kv_row), kv_slc[mp:].reshape(bs, ts, *kv_row),
            win_s[:, -min(WINDOW, win_s.shape[1]):], pool_full[:, -POOL_BUF:], s_c, s_n, s_m)
        for acc, a in zip(outs, layer_out):
            acc.append(a)
    y_prompt = x[:mp].reshape(bp, tp, d)
    y_sample = x[mp:].reshape(bs, ts, d)
    return (y_prompt, y_sample, *[jnp.stack(a) for a in outs])
```

```python
import functools
import math

import jax
import jax.numpy as jnp
from jax import lax
from jax.experimental import pallas as pl
from jax.experimental.pallas import tpu as pltpu

F32 = jnp.float32
BF16 = jnp.bfloat16
EPS = 1e-6

VMEM_LIMIT_BYTES = 56 * 1024 * 1024
LANES = 128

POOL_WINDOWS = (2, 4, 8, 16)
POOL_BUF = 15
MLSTM_HEADS = 4
MLSTM_HD = 128
MLSTM_CHUNK = 64
NSA_HD = 128
NSA_HEADS = 8
NSA_KV_HEADS = 2
NSA_GROUP = 4
CMP_LEN = 32
CMP_STRIDE = 16
SEL_BLOCK = 64
SEL_TOPN = 16
WINDOW = 512
FORCE_BONUS = 1.0e4

NEG_MASK = -1.0e30
NT = (((1,), (1,)), ((), ()))
TN = (((0,), (0,)), ((), ()))


def _pick_tile(n, pref):
    t = pref
    while t > 8 and n % t:
        t //= 2
    assert n % t == 0, (n, pref)
    return t


def _params(*sem):
    return pltpu.CompilerParams(dimension_semantics=sem, vmem_limit_bytes=VMEM_LIMIT_BYTES)


def _drop_alias_ref(body, index, *refs):
    return body(*refs[:index], *refs[index + 1:])


def _rms_rows(x, g):
    ms = jnp.mean(x * x, axis=-1, keepdims=True)
    return x * lax.rsqrt(ms + EPS) * g


def _ffn_body(x_ref, g_ref, wg_ref, wu_ref, wd_ref, gf_ref, o_ref, n_scr, *, final_norm):
    f = pl.program_id(1)

    @pl.when(f == 0)
    def _():
        x = x_ref[...]
        n_scr[...] = _rms_rows(x, g_ref[...]).astype(BF16)
        o_ref[...] = x

    n = n_scr[...]
    hg = jnp.dot(n, wg_ref[...], preferred_element_type=F32)
    hu = jnp.dot(n, wu_ref[...], preferred_element_type=F32)
    h = (hg * jax.nn.sigmoid(hg) * hu).astype(BF16)
    o_ref[...] += 0.5 * jnp.dot(h, wd_ref[...], preferred_element_type=F32)

    if final_norm:
        @pl.when(f == pl.num_programs(1) - 1)
        def _():
            o_ref[...] = _rms_rows(o_ref[...], gf_ref[...])


def ffn_half_step(x, g, wg, wu, wd, layer, gf=None):
    m, d = x.shape
    fdim = wg.shape[2]
    tm = _pick_tile(m, 512)
    tf = _pick_tile(fdim, 512)
    final_norm = gf is not None
    if gf is None:
        gf = g
    return pl.pallas_call(
        functools.partial(_ffn_body, final_norm=final_norm),
        out_shape=jax.ShapeDtypeStruct((m, d), F32),
        grid=(m // tm, fdim // tf),
        in_specs=[
            pl.BlockSpec((tm, d), lambda i, f: (i, 0)),
            pl.BlockSpec((1, d), lambda i, f: (0, 0)),
            pl.BlockSpec((None, d, tf), lambda i, f: (layer, 0, f)),
            pl.BlockSpec((None, d, tf), lambda i, f: (layer, 0, f)),
            pl.BlockSpec((None, tf, d), lambda i, f: (layer, f, 0)),
            pl.BlockSpec((1, d), lambda i, f: (0, 0)),
        ],
        out_specs=pl.BlockSpec((tm, d), lambda i, f: (i, 0)),
        scratch_shapes=[pltpu.VMEM((tm, d), BF16)],
        compiler_params=_params("parallel", "arbitrary"),
        name="ffn_half_step",
    )(x, g.reshape(1, d), wg, wu, wd, gf.reshape(1, d))


def _inproj_body(x_ref, g_ref, w_ref, o_ref, n_scr):
    @pl.when(pl.program_id(1) == 0)
    def _():
        n_scr[...] = _rms_rows(x_ref[...], g_ref[...]).astype(BF16)

    o_ref[...] = jnp.dot(n_scr[...], w_ref[...], preferred_element_type=F32)


def norm_project(x, g, w):
    m, d = x.shape
    n = w.shape[1]
    tm = _pick_tile(m, 1024)
    tn = _pick_tile(n, 1024)
    return pl.pallas_call(
        _inproj_body,
        out_shape=jax.ShapeDtypeStruct((m, n), F32),
        grid=(m // tm, n // tn),
        in_specs=[
            pl.BlockSpec((tm, d), lambda i, j: (i, 0)),
            pl.BlockSpec((1, d), lambda i, j: (0, 0)),
            pl.BlockSpec((d, tn), lambda i, j: (0, j)),
        ],
        out_specs=pl.BlockSpec((tm, tn), lambda i, j: (i, j)),
        scratch_shapes=[pltpu.VMEM((tm, d), BF16)],
        compiler_params=_params("parallel", "arbitrary"),
        name="norm_project",
    )(x, g.reshape(1, d), w)


def _outproj_body(x_ref, ya_ref, yb_ref, yc_ref, wa_ref, wb_ref, wc_ref, o_ref):
    acc = x_ref[...]
    acc += jnp.dot(ya_ref[...].astype(BF16), wa_ref[...], preferred_element_type=F32)
    acc += jnp.dot(yb_ref[...].astype(BF16), wb_ref[...], preferred_element_type=F32)
    acc += jnp.dot(yc_ref[...].astype(BF16), wc_ref[...], preferred_element_type=F32)
    o_ref[...] = acc


def out_project(x, y_pool, y_mlstm, y_nsa, w_out, layer):
    m, d = x.shape
    da, db, dc = y_pool.shape[1], y_mlstm.shape[1], y_nsa.shape[1]
    assert da == db and dc % da == 0
    tm = _pick_tile(m, 512)
    tn = d
    return pl.pallas_call(
        _outproj_body,
        out_shape=jax.ShapeDtypeStruct((m, d), F32),
        grid=(m // tm, d // tn),
        in_specs=[
            pl.BlockSpec((tm, tn), lambda i, j: (i, j)),
            pl.BlockSpec((tm, da), lambda i, j: (i, 0)),
            pl.BlockSpec((tm, db), lambda i, j: (i, 0)),
            pl.BlockSpec((tm, dc), lambda i, j: (i, 0)),
            pl.BlockSpec((None, da, tn), lambda i, j: (layer, 0, j)),
            pl.BlockSpec((None, db, tn), lambda i, j: (layer, 1, j)),
            pl.BlockSpec((None, dc, tn), lambda i, j: (layer, (da + db) // dc, j)),
        ],
        out_specs=pl.BlockSpec((tm, tn), lambda i, j: (i, j)),
        compiler_params=_params("parallel", "arbitrary"),
        name="out_project",
    )(x, y_pool, y_mlstm, y_nsa, w_out, w_out, w_out)


POOL_HALO = 16


def _pool_group(load, g, n_avail, w_ref, sc_ref):
    w = POOL_WINDOWS[g]
    z = load(0)
    acc = z
    for j in range(1, w):
        acc = acc + load(j)
    d = acc / jnp.minimum(n_avail, w).astype(F32) - z
    lead = d.shape[:-1]
    gd = d.shape[-1]
    y = jnp.dot(d.reshape(-1, gd).astype(BF16), w_ref[g].astype(BF16), preferred_element_type=F32)
    return (y * sc_ref[:, g * gd:(g + 1) * gd]).reshape(*lead, gd)


def _pool_prompt_body(z_ref, w_ref, sc_ref, o_ref, full_scr, *, chunk):
    seq, pd = z_ref.shape
    gd = pd // len(POOL_WINDOWS)
    full_scr[0:POOL_HALO, :] = jnp.zeros((POOL_HALO, pd), F32)
    full_scr[POOL_HALO:POOL_HALO + seq, :] = z_ref[...]
    for c in range(seq // chunk):
        n_avail = c * chunk + 1 + lax.broadcasted_iota(jnp.int32, (chunk, gd), 0)
        for g in range(len(POOL_WINDOWS)):
            load = lambda j: full_scr[pl.ds(POOL_HALO + c * chunk - j, chunk), g * gd:(g + 1) * gd]
            o_ref[c * chunk:(c + 1) * chunk, g * gd:(g + 1) * gd] = _pool_group(load, g, n_avail, w_ref, sc_ref)


def pool_prompt(zb, pool_w, pool_scale, *, batch, seq, col, out_rows=None):
    pd = pool_scale.shape[0]
    chunk = _pick_tile(seq, 256)
    return pl.pallas_call(
        functools.partial(_pool_prompt_body, chunk=chunk),
        out_shape=jax.ShapeDtypeStruct((out_rows or batch * seq, pd), F32),
        grid=(batch,),
        in_specs=[
            pl.BlockSpec((seq, pd), lambda b: (b, col // pd)),
            pl.BlockSpec(pool_w.shape, lambda b: (0, 0, 0)),
            pl.BlockSpec((1, pd), lambda b: (0, 0)),
        ],
        out_specs=pl.BlockSpec((seq, pd), lambda b: (b, 0)),
        scratch_shapes=[pltpu.VMEM((POOL_HALO + seq, pd), F32)],
        compiler_params=_params("parallel"),
        name="pool_prompt",
    )(zb, pool_w, pool_scale.reshape(1, pd))


def _pool_sample_body(full_ref, w_ref, sc_ref, o_ref, *, pos0):
    bt, rows, pd = full_ref.shape
    ts = rows - POOL_HALO
    gd = pd // len(POOL_WINDOWS)
    n_avail = pos0 + 1 + lax.broadcasted_iota(jnp.int32, (bt, ts, gd), 1)
    for g in range(len(POOL_WINDOWS)):
        load = lambda j: full_ref[:, pl.ds(POOL_HALO - j, ts), g * gd:(g + 1) * gd]
        o_ref[:, g * gd:(g + 1) * gd] = _pool_group(load, g, n_avail, w_ref, sc_ref).reshape(bt * ts, gd)


def pool_sample(full, pool_w, pool_scale, y_init, *, pos0, row0):
    batch, rows, pd = full.shape
    ts = rows - POOL_HALO
    bt = math.gcd(batch, 32)
    assert ts % 8 == 0 and row0 % (bt * ts) == 0
    r0 = row0 // (bt * ts)
    return pl.pallas_call(
        functools.partial(_drop_alias_ref, functools.partial(_pool_sample_body, pos0=pos0), 3),
        out_shape=jax.ShapeDtypeStruct(y_init.shape, F32),
        grid=(batch // bt,),
        in_specs=[
            pl.BlockSpec((bt, rows, pd), lambda b: (b, 0, 0)),
            pl.BlockSpec(pool_w.shape, lambda b: (0, 0, 0)),
            pl.BlockSpec((1, pd), lambda b: (0, 0)),
            pl.BlockSpec(memory_space=pl.ANY),
        ],
        out_specs=pl.BlockSpec((bt * ts, pd), lambda b: (r0 + b, 0)),
        input_output_aliases={3: 0},
        compiler_params=_params("parallel"),
        name="pool_sample",
    )(full, pool_w, pool_scale.reshape(1, pd), y_init)


def _log_sigmoid(x):
    return jnp.minimum(x, 0.0) - jnp.log1p(jnp.exp(-jnp.abs(x)))


def _mlstm_body(q_ref, k_ref, v_ref, og_ref, g_ref, bias_ref, gn_ref, c0_ref, n0_ref, m0_ref,
                y_ref, c_ref, n_ref, m_ref, *, L):
    nseq = q_ref.shape[0] // L
    H, D = MLSTM_HEADS, MLSTM_HD
    hi = lax.Precision.HIGHEST

    @pl.when(pl.program_id(1) == 0)
    def _():
        c_ref[...] = c0_ref[...]
        n_ref[...] = n0_ref[...]
        m_ref[...] = m0_ref[...]

    sel = (lax.broadcasted_iota(jnp.int32, (8, LANES), 0) == lax.broadcasted_iota(jnp.int32, (8, LANES), 1)).astype(F32)
    li = lax.broadcasted_iota(jnp.int32, (L, L), 0)
    si = lax.broadcasted_iota(jnp.int32, (L, L), 1)
    causal = li >= si
    lane = lax.broadcasted_iota(jnp.int32, (1, LANES), 1)
    chains = [(s, h) for s in range(nseq) for h in range(H)]

    gates = []
    for s in range(nseq):
        rows = slice(s * L, (s + 1) * L)
        gz = g_ref[rows, :] + bias_ref[...]
        gz_rows = lax.dot_general(sel, gz, NT, precision=hi, preferred_element_type=F32)
        b_cols = jnp.dot(causal.astype(F32), _log_sigmoid(gz), precision=hi, preferred_element_type=F32)
        b_rows = jnp.dot(_log_sigmoid(gz_rows), (li <= si).astype(F32), precision=hi, preferred_element_type=F32)
        gates.append((gz, gz_rows, b_cols, b_rows, m_ref[s]))

    qk, qc, state = {}, {}, {}
    for s, h in chains:
        rows, cols = slice(s * L, (s + 1) * L), slice(h * D, (h + 1) * D)
        qh = q_ref[rows, cols]
        kh = k_ref[rows, cols] * (D ** -0.5)
        ch = c_ref[s, h]
        nh = n_ref[s, h:h + 1, :]
        qb, kb = qh.astype(BF16), kh.astype(BF16)
        qk[s, h] = lax.dot_general(qb, kb, NT, preferred_element_type=F32)
        qc[s, h] = lax.dot_general(qb, ch.astype(BF16), NT, preferred_element_type=F32)
        state[s, h] = (qh, kh, kb, ch, nh)

    sm, stats = {}, {}
    for s, h in chains:
        gz, gz_rows, b_cols, b_rows, m_all = gates[s]
        bc = b_cols[:, H + h:H + h + 1]
        ic = gz[:, h:h + 1]
        br = b_rows[H + h:H + h + 1, :]
        ir = gz_rows[h:h + 1, :]
        m_prev = m_all[:, h:h + 1]
        dmat = jnp.where(causal, bc - br + ir, NEG_MASK)
        inter = bc + m_prev
        m_t = jnp.maximum(inter, jnp.max(dmat, axis=1, keepdims=True))
        sm[s, h] = qk[s, h] * jnp.exp(dmat - m_t)
        m_new = m_t[L - 1:L, :]
        b_last = bc[L - 1:L, :]
        stats[s, h] = (jnp.exp(inter - m_t), m_t, m_new, jnp.exp(b_last + m_prev - m_new),
                       jnp.exp(b_last - bc + ic - m_new))

    num, c_new = {}, {}
    for s, h in chains:
        rows, cols = slice(s * L, (s + 1) * L), slice(h * D, (h + 1) * D)
        a_inter, _, _, decay, w_col = stats[s, h]
        qh, kh, kb, ch, nh = state[s, h]
        vh = v_ref[rows, cols]
        num[s, h] = jnp.dot(sm[s, h].astype(BF16), vh.astype(BF16), preferred_element_type=F32) + a_inter * qc[s, h]
        c_new[s, h] = decay * ch + lax.dot_general((vh * w_col).astype(BF16), kb, TN, preferred_element_type=F32)

    m_out = [gates[s][4] for s in range(nseq)]
    for s, h in chains:
        rows, cols = slice(s * L, (s + 1) * L), slice(h * D, (h + 1) * D)
        a_inter, m_t, m_new, decay, w_col = stats[s, h]
        qh, kh, kb, ch, nh = state[s, h]
        den = jnp.sum(sm[s, h], axis=1, keepdims=True) + a_inter * jnp.sum(qh * nh, axis=1, keepdims=True)
        den = jnp.maximum(jnp.abs(den), jnp.exp(-m_t))
        hh = num[s, h] / den
        mu = jnp.mean(hh, axis=1, keepdims=True)
        var = jnp.mean(jnp.square(hh - mu), axis=1, keepdims=True)
        hn = (hh - mu) * lax.rsqrt(var + EPS) * gn_ref[:, cols]
        y_ref[rows, cols] = jax.nn.sigmoid(og_ref[rows, cols]) * hn
        c_ref[s, h] = c_new[s, h]
        n_ref[s, h:h + 1, :] = decay * nh + jnp.sum(kh * w_col, axis=0, keepdims=True)
        m_out[s] = jnp.where(lane == h, m_new, m_out[s])
    for s in range(nseq):
        m_ref[s] = m_out[s]


def mlstm_mix(zb, zs, if_bias, mnorm, c0, n0, m0, *, row0, batch, seq, col_q, y_init=None, out_rows=None):
    H, D = MLSTM_HEADS, MLSTM_HD
    dim = H * D
    L = math.gcd(seq, MLSTM_CHUNK)
    nc = seq // L
    nseq = math.gcd(batch, min(4, MLSTM_CHUNK // L)) if nc == 1 else 1
    rows = nseq * L
    assert L % 8 == 0 and row0 % rows == 0 and col_q % dim == 0
    r0 = row0 // rows
    cq = col_q // dim
    bias = jnp.pad(if_bias, (0, LANES - if_bias.shape[0])).reshape(1, LANES)
    m0p = jnp.pad(m0, ((0, 0), (0, LANES - H))).reshape(batch, 1, LANES)
    row = lambda b, c: r0 + b * nc + c
    y_rows = (out_rows or batch * seq) if y_init is None else y_init.shape[0]
    y_r0 = 0 if y_init is None else r0
    in_specs = [
        pl.BlockSpec((rows, dim), lambda b, c: (row(b, c), cq)),
        pl.BlockSpec((rows, dim), lambda b, c: (row(b, c), cq + 1)),
        pl.BlockSpec((rows, dim), lambda b, c: (row(b, c), cq + 2)),
        pl.BlockSpec((rows, dim), lambda b, c: (row(b, c), cq + 3)),
        pl.BlockSpec((rows, LANES), lambda b, c: (row(b, c), 0)),
        pl.BlockSpec((1, LANES), lambda b, c: (0, 0)),
        pl.BlockSpec((1, dim), lambda b, c: (0, 0)),
        pl.BlockSpec((nseq, H, D, D), lambda b, c: (b, 0, 0, 0)),
        pl.BlockSpec((nseq, H, D), lambda b, c: (b, 0, 0)),
        pl.BlockSpec((nseq, 1, LANES), lambda b, c: (b, 0, 0)),
    ]
    args = [zb, zb, zb, zb, zs, bias, mnorm.reshape(1, dim), c0, n0, m0p]
    aliases = {}
    body = functools.partial(_mlstm_body, L=L)
    if y_init is not None:
        in_specs.append(pl.BlockSpec(memory_space=pl.ANY))
        args.append(y_init)
        aliases = {len(args) - 1: 0}
        body = functools.partial(_drop_alias_ref, body, len(args) - 1)
    y, c_out, n_out, m_out = pl.pallas_call(
        body,
        out_shape=(jax.ShapeDtypeStruct((y_rows, dim), F32),
                   jax.ShapeDtypeStruct((batch, H, D, D), F32),
                   jax.ShapeDtypeStruct((batch, H, D), F32),
                   jax.ShapeDtypeStruct((batch, 1, LANES), F32)),
        grid=(batch // nseq, nc),
        in_specs=in_specs,
        out_specs=(
            pl.BlockSpec((rows, dim), lambda b, c: (y_r0 + b * nc + c, 0)),
            pl.BlockSpec((nseq, H, D, D), lambda b, c: (b, 0, 0, 0)),
            pl.BlockSpec((nseq, H, D), lambda b, c: (b, 0, 0)),
            pl.BlockSpec((nseq, 1, LANES), lambda b, c: (b, 0, 0)),
        ),
        input_output_aliases=aliases,
        compiler_params=_params("parallel", "arbitrary"),
        name="mlstm_mix",
    )(*args)
    return y, c_out, n_out, m_out[:, 0, :H]


KEY_TILE = 128
KEY_CHUNK = 256
SEL_COLS = 64
POS_HI, POS_LO = SEL_COLS, SEL_COLS + 1
NEG_SEL = -1.0e9


def _slope(h):
    return 2.0 ** (-(8.0 / NSA_HEADS) * (h + 1))


def _key_features(pos, onehot):
    lane = lax.broadcasted_iota(jnp.int32, pos.shape, 1)
    hi = lax.shift_right_logical(pos, 6)
    lo = jnp.bitwise_and(pos, SEL_BLOCK - 1)
    f = jnp.where(lane == POS_HI, hi.astype(F32), jnp.where(lane == POS_LO, lo.astype(F32), 0.0))
    if onehot:
        f = jnp.where(lane == hi, 1.0, f)
    return f


def _query_features(shape, h):
    lane = lax.broadcasted_iota(jnp.int32, shape, 1)
    return jnp.where(lane == POS_HI, SEL_BLOCK * _slope(h), jnp.where(lane == POS_LO, _slope(h), 0.0))


def _compress_block_rows(load, pw_ref, kv, g):
    cols = slice(g * NSA_HD, (g + 1) * NSA_HD)
    a0 = a1 = None
    for j in range(CMP_STRIDE):
        rows = load(j)
        t0 = rows * pw_ref[kv, j:j + 1, cols]
        t1 = rows * pw_ref[kv, CMP_STRIDE + j:CMP_STRIDE + j + 1, cols]
        a0 = t0 if a0 is None else a0 + t0
        a1 = t1 if a1 is None else a1 + t1
    return a0, a1


def _finish_compress(acc, proj_ref, kcmp_aug, vcmp, kv, g, n_ch, ncp, v_transposed):
    c = jnp.dot(acc.astype(BF16), proj_ref[kv, g].astype(BF16), preferred_element_type=F32)
    if kv == 0:
        kcmp_aug[g, 0:n_ch, 0:NSA_HD] = c.astype(BF16)
        n = lax.broadcasted_iota(jnp.int32, (ncp, LANES), 0)
        kcmp_aug[g, :, NSA_HD:2 * NSA_HD] = _key_features(n * CMP_STRIDE + (CMP_LEN - 1), False).astype(BF16)
    elif v_transposed:
        if ncp > n_ch:
            c = jnp.concatenate([c, jnp.zeros((ncp - n_ch, NSA_HD), F32)], axis=0)
        for blk in range(ncp // LANES):
            vcmp[g, :, blk * LANES:(blk + 1) * LANES] = c[blk * LANES:(blk + 1) * LANES].T.astype(BF16)
    else:
        vcmp[g, 0:n_ch, :] = c.astype(BF16)


def _masked_softmax(s, mask):
    s = jnp.where(mask, s, NEG_MASK)
    m = jnp.max(s, axis=1, keepdims=True)
    e = jnp.where(mask, jnp.exp(s - m), 0.0)
    return e / jnp.maximum(jnp.sum(e, axis=1, keepdims=True), 1e-30)


def _select_blocks(psum, t0, n_cmp, n_sel, queries_on_lanes):
    ncp = psum.shape[0] if queries_on_lanes else psum.shape[1]
    nsp = -(-n_sel // 8) * 8
    j = lax.broadcasted_iota(jnp.int32, (nsp, ncp), 0)
    n = lax.broadcasted_iota(jnp.int32, (nsp, ncp), 1)
    cover = ((n * CMP_STRIDE < j * SEL_BLOCK + SEL_BLOCK) & (n * CMP_STRIDE + CMP_LEN > j * SEL_BLOCK)
             & (n < n_cmp)).astype(F32)
    if queries_on_lanes:
        imp = jnp.dot(cover, psum, precision=lax.Precision.HIGHEST, preferred_element_type=F32)
    else:
        imp = lax.dot_general(cover, psum, NT, precision=lax.Precision.HIGHEST, preferred_element_type=F32)
    jq = lax.broadcasted_iota(jnp.int32, (nsp, LANES), 0)
    t = t0 + lax.broadcasted_iota(jnp.int32, (nsp, LANES), 1)
    cur = lax.shift_right_logical(t, 6)
    forced = (jq == 0) | (jq == cur) | (jq == cur - 1)
    valid = (jq * SEL_BLOCK <= t) & (jq < n_sel)
    score = jnp.where(valid, imp + jnp.where(forced, FORCE_BONUS, 0.0), -jnp.inf)
    jf = jq.astype(F32)
    sel = jnp.zeros((nsp, LANES), F32)
    for _ in range(min(SEL_TOPN, n_sel)):
        mx = jnp.max(score, axis=0, keepdims=True)
        first = jnp.min(jnp.where(score == mx, jf, 1.0e9), axis=0, keepdims=True)
        pick = jf == first
        sel = jnp.where(pick, 1.0, sel)
        score = jnp.where(pick, -jnp.inf, score)
    bias = jnp.where((sel > 0.5) | (jq >= n_sel), 0.0, NEG_SEL)
    bias = jnp.concatenate([bias, jnp.zeros((LANES - nsp, LANES), F32)], axis=0)
    return bias.T


def _online_step_t(state, s_t, v_t):
    m, l, acc = state
    m_new = jnp.maximum(m, jnp.max(s_t, axis=0, keepdims=True))
    alpha = jnp.exp(m - m_new)
    p = jnp.exp(s_t - m_new)
    l = alpha * l + jnp.sum(p, axis=0, keepdims=True)
    acc = alpha * acc + jnp.dot(v_t, p.astype(BF16), preferred_element_type=F32)
    return m_new, l, acc


def _stack_heads(q_heads, feats):
    return jnp.concatenate(
        [jnp.concatenate([q, f.astype(BF16)], axis=1) for q, f in zip(q_heads, feats)], axis=0)


def _write_gated(o_ref, gates, g, rows, o_cmp, o_s, o_w):
    for r in range(NSA_GROUP):
        h = g * NSA_GROUP + r
        sl = slice(r * rows, (r + 1) * rows)
        c = 8 + h
        o = gates[:, c:c + 1] * o_cmp[sl] + gates[:, c + 8:c + 9] * o_s[sl] + gates[:, c + 16:c + 17] * o_w[sl]
        o_ref[:, h * NSA_HD:(h + 1) * NSA_HD] = o


def _nsa_prompt_body(q_ref, kc_ref, ks_ref, kw_ref, gate_ref, pw_ref, proj_ref, o_ref,
                     ks_aug, vs_t, kw_aug, vw_t, kcmp_aug, vcmp_t, a1_scr, *, seq):
    i = pl.program_id(1)
    tq = KEY_TILE
    n_ch = seq // CMP_STRIDE
    n_cmp = n_ch - 1
    n_sel = seq // SEL_BLOCK
    ncp = kcmp_aug.shape[1]
    G, HD = NSA_GROUP, NSA_HD
    Q = G * tq

    @pl.when(i == 0)
    def _build():
        pos = lax.broadcasted_iota(jnp.int32, (seq, LANES), 0)
        f_sel = _key_features(pos, True).astype(BF16)
        f_win = _key_features(pos, False).astype(BF16)
        for g in range(NSA_KV_HEADS):
            ks_aug[g, :, 0:HD] = ks_ref[:, g * HD:(g + 1) * HD].astype(BF16)
            ks_aug[g, :, HD:2 * HD] = f_sel
            kw_aug[g, :, 0:HD] = kw_ref[:, g * HD:(g + 1) * HD].astype(BF16)
            kw_aug[g, :, HD:2 * HD] = f_win
            for kt in range(seq // KEY_TILE):
                rows = slice(kt * KEY_TILE, (kt + 1) * KEY_TILE)
                vs_t[g, :, rows] = ks_ref[rows, (2 + g) * HD:(3 + g) * HD].T.astype(BF16)
                vw_t[g, :, rows] = kw_ref[rows, (2 + g) * HD:(3 + g) * HD].T.astype(BF16)
        kcmp_aug[...] = jnp.zeros(kcmp_aug.shape, BF16)
        a1_scr[n_ch:n_ch + 8, :] = jnp.zeros((8, HD), F32)
        for kv in range(2):
            for g in range(NSA_KV_HEADS):
                c0 = (kv * NSA_KV_HEADS + g) * HD
                a0, a1 = _compress_block_rows(
                    lambda j: kc_ref[:, j * 4 * HD + c0:j * 4 * HD + c0 + HD], pw_ref, kv, g)
                a1_scr[0:n_ch, :] = a1
                _finish_compress(a0 + a1_scr[pl.ds(1, n_ch), :], proj_ref, kcmp_aug, vcmp_t, kv, g, n_ch, ncp, True)

    t0 = i * tq
    key = lax.broadcasted_iota(jnp.int32, (KEY_CHUNK, Q), 0)
    t_cols = t0 + jnp.bitwise_and(lax.broadcasted_iota(jnp.int32, (KEY_CHUNK, Q), 1), tq - 1)
    gates_t = jax.nn.sigmoid(gate_ref[...]).T
    scale = HD ** -0.5

    n = lax.broadcasted_iota(jnp.int32, (ncp, Q), 0)
    t_c = t0 + jnp.bitwise_and(lax.broadcasted_iota(jnp.int32, (ncp, Q), 1), tq - 1)
    mask = (n * CMP_STRIDE + (CMP_LEN - 1) <= t_c) & (n < n_cmp)
    kv_groups = range(NSA_KV_HEADS)
    q_heads = [[(q_ref[:, (g * G + r) * HD:(g * G + r + 1) * HD] * scale).astype(BF16) for r in range(G)]
               for g in kv_groups]
    feats = [[_query_features((tq, LANES), g * G + r) for r in range(G)] for g in kv_groups]
    q_plain = [_stack_heads(q_heads[g], feats[g]) for g in kv_groups]
    s_c = [jnp.where(mask, lax.dot_general(kcmp_aug[g], q_plain[g], NT, preferred_element_type=F32), NEG_MASK)
           for g in kv_groups]
    e_c = [jnp.where(mask, jnp.exp(s - jnp.max(s, axis=0, keepdims=True)), 0.0) for s in s_c]
    p_c = [e / jnp.maximum(jnp.sum(e, axis=0, keepdims=True), 1e-30) for e in e_c]
    o_cmp = [jnp.dot(vcmp_t[g], p_c[g].astype(BF16), preferred_element_type=F32) for g in kv_groups]
    psum = [p[:, 0:tq] + p[:, tq:2 * tq] + p[:, 2 * tq:3 * tq] + p[:, 3 * tq:4 * tq] for p in p_c]
    bias = [_select_blocks(psum[g], t0, n_cmp, n_sel, True) for g in kv_groups]
    q_sel = [_stack_heads(q_heads[g], [f + bias[g] for f in feats[g]]) for g in kv_groups]

    last = t0 // KEY_CHUNK
    first_w = jnp.maximum(t0 - WINDOW, 0) // KEY_CHUNK
    init = (jnp.full((1, Q), NEG_MASK, F32), jnp.zeros((1, Q), F32), jnp.zeros((HD, Q), F32))

    def scores(c, k_aug, q, g, valid):
        off = pl.multiple_of(c * KEY_CHUNK, KEY_CHUNK)
        s_t = lax.dot_general(k_aug[g, pl.ds(off, KEY_CHUNK), :], q, NT, preferred_element_type=F32)
        return s_t if valid is None else jnp.where(valid(off + key), s_t, NEG_MASK)

    def values(c, v_t, g):
        return v_t[g, :, pl.ds(pl.multiple_of(c * KEY_CHUNK, KEY_CHUNK), KEY_CHUNK)]

    def causal(kpos):
        return kpos <= t_cols

    def band(kpos):
        return (kpos <= t_cols) & (t_cols - kpos <= WINDOW)

    def early(c, sel):
        s = [scores(c, ks_aug, q_sel[g], g, None) for g in kv_groups]
        return tuple(_online_step_t(sel[g], s[g], values(c, vs_t, g)) for g in kv_groups)

    def late(c, sts):
        sel, win = sts
        s_sel = [scores(c, ks_aug, q_sel[g], g, causal) for g in kv_groups]
        s_win = [scores(c, kw_aug, q_plain[g], g, band) for g in kv_groups]
        sel = tuple(_online_step_t(sel[g], s_sel[g], values(c, vs_t, g)) for g in kv_groups)
        win = tuple(_online_step_t(win[g], s_win[g], values(c, vw_t, g)) for g in kv_groups)
        return sel, win

    inits = (init,) * NSA_KV_HEADS
    sel = lax.fori_loop(0, first_w, early, inits)
    sel, win = lax.fori_loop(first_w, last + 1, late, (sel, inits))

    for g in kv_groups:
        o_s = sel[g][2] / sel[g][1]
        o_w = win[g][2] / win[g][1]
        for r in range(G):
            h = g * G + r
            cols = slice(r * tq, (r + 1) * tq)
            c = 8 + h
            o_t = (gates_t[c:c + 1, :] * o_cmp[g][:, cols] + gates_t[c + 8:c + 9, :] * o_s[:, cols]
                   + gates_t[c + 16:c + 17, :] * o_w[:, cols])
            o_ref[:, h * HD:(h + 1) * HD] = o_t.T


def nsa_prompt(zb, zs, kv_cmp, cmp_pos_w, cmp_proj, *, batch, seq, col_q, col_slc, col_win, out_rows=None):
    assert seq % KEY_CHUNK == 0 and seq // SEL_BLOCK <= SEL_COLS
    nq = seq // KEY_TILE
    n_ch = seq // CMP_STRIDE
    ncp = -(-n_ch // LANES) * LANES
    qw, kvw = NSA_HEADS * NSA_HD, 4 * NSA_HD
    pw = cmp_pos_w.reshape(2, CMP_LEN, 2 * NSA_HD)
    return pl.pallas_call(
        functools.partial(_nsa_prompt_body, seq=seq),
        out_shape=jax.ShapeDtypeStruct((out_rows or batch * seq, qw), F32),
        grid=(batch, nq),
        in_specs=[
            pl.BlockSpec((KEY_TILE, qw), lambda b, i: (b * nq + i, col_q // qw)),
            pl.BlockSpec((n_ch, CMP_STRIDE * kvw), lambda b, i: (b, 0)),
            pl.BlockSpec((seq, kvw), lambda b, i: (b, col_slc // kvw)),
            pl.BlockSpec((seq, kvw), lambda b, i: (b, col_win // kvw)),
            pl.BlockSpec((KEY_TILE, LANES), lambda b, i: (b * nq + i, 0)),
            pl.BlockSpec((2, CMP_LEN, 2 * NSA_HD), lambda b, i: (0, 0, 0)),
            pl.BlockSpec((2, NSA_KV_HEADS, NSA_HD, NSA_HD), lambda b, i: (0, 0, 0, 0)),
        ],
        out_specs=pl.BlockSpec((KEY_TILE, qw), lambda b, i: (b * nq + i, 0)),
        scratch_shapes=[
            pltpu.VMEM((NSA_KV_HEADS, seq, 2 * NSA_HD), BF16),
            pltpu.VMEM((NSA_KV_HEADS, NSA_HD, seq), BF16),
            pltpu.VMEM((NSA_KV_HEADS, seq, 2 * NSA_HD), BF16),
            pltpu.VMEM((NSA_KV_HEADS, NSA_HD, seq), BF16),
            pltpu.VMEM((NSA_KV_HEADS, ncp, 2 * NSA_HD), BF16),
            pltpu.VMEM((NSA_KV_HEADS, NSA_HD, ncp), BF16),
            pltpu.VMEM((n_ch + 8, NSA_HD), F32),
        ],
        compiler_params=_params("parallel", "arbitrary"),
        name="nsa_prompt",
    )(zb, kv_cmp.reshape(-1, CMP_STRIDE * kvw), zb, zb, zs, pw, cmp_proj)


def _nsa_sample_body(pt_ref, q_ref, ksn_ref, kwn_ref, gate_ref, wprev_ref, pw_ref, proj_ref, *rest,
                     ts, past, n_pages):
    del pt_ref
    cmp_pages = rest[:n_pages]
    slc_pages = rest[n_pages:2 * n_pages]
    o_ref = rest[2 * n_pages + 1]
    ks_aug, vs, kw_aug, vw, kcmp_aug, vcmp, acc_scr = rest[2 * n_pages + 2:]
    G, HD = NSA_GROUP, NSA_HD
    page = slc_pages[0].shape[0] // 4
    kp = ks_aug.shape[1]
    wprev = wprev_ref.shape[0] // 4
    wp = kw_aug.shape[1]
    win_pos0 = past - wprev
    n_ch = (past + ts) // CMP_STRIDE
    n_cmp = n_ch - 1
    n_sel = -(-(past + ts) // SEL_BLOCK)
    ncp = kcmp_aug.shape[1]
    ch_per_page = page // CMP_STRIDE

    @pl.when(pl.program_id(0) == 0)
    def _constants():
        pos = lax.broadcasted_iota(jnp.int32, (kp, LANES), 0)
        f_sel = _key_features(pos, True).astype(BF16)
        posw = win_pos0 + lax.broadcasted_iota(jnp.int32, (wp, LANES), 0)
        f_win = _key_features(posw, False).astype(BF16)
        for g in range(NSA_KV_HEADS):
            ks_aug[g, :, HD:2 * HD] = f_sel
            kw_aug[g, :, HD:2 * HD] = f_win
        kcmp_aug[...] = jnp.zeros(kcmp_aug.shape, BF16)
        vcmp[...] = jnp.zeros(vcmp.shape, BF16)

    def with_tail(new_rows):
        return jnp.concatenate([new_rows, jnp.zeros((KEY_TILE - ts, HD), F32)], axis=0).astype(BF16)

    for g in range(NSA_KV_HEADS):
        for p in range(n_pages):
            rows = slice(p * page, (p + 1) * page)
            ks_aug[g, rows, 0:HD] = slc_pages[p][pl.ds(g, page, stride=4), :].astype(BF16)
            vs[g, rows, :] = slc_pages[p][pl.ds(2 + g, page, stride=4), :].astype(BF16)
        ks_aug[g, past:past + KEY_TILE, 0:HD] = with_tail(ksn_ref[:, g * HD:(g + 1) * HD])
        vs[g, past:past + KEY_TILE, :] = with_tail(ksn_ref[:, (2 + g) * HD:(3 + g) * HD])
        kw_aug[g, 0:wprev, 0:HD] = wprev_ref[pl.ds(g, wprev, stride=4), :].astype(BF16)
        vw[g, 0:wprev, :] = wprev_ref[pl.ds(2 + g, wprev, stride=4), :].astype(BF16)
        kw_aug[g, wprev:wprev + KEY_TILE, 0:HD] = with_tail(kwn_ref[:, g * HD:(g + 1) * HD])
        vw[g, wprev:wprev + KEY_TILE, :] = with_tail(kwn_ref[:, (2 + g) * HD:(3 + g) * HD])

    for kv in range(2):
        for g in range(NSA_KV_HEADS):
            c = kv * NSA_KV_HEADS + g
            cols = slice(g * HD, (g + 1) * HD)
            w0 = jnp.concatenate([pw_ref[kv, 0:CMP_STRIDE, cols]] * ch_per_page, axis=0)
            w1 = jnp.concatenate([pw_ref[kv, CMP_STRIDE:CMP_LEN, cols]] * ch_per_page, axis=0)
            for p in range(n_pages):
                x = cmp_pages[p][pl.ds(c, page, stride=4), :]
                if p + 1 < n_pages:
                    nxt = cmp_pages[p + 1][pl.ds(c, CMP_STRIDE, stride=4), :] * w1[0:CMP_STRIDE]
                else:
                    nxt = jnp.zeros((CMP_STRIDE, HD), F32)
                z = x * w0 + jnp.concatenate([(x * w1)[CMP_STRIDE:], nxt], axis=0)
                acc_scr[c, p * ch_per_page:(p + 1) * ch_per_page, :] = jnp.sum(
                    z.reshape(ch_per_page, CMP_STRIDE, HD), axis=1)
            _finish_compress(acc_scr[c, 0:n_ch, :], proj_ref, kcmp_aug, vcmp, kv, g, n_ch, ncp, False)

    R = G * ts
    gates = jax.nn.sigmoid(gate_ref[...])
    scale = HD ** -0.5

    def t_of(shape):
        return past + jnp.bitwise_and(lax.broadcasted_iota(jnp.int32, shape, 0), ts - 1)

    def softmax_pv(s, v):
        m = jnp.max(s, axis=1, keepdims=True)
        e = jnp.exp(s - m)
        return jnp.dot(e.astype(BF16), v, preferred_element_type=F32) / jnp.sum(e, axis=1, keepdims=True)

    kv_groups = range(NSA_KV_HEADS)
    q_heads = [[(q_ref[:, (g * G + r) * HD:(g * G + r + 1) * HD] * scale).astype(BF16) for r in range(G)]
               for g in kv_groups]
    feats = [[_query_features((ts, LANES), g * G + r) for r in range(G)] for g in kv_groups]
    q_plain = [_stack_heads(q_heads[g], feats[g]) for g in kv_groups]

    idx = lax.broadcasted_iota(jnp.int32, (R, wp), 1)
    dist = t_of((R, wp)) - (win_pos0 + idx)
    win_ok = (idx < wprev + ts) & (dist >= 0) & (dist <= WINDOW)
    s_win = [jnp.where(win_ok, lax.dot_general(q_plain[g], kw_aug[g], NT, preferred_element_type=F32), NEG_MASK)
             for g in kv_groups]
    n = lax.broadcasted_iota(jnp.int32, (R, ncp), 1)
    cmp_ok = (n * CMP_STRIDE + (CMP_LEN - 1) <= t_of((R, ncp))) & (n < n_cmp)
    s_cmp = [lax.dot_general(q_plain[g], kcmp_aug[g], NT, preferred_element_type=F32) for g in kv_groups]
    p_c = [_masked_softmax(s, cmp_ok) for s in s_cmp]
    o_cmp = [jnp.dot(p_c[g].astype(BF16), vcmp[g], preferred_element_type=F32) for g in kv_groups]
    o_w = [softmax_pv(s_win[g], vw[g]) for g in kv_groups]
    psum = [jnp.concatenate([p[0:ts] + p[ts:2 * ts] + p[2 * ts:3 * ts] + p[3 * ts:4 * ts],
                             jnp.zeros((LANES - ts, ncp), F32)], axis=0) for p in p_c]
    bias = [_select_blocks(psum[g], past, n_cmp, n_sel, False)[0:ts] for g in kv_groups]
    q_sel = [_stack_heads(q_heads[g], [f + bias[g] for f in feats[g]]) for g in kv_groups]
    sel_ok = lax.broadcasted_iota(jnp.int32, (R, kp), 1) <= t_of((R, kp))
    s_sel = [jnp.where(sel_ok, lax.dot_general(q_sel[g], ks_aug[g], NT, preferred_element_type=F32), NEG_MASK)
             for g in kv_groups]
    o_s = [softmax_pv(s_sel[g], vs[g]) for g in kv_groups]
    for g in kv_groups:
        _write_gated(o_ref, gates, g, ts, o_cmp[g], o_s[g], o_w[g])


def nsa_sample(zb, zs, cache_cmp, cache_slc, state_win, page_table, cmp_pos_w, cmp_proj, y_init, *,
               layer, row0, batch, ts, col_q, col_slc, col_win):
    depth, n_pool, page = cache_cmp.shape[:3]
    n_pages = page_table.shape[1]
    past = n_pages * page
    wprev = state_win.shape[2]
    assert ts & (ts - 1) == 0 and ts <= KEY_TILE and row0 % ts == 0
    assert past % KEY_TILE == 0 and (past + ts) // CMP_STRIDE == past // CMP_STRIDE
    assert page % CMP_STRIDE == 0 and -(-(past + ts) // SEL_BLOCK) <= SEL_COLS and wprev % 16 == 0
    qw, kvw = NSA_HEADS * NSA_HD, 4 * NSA_HD
    n_ch = past // CMP_STRIDE
    ncp = -(-n_ch // LANES) * LANES
    r0 = row0 // ts
    pw = cmp_pos_w.reshape(2, CMP_LEN, 2 * NSA_HD)
    cmp_view = cache_cmp.reshape(depth, n_pool, page * 4, NSA_HD)
    slc_view = cache_slc.reshape(depth, n_pool, page * 4, NSA_HD)
    win_view = state_win.reshape(depth, batch, wprev * 4, NSA_HD)

    def page_map(p):
        return lambda b, pt: (layer, pt[b * n_pages + p], 0, 0)

    in_specs = [
        pl.BlockSpec((ts, qw), lambda b, pt: (r0 + b, col_q // qw)),
        pl.BlockSpec((ts, kvw), lambda b, pt: (r0 + b, col_slc // kvw)),
        pl.BlockSpec((ts, kvw), lambda b, pt: (r0 + b, col_win // kvw)),
        pl.BlockSpec((ts, LANES), lambda b, pt: (r0 + b, 0)),
        pl.BlockSpec((None, None, wprev * 4, NSA_HD), lambda b, pt: (layer, b, 0, 0)),
        pl.BlockSpec((2, CMP_LEN, 2 * NSA_HD), lambda b, pt: (0, 0, 0)),
        pl.BlockSpec((2, NSA_KV_HEADS, NSA_HD, NSA_HD), lambda b, pt: (0, 0, 0, 0)),
    ]
    in_specs += [pl.BlockSpec((None, None, page * 4, NSA_HD), page_map(p % n_pages)) for p in range(2 * n_pages)]
    in_specs.append(pl.BlockSpec(memory_space=pl.ANY))
    return pl.pallas_call(
        functools.partial(_nsa_sample_body, ts=ts, past=past, n_pages=n_pages),
        out_shape=jax.ShapeDtypeStruct(y_init.shape, F32),
        grid_spec=pltpu.PrefetchScalarGridSpec(
            num_scalar_prefetch=1,
            grid=(batch,),
            in_specs=in_specs,
            out_specs=pl.BlockSpec((ts, qw), lambda b, pt: (r0 + b, 0)),
            scratch_shapes=[
                pltpu.VMEM((NSA_KV_HEADS, past + KEY_TILE, 2 * NSA_HD), BF16),
                pltpu.VMEM((NSA_KV_HEADS, past + KEY_TILE, NSA_HD), BF16),
                pltpu.VMEM((NSA_KV_HEADS, wprev + KEY_TILE, 2 * NSA_HD), BF16),
                pltpu.VMEM((NSA_KV_HEADS, wprev + KEY_TILE, NSA_HD), BF16),
                pltpu.VMEM((NSA_KV_HEADS, ncp, 2 * NSA_HD), BF16),
                pltpu.VMEM((NSA_KV_HEADS, ncp, NSA_HD), BF16),
                pltpu.VMEM((2 * NSA_KV_HEADS, n_ch, NSA_HD), F32),
            ],
        ),
        input_output_aliases={8 + 2 * n_pages: 0},
        compiler_params=_params("arbitrary"),
        name="nsa_sample",
    )(page_table.reshape(-1), zb, zb, zb, zs, win_view, pw, cmp_proj,
      *([cmp_view] * n_pages), *([slc_view] * n_pages), y_init)


ZB_NQ, ZB_POOL, ZB_MQ, ZB_MK, ZB_MV, ZB_MO, ZB_CMP, ZB_SLC, ZB_WIN, ZB_END = (
    0, 1024, 1536, 2048, 2560, 3072, 3584, 4096, 4608, 5120)
W_POOL, W_MI, W_NQ, W_CMP, W_NG, W_END = 0, 2560, 2568, 3592, 5128, 5152


def _split_w_in(w_in_l):
    w = w_in_l.astype(BF16)
    big = jnp.concatenate([w[:, W_NQ:W_CMP], w[:, W_POOL:W_MI], w[:, W_CMP:W_NG]], axis=1)
    small = jnp.concatenate([w[:, W_MI:W_NQ], w[:, W_NG:W_END]], axis=1)
    return big, jnp.pad(small, ((0, 0), (0, LANES - small.shape[1])))


def kernel(x_prompt, x_sample, cache_kv_cmp, cache_kv_slc, state_kv_win, state_pool, state_mlstm_C, state_mlstm_n, state_mlstm_m, page_table, ffn1_norm, ffn1_w_gate, ffn1_w_up, ffn1_w_down, mix_norm, w_in, w_out, pool_w, pool_scale, mlstm_if_bias, mlstm_norm, nsa_cmp_pos_w, nsa_cmp_proj, ffn2_norm, ffn2_w_gate, ffn2_w_up, ffn2_w_down, final_norm):
    bp, tp, d = x_prompt.shape
    bs, ts, _ = x_sample.shape
    depth = w_in.shape[0]
    mp, ms = bp * tp, bs * ts
    m_all = mp + ms
    past_len = page_table.shape[1] * cache_kv_cmp.shape[2]
    pd = pool_scale.shape[1]
    kv_row = (2, NSA_KV_HEADS, NSA_HD)
    x = jnp.concatenate([x_prompt.reshape(mp, d), x_sample.reshape(ms, d)], axis=0)
    zeros = lambda *s: jnp.zeros(s, F32)
    ffn1 = [w.astype(BF16) for w in (ffn1_w_gate, ffn1_w_up, ffn1_w_down)]
    ffn2 = [w.astype(BF16) for w in (ffn2_w_gate, ffn2_w_up, ffn2_w_down)]
    w_out_b = w_out.astype(BF16)
    outs = [[] for _ in range(14)]
    win_new = []
    for l in range(depth):
        x = ffn_half_step(x, ffn1_norm[l], *ffn1, l)
        w_big, w_small = _split_w_in(w_in[l])
        zb = norm_project(x, mix_norm[l], w_big)
        zs = norm_project(x, mix_norm[l], w_small)
        kv_cmp = zb[:, ZB_CMP:ZB_SLC]
        kv_slc = zb[:, ZB_SLC:ZB_WIN]
        kv_win = zb[:, ZB_WIN:ZB_END]
        z_pool = zb[:, ZB_POOL:ZB_MQ]

        y_pool = pool_prompt(zb, pool_w[l], pool_scale[l], batch=bp, seq=tp, col=ZB_POOL, out_rows=m_all)
        y_m, p_c, p_n, p_m = mlstm_mix(
            zb, zs, mlstm_if_bias[l], mlstm_norm[l], zeros(bp, MLSTM_HEADS, MLSTM_HD, MLSTM_HD),
            zeros(bp, MLSTM_HEADS, MLSTM_HD), zeros(bp, MLSTM_HEADS), row0=0, batch=bp, seq=tp, col_q=ZB_MQ,
            out_rows=m_all)
        y_nsa = nsa_prompt(zb, zs, kv_cmp, nsa_cmp_pos_w[l], nsa_cmp_proj[l], batch=bp, seq=tp,
                           col_q=ZB_NQ, col_slc=ZB_SLC, col_win=ZB_WIN, out_rows=m_all)

        pool_full = jnp.concatenate([zeros(bs, POOL_HALO - POOL_BUF, pd), state_pool[l],
                                     z_pool[mp:].reshape(bs, ts, pd)], axis=1)
        y_pool = pool_sample(pool_full, pool_w[l], pool_scale[l], y_pool, pos0=past_len, row0=mp)
        y_m, s_c, s_n, s_m = mlstm_mix(
            zb, zs, mlstm_if_bias[l], mlstm_norm[l], state_mlstm_C[l], state_mlstm_n[l], state_mlstm_m[l],
            row0=mp, batch=bs, seq=ts, col_q=ZB_MQ, y_init=y_m)
        y_nsa = nsa_sample(zb, zs, cache_kv_cmp, cache_kv_slc, state_kv_win, page_table,
                           nsa_cmp_pos_w[l], nsa_cmp_proj[l], y_nsa, layer=l, row0=mp, batch=bs, ts=ts,
                           col_q=ZB_NQ, col_slc=ZB_SLC, col_win=ZB_WIN)

        x = out_project(x, y_pool, y_m, y_nsa, w_out_b, l)
        x = ffn_half_step(x, ffn2_norm[l], *ffn2, l, gf=final_norm if l == depth - 1 else None)

        wp_rows = min(WINDOW, tp)
        win_new.append(kv_win[mp:].reshape(bs, ts, *kv_row))
        pool_p = jnp.concatenate([zeros(bp, POOL_BUF, pd), z_pool[:mp].reshape(bp, tp, pd)], axis=1)
        layer_out = (
            kv_cmp[:mp].reshape(bp, tp, *kv_row), kv_slc[:mp].reshape(bp, tp, *kv_row),
            kv_win[:mp].reshape(bp, tp, *kv_row)[:, tp - wp_rows:], pool_p[:, -POOL_BUF:], p_c, p_n, p_m,
            kv_cmp[mp:].reshape(bs, ts, *kv_row), kv_slc[mp:].reshape(bs, ts, *kv_row),
            None, pool_full[:, -POOL_BUF:], s_c, s_n, s_m)
        for acc, a in zip(outs, layer_out):
            acc.append(a)
    y_prompt = x[:mp].reshape(bp, tp, d)
    y_sample = x[mp:].reshape(bs, ts, d)
    w_old = state_kv_win.shape[2]
    keep_old = min(WINDOW, w_old + ts) - ts
    assert keep_old >= 0
    s_kv_win = jnp.concatenate([state_kv_win[:, :, w_old - keep_old:], jnp.stack(win_new)], axis=2)
    stacked = [s_kv_win if a[0] is None else jnp.stack(a) for a in outs]
    return (y_prompt, y_sample, *stacked)
```

```python
import functools
import math

import jax
import jax.numpy as jnp
from jax import lax
from jax.experimental import pallas as pl
from jax.experimental.pallas import tpu as pltpu

F32 = jnp.float32
BF16 = jnp.bfloat16
EPS = 1e-6

VMEM_LIMIT_BYTES = 56 * 1024 * 1024
LANES = 128

POOL_WINDOWS = (2, 4, 8, 16)
POOL_BUF = 15
MLSTM_HEADS = 4
MLSTM_HD = 128
MLSTM_CHUNK = 64
NSA_HD = 128
NSA_HEADS = 8
NSA_KV_HEADS = 2
NSA_GROUP = 4
CMP_LEN = 32
CMP_STRIDE = 16
SEL_BLOCK = 64
SEL_TOPN = 16
WINDOW = 512
FORCE_BONUS = 1.0e4

NEG_MASK = -1.0e30
NT = (((1,), (1,)), ((), ()))
TN = (((0,), (0,)), ((), ()))


def _pick_tile(n, pref):
    t = pref
    while t > 8 and n % t:
        t //= 2
    assert n % t == 0, (n, pref)
    return t


def _params(*sem):
    return pltpu.CompilerParams(dimension_semantics=sem, vmem_limit_bytes=VMEM_LIMIT_BYTES)


def _drop_alias_ref(body, index, *refs):
    return body(*refs[:index], *refs[index + 1:])


def _rms_rows(x, g):
    ms = jnp.mean(x * x, axis=-1, keepdims=True)
    return x * lax.rsqrt(ms + EPS) * g


def _ffn_body(x_ref, g_ref, wg_ref, wu_ref, wd_ref, gf_ref, o_ref, n_scr, *, final_norm):
    f = pl.program_id(1)

    @pl.when(f == 0)
    def _():
        x = x_ref[...]
        n_scr[...] = _rms_rows(x, g_ref[...]).astype(BF16)
        o_ref[...] = x

    n = n_scr[...]
    hg = jnp.dot(n, wg_ref[...], preferred_element_type=F32)
    hu = jnp.dot(n, wu_ref[...], preferred_element_type=F32)
    h = (hg * jax.nn.sigmoid(hg) * hu).astype(BF16)
    o_ref[...] += 0.5 * jnp.dot(h, wd_ref[...], preferred_element_type=F32)

    if final_norm:
        @pl.when(f == pl.num_programs(1) - 1)
        def _():
            o_ref[...] = _rms_rows(o_ref[...], gf_ref[...])


def ffn_half_step(x, g, wg, wu, wd, layer, gf=None):
    m, d = x.shape
    fdim = wg.shape[2]
    tm = _pick_tile(m, 512)
    tf = _pick_tile(fdim, 512)
    final_norm = gf is not None
    if gf is None:
        gf = g
    return pl.pallas_call(
        functools.partial(_ffn_body, final_norm=final_norm),
        out_shape=jax.ShapeDtypeStruct((m, d), F32),
        grid=(m // tm, fdim // tf),
        in_specs=[
            pl.BlockSpec((tm, d), lambda i, f: (i, 0)),
            pl.BlockSpec((1, d), lambda i, f: (0, 0)),
            pl.BlockSpec((None, d, tf), lambda i, f: (layer, 0, f)),
            pl.BlockSpec((None, d, tf), lambda i, f: (layer, 0, f)),
            pl.BlockSpec((None, tf, d), lambda i, f: (layer, f, 0)),
            pl.BlockSpec((1, d), lambda i, f: (0, 0)),
        ],
        out_specs=pl.BlockSpec((tm, d), lambda i, f: (i, 0)),
        scratch_shapes=[pltpu.VMEM((tm, d), BF16)],
        compiler_params=_params("parallel", "arbitrary"),
        name="ffn_half_step",
    )(x, g.reshape(1, d), wg, wu, wd, gf.reshape(1, d))


KV_SLABS = 3
KV_ROW = 4


def _inproj_body(x_ref, g_ref, w_ref, ws_ref, *rest, n_main, n_prompt_tiles, n_aliased):
    zb_ref, zs_ref, *kv_refs, n_scr = rest[n_aliased:]
    i, j = pl.program_id(0), pl.program_id(1)
    tm = x_ref.shape[0]

    @pl.when(j == 0)
    def _():
        n_scr[...] = _rms_rows(x_ref[...], g_ref[...]).astype(BF16)
        zs_ref[...] = jnp.dot(n_scr[...], ws_ref[...], preferred_element_type=F32)

    z = jnp.dot(n_scr[...], w_ref[...], preferred_element_type=F32)

    @pl.when(j < n_main)
    def _():
        zb_ref[...] = z

    def scatter(ref):
        for c in range(KV_ROW):
            ref[pl.ds(c, tm, stride=KV_ROW), :] = z[:, c * LANES:(c + 1) * LANES]

    for k in range(KV_SLABS):
        pl.when((j == n_main + k) & (i < n_prompt_tiles))(functools.partial(scatter, kv_refs[k]))
        pl.when((j == n_main + k) & (i >= n_prompt_tiles))(functools.partial(scatter, kv_refs[KV_SLABS + k]))


def mix_project(x, g, w, w_small, layer, depth, mp, kv_bufs=None):
    m, d = x.shape
    n = w.shape[1]
    tn = KV_ROW * LANES
    ms = m - mp
    tm = _pick_tile(math.gcd(mp, ms), 512)
    n_main = n // tn - KV_SLABS
    npt = mp // tm
    out_shape = [jax.ShapeDtypeStruct((m, n_main * tn), F32), jax.ShapeDtypeStruct((m, LANES), F32)]
    out_shape += [jax.ShapeDtypeStruct((depth, mp * KV_ROW, LANES), F32)] * KV_SLABS
    out_shape += [jax.ShapeDtypeStruct((depth, ms * KV_ROW, LANES), F32)] * KV_SLABS
    out_specs = [pl.BlockSpec((tm, tn), lambda i, j: (i, jnp.minimum(j, n_main - 1))),
                 pl.BlockSpec((tm, LANES), lambda i, j: (i, 0))]
    out_specs += [pl.BlockSpec((None, tm * KV_ROW, LANES), lambda i, j: (layer, jnp.minimum(i, npt - 1), 0))] * KV_SLABS
    out_specs += [pl.BlockSpec((None, tm * KV_ROW, LANES), lambda i, j: (layer, jnp.maximum(i - npt, 0), 0))] * KV_SLABS
    in_specs = [
        pl.BlockSpec((tm, d), lambda i, j: (i, 0)),
        pl.BlockSpec((1, d), lambda i, j: (0, 0)),
        pl.BlockSpec((d, tn), lambda i, j: (0, j)),
        pl.BlockSpec((d, LANES), lambda i, j: (0, 0)),
    ]
    args = [x, g.reshape(1, d), w, w_small]
    aliases = {}
    if kv_bufs is not None:
        in_specs += [pl.BlockSpec(memory_space=pl.ANY)] * len(kv_bufs)
        aliases = {len(args) + k: 2 + k for k in range(len(kv_bufs))}
        args += list(kv_bufs)
    zb, zs, *bufs = pl.pallas_call(
        functools.partial(_inproj_body, n_main=n_main, n_prompt_tiles=npt, n_aliased=len(aliases)),
        out_shape=out_shape,
        grid=(m // tm, n // tn),
        in_specs=in_specs,
        out_specs=out_specs,
        scratch_shapes=[pltpu.VMEM((tm, d), BF16)],
        input_output_aliases=aliases,
        compiler_params=_params("arbitrary", "arbitrary"),
        name="mix_project",
    )(*args)
    return zb, zs, bufs


def _outproj_body(x_ref, ya_ref, yb_ref, yc_ref, wa_ref, wb_ref, wc_ref, o_ref):
    acc = x_ref[...]
    acc += jnp.dot(ya_ref[...].astype(BF16), wa_ref[...], preferred_element_type=F32)
    acc += jnp.dot(yb_ref[...].astype(BF16), wb_ref[...], preferred_element_type=F32)
    acc += jnp.dot(yc_ref[...].astype(BF16), wc_ref[...], preferred_element_type=F32)
    o_ref[...] = acc


def out_project(x, y_pool, y_mlstm, y_nsa, w_out, layer):
    m, d = x.shape
    da, db, dc = y_pool.shape[1], y_mlstm.shape[1], y_nsa.shape[1]
    assert da == db and dc % da == 0
    tm = _pick_tile(m, 512)
    tn = d
    return pl.pallas_call(
        _outproj_body,
        out_shape=jax.ShapeDtypeStruct((m, d), F32),
        grid=(m // tm, d // tn),
        in_specs=[
            pl.BlockSpec((tm, tn), lambda i, j: (i, j)),
            pl.BlockSpec((tm, da), lambda i, j: (i, 0)),
            pl.BlockSpec((tm, db), lambda i, j: (i, 0)),
            pl.BlockSpec((tm, dc), lambda i, j: (i, 0)),
            pl.BlockSpec((None, da, tn), lambda i, j: (layer, 0, j)),
            pl.BlockSpec((None, db, tn), lambda i, j: (layer, 1, j)),
            pl.BlockSpec((None, dc, tn), lambda i, j: (layer, (da + db) // dc, j)),
        ],
        out_specs=pl.BlockSpec((tm, tn), lambda i, j: (i, j)),
        compiler_params=_params("parallel", "arbitrary"),
        name="out_project",
    )(x, y_pool, y_mlstm, y_nsa, w_out, w_out, w_out)


POOL_HALO = 16


def _pool_group(load, g, n_avail, w_ref, sc_ref):
    w = POOL_WINDOWS[g]
    z = load(0)
    acc = z
    for j in range(1, w):
        acc = acc + load(j)
    d = acc / jnp.minimum(n_avail, w).astype(F32) - z
    lead = d.shape[:-1]
    gd = d.shape[-1]
    y = jnp.dot(d.reshape(-1, gd).astype(BF16), w_ref[g].astype(BF16), preferred_element_type=F32)
    return (y * sc_ref[:, g * gd:(g + 1) * gd]).reshape(*lead, gd)


def _pool_prompt_body(z_ref, w_ref, sc_ref, o_ref, full_scr, *, chunk):
    seq, pd = z_ref.shape
    gd = pd // len(POOL_WINDOWS)
    full_scr[0:POOL_HALO, :] = jnp.zeros((POOL_HALO, pd), F32)
    full_scr[POOL_HALO:POOL_HALO + seq, :] = z_ref[...]
    for c in range(seq // chunk):
        n_avail = c * chunk + 1 + lax.broadcasted_iota(jnp.int32, (chunk, gd), 0)
        for g in range(len(POOL_WINDOWS)):
            load = lambda j: full_scr[pl.ds(POOL_HALO + c * chunk - j, chunk), g * gd:(g + 1) * gd]
            o_ref[c * chunk:(c + 1) * chunk, g * gd:(g + 1) * gd] = _pool_group(load, g, n_avail, w_ref, sc_ref)


def pool_prompt(zb, pool_w, pool_scale, *, batch, seq, col, out_rows=None):
    pd = pool_scale.shape[0]
    chunk = _pick_tile(seq, 256)
    return pl.pallas_call(
        functools.partial(_pool_prompt_body, chunk=chunk),
        out_shape=jax.ShapeDtypeStruct((out_rows or batch * seq, pd), F32),
        grid=(batch,),
        in_specs=[
            pl.BlockSpec((seq, pd), lambda b: (b, col // pd)),
            pl.BlockSpec(pool_w.shape, lambda b: (0, 0, 0)),
            pl.BlockSpec((1, pd), lambda b: (0, 0)),
        ],
        out_specs=pl.BlockSpec((seq, pd), lambda b: (b, 0)),
        scratch_shapes=[pltpu.VMEM((POOL_HALO + seq, pd), F32)],
        compiler_params=_params("parallel"),
        name="pool_prompt",
    )(zb, pool_w, pool_scale.reshape(1, pd))


def _pool_sample_body(full_ref, w_ref, sc_ref, o_ref, *, pos0):
    bt, rows, pd = full_ref.shape
    ts = rows - POOL_HALO
    gd = pd // len(POOL_WINDOWS)
    n_avail = pos0 + 1 + lax.broadcasted_iota(jnp.int32, (bt, ts, gd), 1)
    for g in range(len(POOL_WINDOWS)):
        load = lambda j: full_ref[:, pl.ds(POOL_HALO - j, ts), g * gd:(g + 1) * gd]
        o_ref[:, g * gd:(g + 1) * gd] = _pool_group(load, g, n_avail, w_ref, sc_ref).reshape(bt * ts, gd)


def pool_sample(full, pool_w, pool_scale, y_init, *, pos0, row0):
    batch, rows, pd = full.shape
    ts = rows - POOL_HALO
    bt = math.gcd(batch, 32)
    assert ts % 8 == 0 and row0 % (bt * ts) == 0
    r0 = row0 // (bt * ts)
    return pl.pallas_call(
        functools.partial(_drop_alias_ref, functools.partial(_pool_sample_body, pos0=pos0), 3),
        out_shape=jax.ShapeDtypeStruct(y_init.shape, F32),
        grid=(batch // bt,),
        in_specs=[
            pl.BlockSpec((bt, rows, pd), lambda b: (b, 0, 0)),
            pl.BlockSpec(pool_w.shape, lambda b: (0, 0, 0)),
            pl.BlockSpec((1, pd), lambda b: (0, 0)),
            pl.BlockSpec(memory_space=pl.ANY),
        ],
        out_specs=pl.BlockSpec((bt * ts, pd), lambda b: (r0 + b, 0)),
        input_output_aliases={3: 0},
        compiler_params=_params("parallel"),
        name="pool_sample",
    )(full, pool_w, pool_scale.reshape(1, pd), y_init)


def _log_sigmoid(x):
    return jnp.minimum(x, 0.0) - jnp.log1p(jnp.exp(-jnp.abs(x)))


def _mlstm_body(q_ref, k_ref, v_ref, og_ref, g_ref, bias_ref, gn_ref, c0_ref, n0_ref, m0_ref,
                y_ref, c_ref, n_ref, m_ref, *, L):
    nseq = q_ref.shape[0] // L
    H, D = MLSTM_HEADS, MLSTM_HD
    hi = lax.Precision.HIGHEST

    @pl.when(pl.program_id(1) == 0)
    def _():
        c_ref[...] = c0_ref[...]
        n_ref[...] = n0_ref[...]
        m_ref[...] = m0_ref[...]

    sel = (lax.broadcasted_iota(jnp.int32, (8, LANES), 0) == lax.broadcasted_iota(jnp.int32, (8, LANES), 1)).astype(F32)
    li = lax.broadcasted_iota(jnp.int32, (L, L), 0)
    si = lax.broadcasted_iota(jnp.int32, (L, L), 1)
    causal = li >= si
    lane = lax.broadcasted_iota(jnp.int32, (1, LANES), 1)
    chains = [(s, h) for s in range(nseq) for h in range(H)]

    gates = []
    for s in range(nseq):
        rows = slice(s * L, (s + 1) * L)
        gz = g_ref[rows, :] + bias_ref[...]
        gz_rows = lax.dot_general(sel, gz, NT, precision=hi, preferred_element_type=F32)
        b_cols = jnp.dot(causal.astype(F32), _log_sigmoid(gz), precision=hi, preferred_element_type=F32)
        b_rows = jnp.dot(_log_sigmoid(gz_rows), (li <= si).astype(F32), precision=hi, preferred_element_type=F32)
        gates.append((gz, gz_rows, b_cols, b_rows, m_ref[s]))

    qk, qc, state = {}, {}, {}
    for s, h in chains:
        rows, cols = slice(s * L, (s + 1) * L), slice(h * D, (h + 1) * D)
        qh = q_ref[rows, cols]
        kh = k_ref[rows, cols] * (D ** -0.5)
        ch = c_ref[s, h]
        nh = n_ref[s, h:h + 1, :]
        qb, kb = qh.astype(BF16), kh.astype(BF16)
        qk[s, h] = lax.dot_general(qb, kb, NT, preferred_element_type=F32)
        qc[s, h] = lax.dot_general(qb, ch.astype(BF16), NT, preferred_element_type=F32)
        state[s, h] = (qh, kh, kb, ch, nh)

    sm, stats = {}, {}
    for s, h in chains:
        gz, gz_rows, b_cols, b_rows, m_all = gates[s]
        bc = b_cols[:, H + h:H + h + 1]
        ic = gz[:, h:h + 1]
        br = b_rows[H + h:H + h + 1, :]
        ir = gz_rows[h:h + 1, :]
        m_prev = m_all[:, h:h + 1]
        dmat = jnp.where(causal, bc - br + ir, NEG_MASK)
        inter = bc + m_prev
        m_t = jnp.maximum(inter, jnp.max(dmat, axis=1, keepdims=True))
        sm[s, h] = qk[s, h] * jnp.exp(dmat - m_t)
        m_new = m_t[L - 1:L, :]
        b_last = bc[L - 1:L, :]
        stats[s, h] = (jnp.exp(inter - m_t), m_t, m_new, jnp.exp(b_last + m_prev - m_new),
                       jnp.exp(b_last - bc + ic - m_new))

    num, c_new = {}, {}
    for s, h in chains:
        rows, cols = slice(s * L, (s + 1) * L), slice(h * D, (h + 1) * D)
        a_inter, _, _, decay, w_col = stats[s, h]
        qh, kh, kb, ch, nh = state[s, h]
        vh = v_ref[rows, cols]
        num[s, h] = jnp.dot(sm[s, h].astype(BF16), vh.astype(BF16), preferred_element_type=F32) + a_inter * qc[s, h]
        c_new[s, h] = decay * ch + lax.dot_general((vh * w_col).astype(BF16), kb, TN, preferred_element_type=F32)

    m_out = [gates[s][4] for s in range(nseq)]
    for s, h in chains:
        rows, cols = slice(s * L, (s + 1) * L), slice(h * D, (h + 1) * D)
        a_inter, m_t, m_new, decay, w_col = stats[s, h]
        qh, kh, kb, ch, nh = state[s, h]
        den = jnp.sum(sm[s, h], axis=1, keepdims=True) + a_inter * jnp.sum(qh * nh, axis=1, keepdims=True)
        den = jnp.maximum(jnp.abs(den), jnp.exp(-m_t))
        hh = num[s, h] / den
        mu = jnp.mean(hh, axis=1, keepdims=True)
        var = jnp.mean(jnp.square(hh - mu), axis=1, keepdims=True)
        hn = (hh - mu) * lax.rsqrt(var + EPS) * gn_ref[:, cols]
        y_ref[rows, cols] = jax.nn.sigmoid(og_ref[rows, cols]) * hn
        c_ref[s, h] = c_new[s, h]
        n_ref[s, h:h + 1, :] = decay * nh + jnp.sum(kh * w_col, axis=0, keepdims=True)
        m_out[s] = jnp.where(lane == h, m_new, m_out[s])
    for s in range(nseq):
        m_ref[s] = m_out[s]


def mlstm_mix(zb, zs, if_bias, mnorm, c0, n0, m0, *, row0, batch, seq, col_q, y_init=None, out_rows=None):
    H, D = MLSTM_HEADS, MLSTM_HD
    dim = H * D
    L = math.gcd(seq, MLSTM_CHUNK)
    nc = seq // L
    nseq = math.gcd(batch, min(4, MLSTM_CHUNK // L)) if nc == 1 else 1
    rows = nseq * L
    assert L % 8 == 0 and row0 % rows == 0 and col_q % dim == 0
    r0 = row0 // rows
    cq = col_q // dim
    bias = jnp.pad(if_bias, (0, LANES - if_bias.shape[0])).reshape(1, LANES)
    m0p = jnp.pad(m0, ((0, 0), (0, LANES - H))).reshape(batch, 1, LANES)
    row = lambda b, c: r0 + b * nc + c
    y_rows = (out_rows or batch * seq) if y_init is None else y_init.shape[0]
    y_r0 = 0 if y_init is None else r0
    in_specs = [
        pl.BlockSpec((rows, dim), lambda b, c: (row(b, c), cq)),
        pl.BlockSpec((rows, dim), lambda b, c: (row(b, c), cq + 1)),
        pl.BlockSpec((rows, dim), lambda b, c: (row(b, c), cq + 2)),
        pl.BlockSpec((rows, dim), lambda b, c: (row(b, c), cq + 3)),
        pl.BlockSpec((rows, LANES), lambda b, c: (row(b, c), 0)),
        pl.BlockSpec((1, LANES), lambda b, c: (0, 0)),
        pl.BlockSpec((1, dim), lambda b, c: (0, 0)),
        pl.BlockSpec((nseq, H, D, D), lambda b, c: (b, 0, 0, 0)),
        pl.BlockSpec((nseq, H, D), lambda b, c: (b, 0, 0)),
        pl.BlockSpec((nseq, 1, LANES), lambda b, c: (b, 0, 0)),
    ]
    args = [zb, zb, zb, zb, zs, bias, mnorm.reshape(1, dim), c0, n0, m0p]
    aliases = {}
    body = functools.partial(_mlstm_body, L=L)
    if y_init is not None:
        in_specs.append(pl.BlockSpec(memory_space=pl.ANY))
        args.append(y_init)
        aliases = {len(args) - 1: 0}
        body = functools.partial(_drop_alias_ref, body, len(args) - 1)
    y, c_out, n_out, m_out = pl.pallas_call(
        body,
        out_shape=(jax.ShapeDtypeStruct((y_rows, dim), F32),
                   jax.ShapeDtypeStruct((batch, H, D, D), F32),
                   jax.ShapeDtypeStruct((batch, H, D), F32),
                   jax.ShapeDtypeStruct((batch, 1, LANES), F32)),
        grid=(batch // nseq, nc),
        in_specs=in_specs,
        out_specs=(
            pl.BlockSpec((rows, dim), lambda b, c: (y_r0 + b * nc + c, 0)),
            pl.BlockSpec((nseq, H, D, D), lambda b, c: (b, 0, 0, 0)),
            pl.BlockSpec((nseq, H, D), lambda b, c: (b, 0, 0)),
            pl.BlockSpec((nseq, 1, LANES), lambda b, c: (b, 0, 0)),
        ),
        input_output_aliases=aliases,
        compiler_params=_params("parallel", "arbitrary"),
        name="mlstm_mix",
    )(*args)
    return y, c_out, n_out, m_out[:, 0, :H]


KEY_TILE = 128
KEY_CHUNK = 256
SEL_COLS = 64
POS_HI, POS_LO = SEL_COLS, SEL_COLS + 1
NEG_SEL = -1.0e9


def _slope(h):
    return 2.0 ** (-(8.0 / NSA_HEADS) * (h + 1))


def _key_features(pos, onehot):
    lane = lax.broadcasted_iota(jnp.int32, pos.shape, 1)
    hi = lax.shift_right_logical(pos, 6)
    lo = jnp.bitwise_and(pos, SEL_BLOCK - 1)
    f = jnp.where(lane == POS_HI, hi.astype(F32), jnp.where(lane == POS_LO, lo.astype(F32), 0.0))
    if onehot:
        f = jnp.where(lane == hi, 1.0, f)
    return f


def _query_features(shape, h):
    lane = lax.broadcasted_iota(jnp.int32, shape, 1)
    return jnp.where(lane == POS_HI, SEL_BLOCK * _slope(h), jnp.where(lane == POS_LO, _slope(h), 0.0))


def _compress_block_rows(load, pw_ref, kv, g):
    cols = slice(g * NSA_HD, (g + 1) * NSA_HD)
    a0 = a1 = None
    for j in range(CMP_STRIDE):
        rows = load(j)
        t0 = rows * pw_ref[kv, j:j + 1, cols]
        t1 = rows * pw_ref[kv, CMP_STRIDE + j:CMP_STRIDE + j + 1, cols]
        a0 = t0 if a0 is None else a0 + t0
        a1 = t1 if a1 is None else a1 + t1
    return a0, a1


def _finish_compress(acc, proj_ref, kcmp_aug, vcmp, kv, g, n_ch, ncp, v_transposed):
    c = jnp.dot(acc.astype(BF16), proj_ref[kv, g].astype(BF16), preferred_element_type=F32)
    if kv == 0:
        kcmp_aug[g, 0:n_ch, 0:NSA_HD] = c.astype(BF16)
        n = lax.broadcasted_iota(jnp.int32, (ncp, LANES), 0)
        kcmp_aug[g, :, NSA_HD:2 * NSA_HD] = _key_features(n * CMP_STRIDE + (CMP_LEN - 1), False).astype(BF16)
    elif v_transposed:
        if ncp > n_ch:
            c = jnp.concatenate([c, jnp.zeros((ncp - n_ch, NSA_HD), F32)], axis=0)
        for blk in range(ncp // LANES):
            vcmp[g, :, blk * LANES:(blk + 1) * LANES] = c[blk * LANES:(blk + 1) * LANES].T.astype(BF16)
    else:
        vcmp[g, 0:n_ch, :] = c.astype(BF16)


def _masked_softmax(s, mask):
    s = jnp.where(mask, s, NEG_MASK)
    m = jnp.max(s, axis=1, keepdims=True)
    e = jnp.where(mask, jnp.exp(s - m), 0.0)
    return e / jnp.maximum(jnp.sum(e, axis=1, keepdims=True), 1e-30)


def _select_blocks(psum, t0, n_cmp, n_sel, queries_on_lanes):
    ncp = psum.shape[0] if queries_on_lanes else psum.shape[1]
    nsp = -(-n_sel // 8) * 8
    j = lax.broadcasted_iota(jnp.int32, (nsp, ncp), 0)
    n = lax.broadcasted_iota(jnp.int32, (nsp, ncp), 1)
    cover = ((n * CMP_STRIDE < j * SEL_BLOCK + SEL_BLOCK) & (n * CMP_STRIDE + CMP_LEN > j * SEL_BLOCK)
             & (n < n_cmp)).astype(F32)
    if queries_on_lanes:
        imp = jnp.dot(cover, psum, precision=lax.Precision.HIGHEST, preferred_element_type=F32)
    else:
        imp = lax.dot_general(cover, psum, NT, precision=lax.Precision.HIGHEST, preferred_element_type=F32)
    jq = lax.broadcasted_iota(jnp.int32, (nsp, LANES), 0)
    t = t0 + lax.broadcasted_iota(jnp.int32, (nsp, LANES), 1)
    cur = lax.shift_right_logical(t, 6)
    forced = (jq == 0) | (jq == cur) | (jq == cur - 1)
    valid = (jq * SEL_BLOCK <= t) & (jq < n_sel)
    score = jnp.where(valid, imp + jnp.where(forced, FORCE_BONUS, 0.0), -jnp.inf)
    jf = jq.astype(F32)
    sel = jnp.zeros((nsp, LANES), F32)
    for _ in range(min(SEL_TOPN, n_sel)):
        mx = jnp.max(score, axis=0, keepdims=True)
        first = jnp.min(jnp.where(score == mx, jf, 1.0e9), axis=0, keepdims=True)
        pick = jf == first
        sel = jnp.where(pick, 1.0, sel)
        score = jnp.where(pick, -jnp.inf, score)
    bias = jnp.where((sel > 0.5) | (jq >= n_sel), 0.0, NEG_SEL)
    bias = jnp.concatenate([bias, jnp.zeros((LANES - nsp, LANES), F32)], axis=0)
    return bias.T


def _online_step_t(state, s_t, v_t):
    m, l, acc = state
    m_new = jnp.maximum(m, jnp.max(s_t, axis=0, keepdims=True))
    alpha = jnp.exp(m - m_new)
    p = jnp.exp(s_t - m_new)
    l = alpha * l + jnp.sum(p, axis=0, keepdims=True)
    acc = alpha * acc + jnp.dot(v_t, p.astype(BF16), preferred_element_type=F32)
    return m_new, l, acc


def _stack_heads(q_heads, feats):
    return jnp.concatenate(
        [jnp.concatenate([q, f.astype(BF16)], axis=1) for q, f in zip(q_heads, feats)], axis=0)


def _write_gated(o_ref, gates, g, rows, o_cmp, o_s, o_w):
    for r in range(NSA_GROUP):
        h = g * NSA_GROUP + r
        sl = slice(r * rows, (r + 1) * rows)
        c = 8 + h
        o = gates[:, c:c + 1] * o_cmp[sl] + gates[:, c + 8:c + 9] * o_s[sl] + gates[:, c + 16:c + 17] * o_w[sl]
        o_ref[:, h * NSA_HD:(h + 1) * NSA_HD] = o


def _nsa_prompt_body(q_ref, kc_ref, ks_ref, kw_ref, gate_ref, pw_ref, proj_ref, o_ref,
                     ks_aug, vs_t, kw_aug, vw_t, kcmp_aug, vcmp_t, a1_scr, *, seq):
    i = pl.program_id(1)
    tq = KEY_TILE
    n_ch = seq // CMP_STRIDE
    n_cmp = n_ch - 1
    n_sel = seq // SEL_BLOCK
    ncp = kcmp_aug.shape[1]
    G, HD = NSA_GROUP, NSA_HD
    Q = G * tq

    @pl.when(i == 0)
    def _build():
        pos = lax.broadcasted_iota(jnp.int32, (seq, LANES), 0)
        f_sel = _key_features(pos, True).astype(BF16)
        f_win = _key_features(pos, False).astype(BF16)
        for g in range(NSA_KV_HEADS):
            ks_aug[g, :, 0:HD] = ks_ref[pl.ds(g, seq, stride=KV_ROW), :].astype(BF16)
            ks_aug[g, :, HD:2 * HD] = f_sel
            kw_aug[g, :, 0:HD] = kw_ref[pl.ds(g, seq, stride=KV_ROW), :].astype(BF16)
            kw_aug[g, :, HD:2 * HD] = f_win
            for kt in range(seq // KEY_TILE):
                rows = slice(kt * KEY_TILE, (kt + 1) * KEY_TILE)
                src_rows = pl.ds(kt * KEY_TILE * KV_ROW + 2 + g, KEY_TILE, stride=KV_ROW)
                vs_t[g, :, rows] = ks_ref[src_rows, :].T.astype(BF16)
                vw_t[g, :, rows] = kw_ref[src_rows, :].T.astype(BF16)
        kcmp_aug[...] = jnp.zeros(kcmp_aug.shape, BF16)
        a1_scr[n_ch:n_ch + 8, :] = jnp.zeros((8, HD), F32)
        for kv in range(2):
            for g in range(NSA_KV_HEADS):
                c = kv * NSA_KV_HEADS + g
                a0, a1 = _compress_block_rows(
                    lambda j: kc_ref[pl.ds(j * KV_ROW + c, n_ch, stride=KV_ROW * CMP_STRIDE), :], pw_ref, kv, g)
                a1_scr[0:n_ch, :] = a1
                _finish_compress(a0 + a1_scr[pl.ds(1, n_ch), :], proj_ref, kcmp_aug, vcmp_t, kv, g, n_ch, ncp, True)

    t0 = i * tq
    key = lax.broadcasted_iota(jnp.int32, (KEY_CHUNK, Q), 0)
    t_cols = t0 + jnp.bitwise_and(lax.broadcasted_iota(jnp.int32, (KEY_CHUNK, Q), 1), tq - 1)
    gates_t = jax.nn.sigmoid(gate_ref[...]).T
    scale = HD ** -0.5

    n = lax.broadcasted_iota(jnp.int32, (ncp, Q), 0)
    t_c = t0 + jnp.bitwise_and(lax.broadcasted_iota(jnp.int32, (ncp, Q), 1), tq - 1)
    mask = (n * CMP_STRIDE + (CMP_LEN - 1) <= t_c) & (n < n_cmp)
    kv_groups = range(NSA_KV_HEADS)
    q_heads = [[(q_ref[:, (g * G + r) * HD:(g * G + r + 1) * HD] * scale).astype(BF16) for r in range(G)]
               for g in kv_groups]
    feats = [[_query_features((tq, LANES), g * G + r) for r in range(G)] for g in kv_groups]
    q_plain = [_stack_heads(q_heads[g], feats[g]) for g in kv_groups]
    s_c = [jnp.where(mask, lax.dot_general(kcmp_aug[g], q_plain[g], NT, preferred_element_type=F32), NEG_MASK)
           for g in kv_groups]
    e_c = [jnp.where(mask, jnp.exp(s - jnp.max(s, axis=0, keepdims=True)), 0.0) for s in s_c]
    p_c = [e / jnp.maximum(jnp.sum(e, axis=0, keepdims=True), 1e-30) for e in e_c]
    o_cmp = [jnp.dot(vcmp_t[g], p_c[g].astype(BF16), preferred_element_type=F32) for g in kv_groups]
    psum = [p[:, 0:tq] + p[:, tq:2 * tq] + p[:, 2 * tq:3 * tq] + p[:, 3 * tq:4 * tq] for p in p_c]
    bias = [_select_blocks(psum[g], t0, n_cmp, n_sel, True) for g in kv_groups]
    q_sel = [_stack_heads(q_heads[g], [f + bias[g] for f in feats[g]]) for g in kv_groups]

    last = t0 // KEY_CHUNK
    first_w = jnp.maximum(t0 - WINDOW, 0) // KEY_CHUNK
    init = (jnp.full((1, Q), NEG_MASK, F32), jnp.zeros((1, Q), F32), jnp.zeros((HD, Q), F32))

    def scores(c, k_aug, q, g, valid):
        off = pl.multiple_of(c * KEY_CHUNK, KEY_CHUNK)
        s_t = lax.dot_general(k_aug[g, pl.ds(off, KEY_CHUNK), :], q, NT, preferred_element_type=F32)
        return s_t if valid is None else jnp.where(valid(off + key), s_t, NEG_MASK)

    def values(c, v_t, g):
        return v_t[g, :, pl.ds(pl.multiple_of(c * KEY_CHUNK, KEY_CHUNK), KEY_CHUNK)]

    def causal(kpos):
        return kpos <= t_cols

    def band(kpos):
        return (kpos <= t_cols) & (t_cols - kpos <= WINDOW)

    def early(c, sel):
        s = [scores(c, ks_aug, q_sel[g], g, None) for g in kv_groups]
        return tuple(_online_step_t(sel[g], s[g], values(c, vs_t, g)) for g in kv_groups)

    def late(c, sts):
        sel, win = sts
        s_sel = [scores(c, ks_aug, q_sel[g], g, causal) for g in kv_groups]
        s_win = [scores(c, kw_aug, q_plain[g], g, band) for g in kv_groups]
        sel = tuple(_online_step_t(sel[g], s_sel[g], values(c, vs_t, g)) for g in kv_groups)
        win = tuple(_online_step_t(win[g], s_win[g], values(c, vw_t, g)) for g in kv_groups)
        return sel, win

    inits = (init,) * NSA_KV_HEADS
    sel = lax.fori_loop(0, first_w, early, inits)
    sel, win = lax.fori_loop(first_w, last + 1, late, (sel, inits))

    for g in kv_groups:
        o_s = sel[g][2] / sel[g][1]
        o_w = win[g][2] / win[g][1]
        for r in range(G):
            h = g * G + r
            cols = slice(r * tq, (r + 1) * tq)
            c = 8 + h
            o_t = (gates_t[c:c + 1, :] * o_cmp[g][:, cols] + gates_t[c + 8:c + 9, :] * o_s[:, cols]
                   + gates_t[c + 16:c + 17, :] * o_w[:, cols])
            o_ref[:, h * HD:(h + 1) * HD] = o_t.T


def nsa_prompt(zb, zs, kv_cmp, kv_slc, kv_win, cmp_pos_w, cmp_proj, *, layer, batch, seq, col_q, out_rows=None):
    assert seq % KEY_CHUNK == 0 and seq // SEL_BLOCK <= SEL_COLS
    nq = seq // KEY_TILE
    n_ch = seq // CMP_STRIDE
    ncp = -(-n_ch // LANES) * LANES
    qw = NSA_HEADS * NSA_HD
    pw = cmp_pos_w.reshape(2, CMP_LEN, 2 * NSA_HD)
    return pl.pallas_call(
        functools.partial(_nsa_prompt_body, seq=seq),
        out_shape=jax.ShapeDtypeStruct((out_rows or batch * seq, qw), F32),
        grid=(batch, nq),
        in_specs=[
            pl.BlockSpec((KEY_TILE, qw), lambda b, i: (b * nq + i, col_q // qw)),
            pl.BlockSpec((None, seq * KV_ROW, NSA_HD), lambda b, i: (layer, b, 0)),
            pl.BlockSpec((None, seq * KV_ROW, NSA_HD), lambda b, i: (layer, b, 0)),
            pl.BlockSpec((None, seq * KV_ROW, NSA_HD), lambda b, i: (layer, b, 0)),
            pl.BlockSpec((KEY_TILE, LANES), lambda b, i: (b * nq + i, 0)),
            pl.BlockSpec((2, CMP_LEN, 2 * NSA_HD), lambda b, i: (0, 0, 0)),
            pl.BlockSpec((2, NSA_KV_HEADS, NSA_HD, NSA_HD), lambda b, i: (0, 0, 0, 0)),
        ],
        out_specs=pl.BlockSpec((KEY_TILE, qw), lambda b, i: (b * nq + i, 0)),
        scratch_shapes=[
            pltpu.VMEM((NSA_KV_HEADS, seq, 2 * NSA_HD), BF16),
            pltpu.VMEM((NSA_KV_HEADS, NSA_HD, seq), BF16),
            pltpu.VMEM((NSA_KV_HEADS, seq, 2 * NSA_HD), BF16),
            pltpu.VMEM((NSA_KV_HEADS, NSA_HD, seq), BF16),
            pltpu.VMEM((NSA_KV_HEADS, ncp, 2 * NSA_HD), BF16),
            pltpu.VMEM((NSA_KV_HEADS, NSA_HD, ncp), BF16),
            pltpu.VMEM((n_ch + 8, NSA_HD), F32),
        ],
        compiler_params=_params("parallel", "arbitrary"),
        name="nsa_prompt",
    )(zb, kv_cmp, kv_slc, kv_win, zs, pw, cmp_proj)


def _nsa_sample_body(pt_ref, q_ref, ksn_ref, kwn_ref, gate_ref, wprev_ref, pw_ref, proj_ref, *rest,
                     ts, past, n_pages):
    del pt_ref
    cmp_pages = rest[:n_pages]
    slc_pages = rest[n_pages:2 * n_pages]
    o_ref = rest[2 * n_pages + 1]
    ks_aug, vs, kw_aug, vw, kcmp_aug, vcmp, acc_scr = rest[2 * n_pages + 2:]
    G, HD = NSA_GROUP, NSA_HD
    page = slc_pages[0].shape[0] // 4
    kp = ks_aug.shape[1]
    wprev = wprev_ref.shape[0] // 4
    wp = kw_aug.shape[1]
    win_pos0 = past - wprev
    n_ch = (past + ts) // CMP_STRIDE
    n_cmp = n_ch - 1
    n_sel = -(-(past + ts) // SEL_BLOCK)
    ncp = kcmp_aug.shape[1]
    ch_per_page = page // CMP_STRIDE

    @pl.when(pl.program_id(0) == 0)
    def _constants():
        pos = lax.broadcasted_iota(jnp.int32, (kp, LANES), 0)
        f_sel = _key_features(pos, True).astype(BF16)
        posw = win_pos0 + lax.broadcasted_iota(jnp.int32, (wp, LANES), 0)
        f_win = _key_features(posw, False).astype(BF16)
        for g in range(NSA_KV_HEADS):
            ks_aug[g, :, HD:2 * HD] = f_sel
            kw_aug[g, :, HD:2 * HD] = f_win
        kcmp_aug[...] = jnp.zeros(kcmp_aug.shape, BF16)
        vcmp[...] = jnp.zeros(vcmp.shape, BF16)

    def with_tail(new_rows):
        return jnp.concatenate([new_rows, jnp.zeros((KEY_TILE - ts, HD), F32)], axis=0).astype(BF16)

    for g in range(NSA_KV_HEADS):
        for p in range(n_pages):
            rows = slice(p * page, (p + 1) * page)
            ks_aug[g, rows, 0:HD] = slc_pages[p][pl.ds(g, page, stride=4), :].astype(BF16)
            vs[g, rows, :] = slc_pages[p][pl.ds(2 + g, page, stride=4), :].astype(BF16)
        ks_aug[g, past:past + KEY_TILE, 0:HD] = with_tail(ksn_ref[pl.ds(g, ts, stride=KV_ROW), :])
        vs[g, past:past + KEY_TILE, :] = with_tail(ksn_ref[pl.ds(2 + g, ts, stride=KV_ROW), :])
        kw_aug[g, 0:wprev, 0:HD] = wprev_ref[pl.ds(g, wprev, stride=4), :].astype(BF16)
        vw[g, 0:wprev, :] = wprev_ref[pl.ds(2 + g, wprev, stride=4), :].astype(BF16)
        kw_aug[g, wprev:wprev + KEY_TILE, 0:HD] = with_tail(kwn_ref[pl.ds(g, ts, stride=KV_ROW), :])
        vw[g, wprev:wprev + KEY_TILE, :] = with_tail(kwn_ref[pl.ds(2 + g, ts, stride=KV_ROW), :])

    for kv in range(2):
        for g in range(NSA_KV_HEADS):
            c = kv * NSA_KV_HEADS + g
            cols = slice(g * HD, (g + 1) * HD)
            w0 = jnp.concatenate([pw_ref[kv, 0:CMP_STRIDE, cols]] * ch_per_page, axis=0)
            w1 = jnp.concatenate([pw_ref[kv, CMP_STRIDE:CMP_LEN, cols]] * ch_per_page, axis=0)
            for p in range(n_pages):
                x = cmp_pages[p][pl.ds(c, page, stride=4), :]
                if p + 1 < n_pages:
                    nxt = cmp_pages[p + 1][pl.ds(c, CMP_STRIDE, stride=4), :] * w1[0:CMP_STRIDE]
                else:
                    nxt = jnp.zeros((CMP_STRIDE, HD), F32)
                z = x * w0 + jnp.concatenate([(x * w1)[CMP_STRIDE:], nxt], axis=0)
                acc_scr[c, p * ch_per_page:(p + 1) * ch_per_page, :] = jnp.sum(
                    z.reshape(ch_per_page, CMP_STRIDE, HD), axis=1)
            _finish_compress(acc_scr[c, 0:n_ch, :], proj_ref, kcmp_aug, vcmp, kv, g, n_ch, ncp, False)

    R = G * ts
    gates = jax.nn.sigmoid(gate_ref[...])
    scale = HD ** -0.5

    def t_of(shape):
        return past + jnp.bitwise_and(lax.broadcasted_iota(jnp.int32, shape, 0), ts - 1)

    def softmax_pv(s, v):
        m = jnp.max(s, axis=1, keepdims=True)
        e = jnp.exp(s - m)
        return jnp.dot(e.astype(BF16), v, preferred_element_type=F32) / jnp.sum(e, axis=1, keepdims=True)

    kv_groups = range(NSA_KV_HEADS)
    q_heads = [[(q_ref[:, (g * G + r) * HD:(g * G + r + 1) * HD] * scale).astype(BF16) for r in range(G)]
               for g in kv_groups]
    feats = [[_query_features((ts, LANES), g * G + r) for r in range(G)] for g in kv_groups]
    q_plain = [_stack_heads(q_heads[g], feats[g]) for g in kv_groups]

    idx = lax.broadcasted_iota(jnp.int32, (R, wp), 1)
    dist = t_of((R, wp)) - (win_pos0 + idx)
    win_ok = (idx < wprev + ts) & (dist >= 0) & (dist <= WINDOW)
    s_win = [jnp.where(win_ok, lax.dot_general(q_plain[g], kw_aug[g], NT, preferred_element_type=F32), NEG_MASK)
             for g in kv_groups]
    n = lax.broadcasted_iota(jnp.int32, (R, ncp), 1)
    cmp_ok = (n * CMP_STRIDE + (CMP_LEN - 1) <= t_of((R, ncp))) & (n < n_cmp)
    s_cmp = [lax.dot_general(q_plain[g], kcmp_aug[g], NT, preferred_element_type=F32) for g in kv_groups]
    p_c = [_masked_softmax(s, cmp_ok) for s in s_cmp]
    o_cmp = [jnp.dot(p_c[g].astype(BF16), vcmp[g], preferred_element_type=F32) for g in kv_groups]
    o_w = [softmax_pv(s_win[g], vw[g]) for g in kv_groups]
    psum = [jnp.concatenate([p[0:ts] + p[ts:2 * ts] + p[2 * ts:3 * ts] + p[3 * ts:4 * ts],
                             jnp.zeros((LANES - ts, ncp), F32)], axis=0) for p in p_c]
    bias = [_select_blocks(psum[g], past, n_cmp, n_sel, False)[0:ts] for g in kv_groups]
    q_sel = [_stack_heads(q_heads[g], [f + bias[g] for f in feats[g]]) for g in kv_groups]
    sel_ok = lax.broadcasted_iota(jnp.int32, (R, kp), 1) <= t_of((R, kp))
    s_sel = [jnp.where(sel_ok, lax.dot_general(q_sel[g], ks_aug[g], NT, preferred_element_type=F32), NEG_MASK)
             for g in kv_groups]
    o_s = [softmax_pv(s_sel[g], vs[g]) for g in kv_groups]
    for g in kv_groups:
        _write_gated(o_ref, gates, g, ts, o_cmp[g], o_s[g], o_w[g])


def nsa_sample(zb, zs, kv_slc, kv_win, cache_cmp, cache_slc, state_win, page_table, cmp_pos_w, cmp_proj, y_init, *,
               layer, row0, batch, ts, col_q):
    depth, n_pool, page = cache_cmp.shape[:3]
    n_pages = page_table.shape[1]
    past = n_pages * page
    wprev = state_win.shape[2]
    assert ts & (ts - 1) == 0 and ts <= KEY_TILE and row0 % ts == 0
    assert past % KEY_TILE == 0 and (past + ts) // CMP_STRIDE == past // CMP_STRIDE
    assert page % CMP_STRIDE == 0 and -(-(past + ts) // SEL_BLOCK) <= SEL_COLS and wprev % 16 == 0
    qw = NSA_HEADS * NSA_HD
    n_ch = past // CMP_STRIDE
    ncp = -(-n_ch // LANES) * LANES
    r0 = row0 // ts
    pw = cmp_pos_w.reshape(2, CMP_LEN, 2 * NSA_HD)
    cmp_view = cache_cmp.reshape(depth, n_pool, page * 4, NSA_HD)
    slc_view = cache_slc.reshape(depth, n_pool, page * 4, NSA_HD)
    win_view = state_win.reshape(depth, batch, wprev * 4, NSA_HD)

    def page_map(p):
        return lambda b, pt: (layer, pt[b * n_pages + p], 0, 0)

    in_specs = [
        pl.BlockSpec((ts, qw), lambda b, pt: (r0 + b, col_q // qw)),
        pl.BlockSpec((None, ts * KV_ROW, NSA_HD), lambda b, pt: (layer, b, 0)),
        pl.BlockSpec((None, ts * KV_ROW, NSA_HD), lambda b, pt: (layer, b, 0)),
        pl.BlockSpec((ts, LANES), lambda b, pt: (r0 + b, 0)),
        pl.BlockSpec((None, None, wprev * 4, NSA_HD), lambda b, pt: (layer, b, 0, 0)),
        pl.BlockSpec((2, CMP_LEN, 2 * NSA_HD), lambda b, pt: (0, 0, 0)),
        pl.BlockSpec((2, NSA_KV_HEADS, NSA_HD, NSA_HD), lambda b, pt: (0, 0, 0, 0)),
    ]
    in_specs += [pl.BlockSpec((None, None, page * 4, NSA_HD), page_map(p % n_pages)) for p in range(2 * n_pages)]
    in_specs.append(pl.BlockSpec(memory_space=pl.ANY))
    return pl.pallas_call(
        functools.partial(_nsa_sample_body, ts=ts, past=past, n_pages=n_pages),
        out_shape=jax.ShapeDtypeStruct(y_init.shape, F32),
        grid_spec=pltpu.PrefetchScalarGridSpec(
            num_scalar_prefetch=1,
            grid=(batch,),
            in_specs=in_specs,
            out_specs=pl.BlockSpec((ts, qw), lambda b, pt: (r0 + b, 0)),
            scratch_shapes=[
                pltpu.VMEM((NSA_KV_HEADS, past + KEY_TILE, 2 * NSA_HD), BF16),
                pltpu.VMEM((NSA_KV_HEADS, past + KEY_TILE, NSA_HD), BF16),
                pltpu.VMEM((NSA_KV_HEADS, wprev + KEY_TILE, 2 * NSA_HD), BF16),
                pltpu.VMEM((NSA_KV_HEADS, wprev + KEY_TILE, NSA_HD), BF16),
                pltpu.VMEM((NSA_KV_HEADS, ncp, 2 * NSA_HD), BF16),
                pltpu.VMEM((NSA_KV_HEADS, ncp, NSA_HD), BF16),
                pltpu.VMEM((2 * NSA_KV_HEADS, n_ch, NSA_HD), F32),
            ],
        ),
        input_output_aliases={8 + 2 * n_pages: 0},
        compiler_params=_params("arbitrary"),
        name="nsa_sample",
    )(page_table.reshape(-1), zb, kv_slc, kv_win, zs, win_view, pw, cmp_proj,
      *([cmp_view] * n_pages), *([slc_view] * n_pages), y_init)


ZB_NQ, ZB_POOL, ZB_MQ, ZB_MK, ZB_MV, ZB_MO, ZB_END = (0, 1024, 1536, 2048, 2560, 3072, 3584)
W_POOL, W_MI, W_NQ, W_CMP, W_NG, W_END = 0, 2560, 2568, 3592, 5128, 5152


def _split_w_in(w_in_l):
    w = w_in_l.astype(BF16)
    big = jnp.concatenate([w[:, W_NQ:W_CMP], w[:, W_POOL:W_MI], w[:, W_CMP:W_NG]], axis=1)
    small = jnp.concatenate([w[:, W_MI:W_NQ], w[:, W_NG:W_END]], axis=1)
    return big, jnp.pad(small, ((0, 0), (0, LANES - small.shape[1])))


def kernel(x_prompt, x_sample, cache_kv_cmp, cache_kv_slc, state_kv_win, state_pool, state_mlstm_C, state_mlstm_n, state_mlstm_m, page_table, ffn1_norm, ffn1_w_gate, ffn1_w_up, ffn1_w_down, mix_norm, w_in, w_out, pool_w, pool_scale, mlstm_if_bias, mlstm_norm, nsa_cmp_pos_w, nsa_cmp_proj, ffn2_norm, ffn2_w_gate, ffn2_w_up, ffn2_w_down, final_norm):
    bp, tp, d = x_prompt.shape
    bs, ts, _ = x_sample.shape
    depth = w_in.shape[0]
    mp, ms = bp * tp, bs * ts
    m_all = mp + ms
    past_len = page_table.shape[1] * cache_kv_cmp.shape[2]
    pd = pool_scale.shape[1]
    kv_row = (2, NSA_KV_HEADS, NSA_HD)
    x = jnp.concatenate([x_prompt.reshape(mp, d), x_sample.reshape(ms, d)], axis=0)
    zeros = lambda *s: jnp.zeros(s, F32)
    ffn1 = [w.astype(BF16) for w in (ffn1_w_gate, ffn1_w_up, ffn1_w_down)]
    ffn2 = [w.astype(BF16) for w in (ffn2_w_gate, ffn2_w_up, ffn2_w_down)]
    w_out_b = w_out.astype(BF16)
    outs = [[] for _ in range(8)]
    kv_bufs = None
    for l in range(depth):
        x = ffn_half_step(x, ffn1_norm[l], *ffn1, l)
        w_big, w_small = _split_w_in(w_in[l])
        zb, zs, kv_bufs = mix_project(x, mix_norm[l], w_big, w_small, l, depth, mp, kv_bufs)
        cmp_p, slc_p, win_p, cmp_s, slc_s, win_s = kv_bufs
        z_pool = zb[:, ZB_POOL:ZB_MQ]

        y_pool = pool_prompt(zb, pool_w[l], pool_scale[l], batch=bp, seq=tp, col=ZB_POOL, out_rows=m_all)
        y_m, p_c, p_n, p_m = mlstm_mix(
            zb, zs, mlstm_if_bias[l], mlstm_norm[l], zeros(bp, MLSTM_HEADS, MLSTM_HD, MLSTM_HD),
            zeros(bp, MLSTM_HEADS, MLSTM_HD), zeros(bp, MLSTM_HEADS), row0=0, batch=bp, seq=tp, col_q=ZB_MQ,
            out_rows=m_all)
        y_nsa = nsa_prompt(zb, zs, cmp_p, slc_p, win_p, nsa_cmp_pos_w[l], nsa_cmp_proj[l], layer=l, batch=bp,
                           seq=tp, col_q=ZB_NQ, out_rows=m_all)

        pool_full = jnp.concatenate([zeros(bs, POOL_HALO - POOL_BUF, pd), state_pool[l],
                                     z_pool[mp:].reshape(bs, ts, pd)], axis=1)
        y_pool = pool_sample(pool_full, pool_w[l], pool_scale[l], y_pool, pos0=past_len, row0=mp)
        y_m, s_c, s_n, s_m = mlstm_mix(
            zb, zs, mlstm_if_bias[l], mlstm_norm[l], state_mlstm_C[l], state_mlstm_n[l], state_mlstm_m[l],
            row0=mp, batch=bs, seq=ts, col_q=ZB_MQ, y_init=y_m)
        y_nsa = nsa_sample(zb, zs, slc_s, win_s, cache_kv_cmp, cache_kv_slc, state_kv_win, page_table,
                           nsa_cmp_pos_w[l], nsa_cmp_proj[l], y_nsa, layer=l, row0=mp, batch=bs, ts=ts,
                           col_q=ZB_NQ)

        x = out_project(x, y_pool, y_m, y_nsa, w_out_b, l)
        x = ffn_half_step(x, ffn2_norm[l], *ffn2, l, gf=final_norm if l == depth - 1 else None)

        pool_p = jnp.concatenate([zeros(bp, POOL_BUF, pd), z_pool[:mp].reshape(bp, tp, pd)], axis=1)
        layer_out = (pool_p[:, -POOL_BUF:], p_c, p_n, p_m, pool_full[:, -POOL_BUF:], s_c, s_n, s_m)
        for acc, a in zip(outs, layer_out):
            acc.append(a)
    y_prompt = x[:mp].reshape(bp, tp, d)
    y_sample = x[mp:].reshape(bs, ts, d)
    p_pool, p_c, p_n, p_m, s_pool, s_c, s_n, s_m = [jnp.stack(a) for a in outs]
    p_kv = [b.reshape(depth, bp, tp, *kv_row) for b in (cmp_p, slc_p, win_p)]
    s_kv = [b.reshape(depth, bs, ts, *kv_row) for b in (cmp_s, slc_s, win_s)]
    w_old = state_kv_win.shape[2]
    keep_old = min(WINDOW, w_old + ts) - ts
    assert keep_old >= 0
    s_kv_win = jnp.concatenate([state_kv_win[:, :, w_old - keep_old:], s_kv[2]], axis=2)
    return (y_prompt, y_sample, p_kv[0], p_kv[1], p_kv[2][:, :, tp - min(WINDOW, tp):], p_pool, p_c, p_n, p_m,
            s_kv[0], s_kv[1], s_kv_win, s_pool, s_c, s_n, s_m)
```

```python
import functools
import math

import jax
import jax.numpy as jnp
from jax import lax
from jax.experimental import pallas as pl
from jax.experimental.pallas import tpu as pltpu

F32 = jnp.float32
BF16 = jnp.bfloat16
EPS = 1e-6

VMEM_LIMIT_BYTES = 56 * 1024 * 1024
LANES = 128

POOL_WINDOWS = (2, 4, 8, 16)
POOL_BUF = 15
MLSTM_HEADS = 4
MLSTM_HD = 128
MLSTM_CHUNK = 64
NSA_HD = 128
NSA_HEADS = 8
NSA_KV_HEADS = 2
NSA_GROUP = 4
CMP_LEN = 32
CMP_STRIDE = 16
SEL_BLOCK = 64
SEL_TOPN = 16
WINDOW = 512
FORCE_BONUS = 1.0e4

NEG_MASK = -1.0e30
NT = (((1,), (1,)), ((), ()))
TN = (((0,), (0,)), ((), ()))


def _pick_tile(n, pref):
    t = pref
    while t > 8 and n % t:
        t //= 2
    assert n % t == 0, (n, pref)
    return t


def _params(*sem):
    return pltpu.CompilerParams(dimension_semantics=sem, vmem_limit_bytes=VMEM_LIMIT_BYTES)


def _drop_alias_ref(body, index, *refs):
    return body(*refs[:index], *refs[index + 1:])


def _rms_rows(x, g):
    ms = jnp.mean(x * x, axis=-1, keepdims=True)
    return x * lax.rsqrt(ms + EPS) * g


def _ffn_body(x_ref, g_ref, wg_ref, wu_ref, wd_ref, gf_ref, o_ref, n_scr, *, final_norm):
    f = pl.program_id(1)

    @pl.when(f == 0)
    def _():
        x = x_ref[...]
        n_scr[...] = _rms_rows(x, g_ref[...]).astype(BF16)
        o_ref[...] = x

    n = n_scr[...]
    hg = jnp.dot(n, wg_ref[...], preferred_element_type=F32)
    hu = jnp.dot(n, wu_ref[...], preferred_element_type=F32)
    h = (hg * jax.nn.sigmoid(hg) * hu).astype(BF16)
    o_ref[...] += 0.5 * jnp.dot(h, wd_ref[...], preferred_element_type=F32)

    if final_norm:
        @pl.when(f == pl.num_programs(1) - 1)
        def _():
            o_ref[...] = _rms_rows(o_ref[...], gf_ref[...])


def ffn_half_step(x, g, wg, wu, wd, layer, gf=None):
    m, d = x.shape
    fdim = wg.shape[2]
    tm = _pick_tile(m, 512)
    tf = _pick_tile(fdim, 512)
    final_norm = gf is not None
    if gf is None:
        gf = g
    return pl.pallas_call(
        functools.partial(_ffn_body, final_norm=final_norm),
        out_shape=jax.ShapeDtypeStruct((m, d), F32),
        grid=(m // tm, fdim // tf),
        in_specs=[
            pl.BlockSpec((tm, d), lambda i, f: (i, 0)),
            pl.BlockSpec((1, d), lambda i, f: (0, 0)),
            pl.BlockSpec((None, d, tf), lambda i, f: (layer, 0, f)),
            pl.BlockSpec((None, d, tf), lambda i, f: (layer, 0, f)),
            pl.BlockSpec((None, tf, d), lambda i, f: (layer, f, 0)),
            pl.BlockSpec((1, d), lambda i, f: (0, 0)),
        ],
        out_specs=pl.BlockSpec((tm, d), lambda i, f: (i, 0)),
        scratch_shapes=[pltpu.VMEM((tm, d), BF16)],
        compiler_params=_params("parallel", "arbitrary"),
        name="ffn_half_step",
    )(x, g.reshape(1, d), wg, wu, wd, gf.reshape(1, d))


KV_SLABS = 3
KV_ROW = 4


def _inproj_body(x_ref, g_ref, w_ref, ws_ref, *rest, main_cols, n_prompt_tiles, n_aliased):
    zb_ref, zs_ref, *kv_refs, n_scr = rest[n_aliased:]
    i, j = pl.program_id(0), pl.program_id(1)
    tm, tn = x_ref.shape[0], w_ref.shape[1]
    slab_w = KV_ROW * LANES

    @pl.when(j == 0)
    def _():
        n_scr[...] = _rms_rows(x_ref[...], g_ref[...]).astype(BF16)
        zs_ref[...] = jnp.dot(n_scr[...], ws_ref[...], preferred_element_type=F32)

    z = jnp.dot(n_scr[...], w_ref[...], preferred_element_type=F32)

    @pl.when(j * tn < main_cols)
    def _():
        zb_ref[...] = z

    def scatter(ref, off):
        for c in range(KV_ROW):
            ref[pl.ds(c, tm, stride=KV_ROW), :] = z[:, off + c * LANES:off + (c + 1) * LANES]

    for k in range(KV_SLABS):
        jt, off = divmod(main_cols + k * slab_w, tn)
        pl.when((j == jt) & (i < n_prompt_tiles))(functools.partial(scatter, kv_refs[k], off))
        pl.when((j == jt) & (i >= n_prompt_tiles))(functools.partial(scatter, kv_refs[KV_SLABS + k], off))


def mix_project(x, g, w, w_small, layer, depth, mp, kv_bufs=None):
    m, d = x.shape
    n = w.shape[1]
    tn = _pick_tile(n, 1024)
    ms = m - mp
    tm = _pick_tile(math.gcd(mp, ms), 512)
    main_cols = n - KV_SLABS * KV_ROW * LANES
    n_main = -(-main_cols // tn)
    npt = mp // tm
    out_shape = [jax.ShapeDtypeStruct((m, n_main * tn), F32), jax.ShapeDtypeStruct((m, LANES), F32)]
    out_shape += [jax.ShapeDtypeStruct((depth, mp * KV_ROW, LANES), F32)] * KV_SLABS
    out_shape += [jax.ShapeDtypeStruct((depth, ms * KV_ROW, LANES), F32)] * KV_SLABS
    out_specs = [pl.BlockSpec((tm, tn), lambda i, j: (i, jnp.minimum(j, n_main - 1))),
                 pl.BlockSpec((tm, LANES), lambda i, j: (i, 0))]
    out_specs += [pl.BlockSpec((None, tm * KV_ROW, LANES), lambda i, j: (layer, jnp.minimum(i, npt - 1), 0))] * KV_SLABS
    out_specs += [pl.BlockSpec((None, tm * KV_ROW, LANES), lambda i, j: (layer, jnp.maximum(i - npt, 0), 0))] * KV_SLABS
    in_specs = [
        pl.BlockSpec((tm, d), lambda i, j: (i, 0)),
        pl.BlockSpec((1, d), lambda i, j: (0, 0)),
        pl.BlockSpec((d, tn), lambda i, j: (0, j)),
        pl.BlockSpec((d, LANES), lambda i, j: (0, 0)),
    ]
    args = [x, g.reshape(1, d), w, w_small]
    aliases = {}
    if kv_bufs is not None:
        in_specs += [pl.BlockSpec(memory_space=pl.ANY)] * len(kv_bufs)
        aliases = {len(args) + k: 2 + k for k in range(len(kv_bufs))}
        args += list(kv_bufs)
    zb, zs, *bufs = pl.pallas_call(
        functools.partial(_inproj_body, main_cols=main_cols, n_prompt_tiles=npt, n_aliased=len(aliases)),
        out_shape=out_shape,
        grid=(m // tm, n // tn),
        in_specs=in_specs,
        out_specs=out_specs,
        scratch_shapes=[pltpu.VMEM((tm, d), BF16)],
        input_output_aliases=aliases,
        compiler_params=_params("arbitrary", "arbitrary"),
        name="mix_project",
    )(*args)
    return zb, zs, bufs


def _outproj_body(x_ref, ya_ref, yb_ref, yc_ref, wa_ref, wb_ref, wc_ref, o_ref):
    acc = x_ref[...]
    acc += jnp.dot(ya_ref[...].astype(BF16), wa_ref[...], preferred_element_type=F32)
    acc += jnp.dot(yb_ref[...].astype(BF16), wb_ref[...], preferred_element_type=F32)
    acc += jnp.dot(yc_ref[...].astype(BF16), wc_ref[...], preferred_element_type=F32)
    o_ref[...] = acc


def out_project(x, y_pool, y_mlstm, y_nsa, w_out, layer):
    m, d = x.shape
    da, db, dc = y_pool.shape[1], y_mlstm.shape[1], y_nsa.shape[1]
    assert da == db and dc % da == 0
    tm = _pick_tile(m, 512)
    tn = d
    return pl.pallas_call(
        _outproj_body,
        out_shape=jax.ShapeDtypeStruct((m, d), F32),
        grid=(m // tm, d // tn),
        in_specs=[
            pl.BlockSpec((tm, tn), lambda i, j: (i, j)),
            pl.BlockSpec((tm, da), lambda i, j: (i, 0)),
            pl.BlockSpec((tm, db), lambda i, j: (i, 0)),
            pl.BlockSpec((tm, dc), lambda i, j: (i, 0)),
            pl.BlockSpec((None, da, tn), lambda i, j: (layer, 0, j)),
            pl.BlockSpec((None, db, tn), lambda i, j: (layer, 1, j)),
            pl.BlockSpec((None, dc, tn), lambda i, j: (layer, (da + db) // dc, j)),
        ],
        out_specs=pl.BlockSpec((tm, tn), lambda i, j: (i, j)),
        compiler_params=_params("parallel", "arbitrary"),
        name="out_project",
    )(x, y_pool, y_mlstm, y_nsa, w_out, w_out, w_out)


POOL_HALO = 16


def _pool_group(load, g, n_avail, w_ref, sc_ref):
    w = POOL_WINDOWS[g]
    z = load(0)
    acc = z
    for j in range(1, w):
        acc = acc + load(j)
    d = acc / jnp.minimum(n_avail, w).astype(F32) - z
    lead = d.shape[:-1]
    gd = d.shape[-1]
    y = jnp.dot(d.reshape(-1, gd).astype(BF16), w_ref[g].astype(BF16), preferred_element_type=F32)
    return (y * sc_ref[:, g * gd:(g + 1) * gd]).reshape(*lead, gd)


def _pool_prompt_body(z_ref, w_ref, sc_ref, o_ref, full_scr, *, chunk):
    seq, pd = z_ref.shape
    gd = pd // len(POOL_WINDOWS)
    full_scr[0:POOL_HALO, :] = jnp.zeros((POOL_HALO, pd), F32)
    full_scr[POOL_HALO:POOL_HALO + seq, :] = z_ref[...]
    for c in range(seq // chunk):
        n_avail = c * chunk + 1 + lax.broadcasted_iota(jnp.int32, (chunk, gd), 0)
        for g in range(len(POOL_WINDOWS)):
            load = lambda j: full_scr[pl.ds(POOL_HALO + c * chunk - j, chunk), g * gd:(g + 1) * gd]
            o_ref[c * chunk:(c + 1) * chunk, g * gd:(g + 1) * gd] = _pool_group(load, g, n_avail, w_ref, sc_ref)


def pool_prompt(zb, pool_w, pool_scale, *, batch, seq, col, out_rows=None):
    pd = pool_scale.shape[0]
    chunk = _pick_tile(seq, 256)
    return pl.pallas_call(
        functools.partial(_pool_prompt_body, chunk=chunk),
        out_shape=jax.ShapeDtypeStruct((out_rows or batch * seq, pd), F32),
        grid=(batch,),
        in_specs=[
            pl.BlockSpec((seq, pd), lambda b: (b, col // pd)),
            pl.BlockSpec(pool_w.shape, lambda b: (0, 0, 0)),
            pl.BlockSpec((1, pd), lambda b: (0, 0)),
        ],
        out_specs=pl.BlockSpec((seq, pd), lambda b: (b, 0)),
        scratch_shapes=[pltpu.VMEM((POOL_HALO + seq, pd), F32)],
        compiler_params=_params("parallel"),
        name="pool_prompt",
    )(zb, pool_w, pool_scale.reshape(1, pd))


def _pool_sample_body(full_ref, w_ref, sc_ref, o_ref, *, pos0):
    bt, rows, pd = full_ref.shape
    ts = rows - POOL_HALO
    gd = pd // len(POOL_WINDOWS)
    n_avail = pos0 + 1 + lax.broadcasted_iota(jnp.int32, (bt, ts, gd), 1)
    for g in range(len(POOL_WINDOWS)):
        load = lambda j: full_ref[:, pl.ds(POOL_HALO - j, ts), g * gd:(g + 1) * gd]
        o_ref[:, g * gd:(g + 1) * gd] = _pool_group(load, g, n_avail, w_ref, sc_ref).reshape(bt * ts, gd)


def pool_sample(full, pool_w, pool_scale, y_init, *, pos0, row0):
    batch, rows, pd = full.shape
    ts = rows - POOL_HALO
    bt = math.gcd(batch, 32)
    assert ts % 8 == 0 and row0 % (bt * ts) == 0
    r0 = row0 // (bt * ts)
    return pl.pallas_call(
        functools.partial(_drop_alias_ref, functools.partial(_pool_sample_body, pos0=pos0), 3),
        out_shape=jax.ShapeDtypeStruct(y_init.shape, F32),
        grid=(batch // bt,),
        in_specs=[
            pl.BlockSpec((bt, rows, pd), lambda b: (b, 0, 0)),
            pl.BlockSpec(pool_w.shape, lambda b: (0, 0, 0)),
            pl.BlockSpec((1, pd), lambda b: (0, 0)),
            pl.BlockSpec(memory_space=pl.ANY),
        ],
        out_specs=pl.BlockSpec((bt * ts, pd), lambda b: (r0 + b, 0)),
        input_output_aliases={3: 0},
        compiler_params=_params("parallel"),
        name="pool_sample",
    )(full, pool_w, pool_scale.reshape(1, pd), y_init)


def _log_sigmoid(x):
    return jnp.minimum(x, 0.0) - jnp.log1p(jnp.exp(-jnp.abs(x)))


def _mlstm_body(q_ref, k_ref, v_ref, og_ref, g_ref, bias_ref, gn_ref, c0_ref, n0_ref, m0_ref,
                y_ref, c_ref, n_ref, m_ref, *, L):
    nseq = q_ref.shape[0] // L
    H, D = MLSTM_HEADS, MLSTM_HD
    hi = lax.Precision.HIGHEST

    @pl.when(pl.program_id(1) == 0)
    def _():
        c_ref[...] = c0_ref[...]
        n_ref[...] = n0_ref[...]
        m_ref[...] = m0_ref[...]

    sel = (lax.broadcasted_iota(jnp.int32, (8, LANES), 0) == lax.broadcasted_iota(jnp.int32, (8, LANES), 1)).astype(F32)
    li = lax.broadcasted_iota(jnp.int32, (L, L), 0)
    si = lax.broadcasted_iota(jnp.int32, (L, L), 1)
    causal = li >= si
    lane = lax.broadcasted_iota(jnp.int32, (1, LANES), 1)
    chains = [(s, h) for s in range(nseq) for h in range(H)]

    gates = []
    for s in range(nseq):
        rows = slice(s * L, (s + 1) * L)
        gz = g_ref[rows, :] + bias_ref[...]
        gz_rows = lax.dot_general(sel, gz, NT, precision=hi, preferred_element_type=F32)
        b_cols = jnp.dot(causal.astype(F32), _log_sigmoid(gz), precision=hi, preferred_element_type=F32)
        b_rows = jnp.dot(_log_sigmoid(gz_rows), (li <= si).astype(F32), precision=hi, preferred_element_type=F32)
        gates.append((gz, gz_rows, b_cols, b_rows, m_ref[s]))

    qk, qc, state = {}, {}, {}
    for s, h in chains:
        rows, cols = slice(s * L, (s + 1) * L), slice(h * D, (h + 1) * D)
        qh = q_ref[rows, cols]
        kh = k_ref[rows, cols] * (D ** -0.5)
        ch = c_ref[s, h]
        nh = n_ref[s, h:h + 1, :]
        qb, kb = qh.astype(BF16), kh.astype(BF16)
        qk[s, h] = lax.dot_general(qb, kb, NT, preferred_element_type=F32)
        qc[s, h] = lax.dot_general(qb, ch.astype(BF16), NT, preferred_element_type=F32)
        state[s, h] = (qh, kh, kb, ch, nh)

    sm, stats = {}, {}
    for s, h in chains:
        gz, gz_rows, b_cols, b_rows, m_all = gates[s]
        bc = b_cols[:, H + h:H + h + 1]
        ic = gz[:, h:h + 1]
        br = b_rows[H + h:H + h + 1, :]
        ir = gz_rows[h:h + 1, :]
        m_prev = m_all[:, h:h + 1]
        dmat = jnp.where(causal, bc - br + ir, NEG_MASK)
        inter = bc + m_prev
        m_t = jnp.maximum(inter, jnp.max(dmat, axis=1, keepdims=True))
        sm[s, h] = qk[s, h] * jnp.exp(dmat - m_t)
        m_new = m_t[L - 1:L, :]
        b_last = bc[L - 1:L, :]
        stats[s, h] = (jnp.exp(inter - m_t), m_t, m_new, jnp.exp(b_last + m_prev - m_new),
                       jnp.exp(b_last - bc + ic - m_new))

    num, c_new = {}, {}
    for s, h in chains:
        rows, cols = slice(s * L, (s + 1) * L), slice(h * D, (h + 1) * D)
        a_inter, _, _, decay, w_col = stats[s, h]
        qh, kh, kb, ch, nh = state[s, h]
        vh = v_ref[rows, cols]
        num[s, h] = jnp.dot(sm[s, h].astype(BF16), vh.astype(BF16), preferred_element_type=F32) + a_inter * qc[s, h]
        c_new[s, h] = decay * ch + lax.dot_general((vh * w_col).astype(BF16), kb, TN, preferred_element_type=F32)

    m_out = [gates[s][4] for s in range(nseq)]
    for s, h in chains:
        rows, cols = slice(s * L, (s + 1) * L), slice(h * D, (h + 1) * D)
        a_inter, m_t, m_new, decay, w_col = stats[s, h]
        qh, kh, kb, ch, nh = state[s, h]
        den = jnp.sum(sm[s, h], axis=1, keepdims=True) + a_inter * jnp.sum(qh * nh, axis=1, keepdims=True)
        den = jnp.maximum(jnp.abs(den), jnp.exp(-m_t))
        hh = num[s, h] / den
        mu = jnp.mean(hh, axis=1, keepdims=True)
        var = jnp.mean(jnp.square(hh - mu), axis=1, keepdims=True)
        hn = (hh - mu) * lax.rsqrt(var + EPS) * gn_ref[:, cols]
        y_ref[rows, cols] = jax.nn.sigmoid(og_ref[rows, cols]) * hn
        c_ref[s, h] = c_new[s, h]
        n_ref[s, h:h + 1, :] = decay * nh + jnp.sum(kh * w_col, axis=0, keepdims=True)
        m_out[s] = jnp.where(lane == h, m_new, m_out[s])
    for s in range(nseq):
        m_ref[s] = m_out[s]


def mlstm_mix(zb, zs, if_bias, mnorm, c0, n0, m0, *, row0, batch, seq, col_q, y_init=None, out_rows=None):
    H, D = MLSTM_HEADS, MLSTM_HD
    dim = H * D
    L = math.gcd(seq, MLSTM_CHUNK)
    nc = seq // L
    nseq = math.gcd(batch, min(4, MLSTM_CHUNK // L)) if nc == 1 else 1
    rows = nseq * L
    assert L % 8 == 0 and row0 % rows == 0 and col_q % dim == 0
    r0 = row0 // rows
    cq = col_q // dim
    bias = jnp.pad(if_bias, (0, LANES - if_bias.shape[0])).reshape(1, LANES)
    m0p = jnp.pad(m0, ((0, 0), (0, LANES - H))).reshape(batch, 1, LANES)
    row = lambda b, c: r0 + b * nc + c
    y_rows = (out_rows or batch * seq) if y_init is None else y_init.shape[0]
    y_r0 = 0 if y_init is None else r0
    in_specs = [
        pl.BlockSpec((rows, dim), lambda b, c: (row(b, c), cq)),
        pl.BlockSpec((rows, dim), lambda b, c: (row(b, c), cq + 1)),
        pl.BlockSpec((rows, dim), lambda b, c: (row(b, c), cq + 2)),
        pl.BlockSpec((rows, dim), lambda b, c: (row(b, c), cq + 3)),
        pl.BlockSpec((rows, LANES), lambda b, c: (row(b, c), 0)),
        pl.BlockSpec((1, LANES), lambda b, c: (0, 0)),
        pl.BlockSpec((1, dim), lambda b, c: (0, 0)),
        pl.BlockSpec((nseq, H, D, D), lambda b, c: (b, 0, 0, 0)),
        pl.BlockSpec((nseq, H, D), lambda b, c: (b, 0, 0)),
        pl.BlockSpec((nseq, 1, LANES), lambda b, c: (b, 0, 0)),
    ]
    args = [zb, zb, zb, zb, zs, bias, mnorm.reshape(1, dim), c0, n0, m0p]
    aliases = {}
    body = functools.partial(_mlstm_body, L=L)
    if y_init is not None:
        in_specs.append(pl.BlockSpec(memory_space=pl.ANY))
        args.append(y_init)
        aliases = {len(args) - 1: 0}
        body = functools.partial(_drop_alias_ref, body, len(args) - 1)
    y, c_out, n_out, m_out = pl.pallas_call(
        body,
        out_shape=(jax.ShapeDtypeStruct((y_rows, dim), F32),
                   jax.ShapeDtypeStruct((batch, H, D, D), F32),
                   jax.ShapeDtypeStruct((batch, H, D), F32),
                   jax.ShapeDtypeStruct((batch, 1, LANES), F32)),
        grid=(batch // nseq, nc),
        in_specs=in_specs,
        out_specs=(
            pl.BlockSpec((rows, dim), lambda b, c: (y_r0 + b * nc + c, 0)),
            pl.BlockSpec((nseq, H, D, D), lambda b, c: (b, 0, 0, 0)),
            pl.BlockSpec((nseq, H, D), lambda b, c: (b, 0, 0)),
            pl.BlockSpec((nseq, 1, LANES), lambda b, c: (b, 0, 0)),
        ),
        input_output_aliases=aliases,
        compiler_params=_params("parallel", "arbitrary"),
        name="mlstm_mix",
    )(*args)
    return y, c_out, n_out, m_out[:, 0, :H]


KEY_TILE = 128
KEY_CHUNK = 256
SEL_COLS = 64
POS_HI, POS_LO = SEL_COLS, SEL_COLS + 1
NEG_SEL = -1.0e9


def _slope(h):
    return 2.0 ** (-(8.0 / NSA_HEADS) * (h + 1))


def _key_features(pos, onehot):
    lane = lax.broadcasted_iota(jnp.int32, pos.shape, 1)
    hi = lax.shift_right_logical(pos, 6)
    lo = jnp.bitwise_and(pos, SEL_BLOCK - 1)
    f = jnp.where(lane == POS_HI, hi.astype(F32), jnp.where(lane == POS_LO, lo.astype(F32), 0.0))
    if onehot:
        f = jnp.where(lane == hi, 1.0, f)
    return f


def _query_features(shape, h):
    lane = lax.broadcasted_iota(jnp.int32, shape, 1)
    return jnp.where(lane == POS_HI, SEL_BLOCK * _slope(h), jnp.where(lane == POS_LO, _slope(h), 0.0))


def _compress_block_rows(load, pw_ref, kv, g):
    cols = slice(g * NSA_HD, (g + 1) * NSA_HD)
    a0 = a1 = None
    for j in range(CMP_STRIDE):
        rows = load(j)
        t0 = rows * pw_ref[kv, j:j + 1, cols]
        t1 = rows * pw_ref[kv, CMP_STRIDE + j:CMP_STRIDE + j + 1, cols]
        a0 = t0 if a0 is None else a0 + t0
        a1 = t1 if a1 is None else a1 + t1
    return a0, a1


def _finish_compress(acc, proj_ref, kcmp_aug, vcmp, kv, g, n_ch, ncp, v_transposed):
    c = jnp.dot(acc.astype(BF16), proj_ref[kv, g].astype(BF16), preferred_element_type=F32)
    if kv == 0:
        kcmp_aug[g, 0:n_ch, 0:NSA_HD] = c.astype(BF16)
        n = lax.broadcasted_iota(jnp.int32, (ncp, LANES), 0)
        kcmp_aug[g, :, NSA_HD:2 * NSA_HD] = _key_features(n * CMP_STRIDE + (CMP_LEN - 1), False).astype(BF16)
    elif v_transposed:
        if ncp > n_ch:
            c = jnp.concatenate([c, jnp.zeros((ncp - n_ch, NSA_HD), F32)], axis=0)
        for blk in range(ncp // LANES):
            vcmp[g, :, blk * LANES:(blk + 1) * LANES] = c[blk * LANES:(blk + 1) * LANES].T.astype(BF16)
    else:
        vcmp[g, 0:n_ch, :] = c.astype(BF16)


def _masked_softmax(s, mask):
    s = jnp.where(mask, s, NEG_MASK)
    m = jnp.max(s, axis=1, keepdims=True)
    e = jnp.where(mask, jnp.exp(s - m), 0.0)
    return e / jnp.maximum(jnp.sum(e, axis=1, keepdims=True), 1e-30)


def _select_blocks(psum, t0, n_cmp, n_sel, queries_on_lanes):
    ncp = psum.shape[0] if queries_on_lanes else psum.shape[1]
    nsp = -(-n_sel // 8) * 8
    j = lax.broadcasted_iota(jnp.int32, (nsp, ncp), 0)
    n = lax.broadcasted_iota(jnp.int32, (nsp, ncp), 1)
    cover = ((n * CMP_STRIDE < j * SEL_BLOCK + SEL_BLOCK) & (n * CMP_STRIDE + CMP_LEN > j * SEL_BLOCK)
             & (n < n_cmp)).astype(F32)
    if queries_on_lanes:
        imp = jnp.dot(cover, psum, precision=lax.Precision.HIGHEST, preferred_element_type=F32)
    else:
        imp = lax.dot_general(cover, psum, NT, precision=lax.Precision.HIGHEST, preferred_element_type=F32)
    jq = lax.broadcasted_iota(jnp.int32, (nsp, LANES), 0)
    t = t0 + lax.broadcasted_iota(jnp.int32, (nsp, LANES), 1)
    cur = lax.shift_right_logical(t, 6)
    forced = (jq == 0) | (jq == cur) | (jq == cur - 1)
    valid = (jq * SEL_BLOCK <= t) & (jq < n_sel)
    score = jnp.where(valid, imp + jnp.where(forced, FORCE_BONUS, 0.0), -jnp.inf)
    jf = jq.astype(F32)
    sel = jnp.zeros((nsp, LANES), F32)
    for _ in range(min(SEL_TOPN, n_sel)):
        mx = jnp.max(score, axis=0, keepdims=True)
        first = jnp.min(jnp.where(score == mx, jf, 1.0e9), axis=0, keepdims=True)
        pick = jf == first
        sel = jnp.where(pick, 1.0, sel)
        score = jnp.where(pick, -jnp.inf, score)
    bias = jnp.where((sel > 0.5) | (jq >= n_sel), 0.0, NEG_SEL)
    bias = jnp.concatenate([bias, jnp.zeros((LANES - nsp, LANES), F32)], axis=0)
    return bias.T


def _online_step_t(state, s_t, v_t):
    m, l, acc = state
    m_new = jnp.maximum(m, jnp.max(s_t, axis=0, keepdims=True))
    alpha = jnp.exp(m - m_new)
    p = jnp.exp(s_t - m_new)
    l = alpha * l + jnp.sum(p, axis=0, keepdims=True)
    acc = alpha * acc + jnp.dot(v_t, p.astype(BF16), preferred_element_type=F32)
    return m_new, l, acc


def _stack_heads(q_heads, feats):
    return jnp.concatenate(
        [jnp.concatenate([q, f.astype(BF16)], axis=1) for q, f in zip(q_heads, feats)], axis=0)


def _write_gated(o_ref, gates, g, rows, o_cmp, o_s, o_w):
    for r in range(NSA_GROUP):
        h = g * NSA_GROUP + r
        sl = slice(r * rows, (r + 1) * rows)
        c = 8 + h
        o = gates[:, c:c + 1] * o_cmp[sl] + gates[:, c + 8:c + 9] * o_s[sl] + gates[:, c + 16:c + 17] * o_w[sl]
        o_ref[:, h * NSA_HD:(h + 1) * NSA_HD] = o


def _nsa_prompt_body(q_ref, kc_ref, ks_ref, kw_ref, gate_ref, pw_ref, proj_ref, o_ref,
                     ks_aug, vs_t, kw_aug, vw_t, kcmp_aug, vcmp_t, a1_scr, *, seq):
    i = pl.program_id(1)
    tq = KEY_TILE
    n_ch = seq // CMP_STRIDE
    n_cmp = n_ch - 1
    n_sel = seq // SEL_BLOCK
    ncp = kcmp_aug.shape[1]
    G, HD = NSA_GROUP, NSA_HD
    Q = G * tq

    @pl.when(i == 0)
    def _build():
        pos = lax.broadcasted_iota(jnp.int32, (seq, LANES), 0)
        f_sel = _key_features(pos, True).astype(BF16)
        f_win = _key_features(pos, False).astype(BF16)
        for g in range(NSA_KV_HEADS):
            ks_aug[g, :, 0:HD] = ks_ref[pl.ds(g, seq, stride=KV_ROW), :].astype(BF16)
            ks_aug[g, :, HD:2 * HD] = f_sel
            kw_aug[g, :, 0:HD] = kw_ref[pl.ds(g, seq, stride=KV_ROW), :].astype(BF16)
            kw_aug[g, :, HD:2 * HD] = f_win
            for kt in range(seq // KEY_TILE):
                rows = slice(kt * KEY_TILE, (kt + 1) * KEY_TILE)
                src_rows = pl.ds(kt * KEY_TILE * KV_ROW + 2 + g, KEY_TILE, stride=KV_ROW)
                vs_t[g, :, rows] = ks_ref[src_rows, :].T.astype(BF16)
                vw_t[g, :, rows] = kw_ref[src_rows, :].T.astype(BF16)
        kcmp_aug[...] = jnp.zeros(kcmp_aug.shape, BF16)
        a1_scr[n_ch:n_ch + 8, :] = jnp.zeros((8, HD), F32)
        for kv in range(2):
            for g in range(NSA_KV_HEADS):
                c = kv * NSA_KV_HEADS + g
                a0, a1 = _compress_block_rows(
                    lambda j: kc_ref[pl.ds(j * KV_ROW + c, n_ch, stride=KV_ROW * CMP_STRIDE), :], pw_ref, kv, g)
                a1_scr[0:n_ch, :] = a1
                _finish_compress(a0 + a1_scr[pl.ds(1, n_ch), :], proj_ref, kcmp_aug, vcmp_t, kv, g, n_ch, ncp, True)

    t0 = i * tq
    key = lax.broadcasted_iota(jnp.int32, (KEY_CHUNK, Q), 0)
    t_cols = t0 + jnp.bitwise_and(lax.broadcasted_iota(jnp.int32, (KEY_CHUNK, Q), 1), tq - 1)
    gates_t = jax.nn.sigmoid(gate_ref[...]).T
    scale = HD ** -0.5

    n = lax.broadcasted_iota(jnp.int32, (ncp, Q), 0)
    t_c = t0 + jnp.bitwise_and(lax.broadcasted_iota(jnp.int32, (ncp, Q), 1), tq - 1)
    mask = (n * CMP_STRIDE + (CMP_LEN - 1) <= t_c) & (n < n_cmp)
    kv_groups = range(NSA_KV_HEADS)
    q_heads = [[(q_ref[:, (g * G + r) * HD:(g * G + r + 1) * HD] * scale).astype(BF16) for r in range(G)]
               for g in kv_groups]
    feats = [[_query_features((tq, LANES), g * G + r) for r in range(G)] for g in kv_groups]
    q_plain = [_stack_heads(q_heads[g], feats[g]) for g in kv_groups]
    s_c = [jnp.where(mask, lax.dot_general(kcmp_aug[g], q_plain[g], NT, preferred_element_type=F32), NEG_MASK)
           for g in kv_groups]
    e_c = [jnp.where(mask, jnp.exp(s - jnp.max(s, axis=0, keepdims=True)), 0.0) for s in s_c]
    p_c = [e / jnp.maximum(jnp.sum(e, axis=0, keepdims=True), 1e-30) for e in e_c]
    o_cmp = [jnp.dot(vcmp_t[g], p_c[g].astype(BF16), preferred_element_type=F32) for g in kv_groups]
    psum = [p[:, 0:tq] + p[:, tq:2 * tq] + p[:, 2 * tq:3 * tq] + p[:, 3 * tq:4 * tq] for p in p_c]
    bias = [_select_blocks(psum[g], t0, n_cmp, n_sel, True) for g in kv_groups]
    q_sel = [_stack_heads(q_heads[g], [f + bias[g] for f in feats[g]]) for g in kv_groups]

    last = t0 // KEY_CHUNK
    first_w = jnp.maximum(t0 - WINDOW, 0) // KEY_CHUNK
    init = (jnp.full((1, Q), NEG_MASK, F32), jnp.zeros((1, Q), F32), jnp.zeros((HD, Q), F32))

    def scores(c, k_aug, q, g, valid):
        off = pl.multiple_of(c * KEY_CHUNK, KEY_CHUNK)
        s_t = lax.dot_general(k_aug[g, pl.ds(off, KEY_CHUNK), :], q, NT, preferred_element_type=F32)
        return s_t if valid is None else jnp.where(valid(off + key), s_t, NEG_MASK)

    def values(c, v_t, g):
        return v_t[g, :, pl.ds(pl.multiple_of(c * KEY_CHUNK, KEY_CHUNK), KEY_CHUNK)]

    def causal(kpos):
        return kpos <= t_cols

    def band(kpos):
        return (kpos <= t_cols) & (t_cols - kpos <= WINDOW)

    def early(c, sel):
        s = [scores(c, ks_aug, q_sel[g], g, None) for g in kv_groups]
        return tuple(_online_step_t(sel[g], s[g], values(c, vs_t, g)) for g in kv_groups)

    def late(c, sts):
        sel, win = sts
        s_sel = [scores(c, ks_aug, q_sel[g], g, causal) for g in kv_groups]
        s_win = [scores(c, kw_aug, q_plain[g], g, band) for g in kv_groups]
        sel = tuple(_online_step_t(sel[g], s_sel[g], values(c, vs_t, g)) for g in kv_groups)
        win = tuple(_online_step_t(win[g], s_win[g], values(c, vw_t, g)) for g in kv_groups)
        return sel, win

    inits = (init,) * NSA_KV_HEADS
    sel = lax.fori_loop(0, first_w, early, inits)
    sel, win = lax.fori_loop(first_w, last + 1, late, (sel, inits))

    for g in kv_groups:
        o_s = sel[g][2] / sel[g][1]
        o_w = win[g][2] / win[g][1]
        for r in range(G):
            h = g * G + r
            cols = slice(r * tq, (r + 1) * tq)
            c = 8 + h
            o_t = (gates_t[c:c + 1, :] * o_cmp[g][:, cols] + gates_t[c + 8:c + 9, :] * o_s[:, cols]
                   + gates_t[c + 16:c + 17, :] * o_w[:, cols])
            o_ref[:, h * HD:(h + 1) * HD] = o_t.T


def nsa_prompt(zb, zs, kv_cmp, kv_slc, kv_win, cmp_pos_w, cmp_proj, *, layer, batch, seq, col_q, out_rows=None):
    assert seq % KEY_CHUNK == 0 and seq // SEL_BLOCK <= SEL_COLS
    nq = seq // KEY_TILE
    n_ch = seq // CMP_STRIDE
    ncp = -(-n_ch // LANES) * LANES
    qw = NSA_HEADS * NSA_HD
    pw = cmp_pos_w.reshape(2, CMP_LEN, 2 * NSA_HD)
    return pl.pallas_call(
        functools.partial(_nsa_prompt_body, seq=seq),
        out_shape=jax.ShapeDtypeStruct((out_rows or batch * seq, qw), F32),
        grid=(batch, nq),
        in_specs=[
            pl.BlockSpec((KEY_TILE, qw), lambda b, i: (b * nq + i, col_q // qw)),
            pl.BlockSpec((None, seq * KV_ROW, NSA_HD), lambda b, i: (layer, b, 0)),
            pl.BlockSpec((None, seq * KV_ROW, NSA_HD), lambda b, i: (layer, b, 0)),
            pl.BlockSpec((None, seq * KV_ROW, NSA_HD), lambda b, i: (layer, b, 0)),
            pl.BlockSpec((KEY_TILE, LANES), lambda b, i: (b * nq + i, 0)),
            pl.BlockSpec((2, CMP_LEN, 2 * NSA_HD), lambda b, i: (0, 0, 0)),
            pl.BlockSpec((2, NSA_KV_HEADS, NSA_HD, NSA_HD), lambda b, i: (0, 0, 0, 0)),
        ],
        out_specs=pl.BlockSpec((KEY_TILE, qw), lambda b, i: (b * nq + i, 0)),
        scratch_shapes=[
            pltpu.VMEM((NSA_KV_HEADS, seq, 2 * NSA_HD), BF16),
            pltpu.VMEM((NSA_KV_HEADS, NSA_HD, seq), BF16),
            pltpu.VMEM((NSA_KV_HEADS, seq, 2 * NSA_HD), BF16),
            pltpu.VMEM((NSA_KV_HEADS, NSA_HD, seq), BF16),
            pltpu.VMEM((NSA_KV_HEADS, ncp, 2 * NSA_HD), BF16),
            pltpu.VMEM((NSA_KV_HEADS, NSA_HD, ncp), BF16),
            pltpu.VMEM((n_ch + 8, NSA_HD), F32),
        ],
        compiler_params=_params("parallel", "arbitrary"),
        name="nsa_prompt",
    )(zb, kv_cmp, kv_slc, kv_win, zs, pw, cmp_proj)


def _nsa_sample_body(pt_ref, q_ref, ksn_ref, kwn_ref, gate_ref, wprev_ref, pw_ref, proj_ref, *rest,
                     ts, past, n_pages):
    del pt_ref
    cmp_pages = rest[:n_pages]
    slc_pages = rest[n_pages:2 * n_pages]
    o_ref = rest[2 * n_pages + 1]
    ks_aug, vs, kw_aug, vw, kcmp_aug, vcmp, acc_scr = rest[2 * n_pages + 2:]
    G, HD = NSA_GROUP, NSA_HD
    page = slc_pages[0].shape[0] // 4
    kp = ks_aug.shape[1]
    wprev = wprev_ref.shape[0] // 4
    wp = kw_aug.shape[1]
    win_pos0 = past - wprev
    n_ch = (past + ts) // CMP_STRIDE
    n_cmp = n_ch - 1
    n_sel = -(-(past + ts) // SEL_BLOCK)
    ncp = kcmp_aug.shape[1]
    ch_per_page = page // CMP_STRIDE

    @pl.when(pl.program_id(0) == 0)
    def _constants():
        pos = lax.broadcasted_iota(jnp.int32, (kp, LANES), 0)
        f_sel = _key_features(pos, True).astype(BF16)
        posw = win_pos0 + lax.broadcasted_iota(jnp.int32, (wp, LANES), 0)
        f_win = _key_features(posw, False).astype(BF16)
        for g in range(NSA_KV_HEADS):
            ks_aug[g, :, HD:2 * HD] = f_sel
            kw_aug[g, :, HD:2 * HD] = f_win
        kcmp_aug[...] = jnp.zeros(kcmp_aug.shape, BF16)
        vcmp[...] = jnp.zeros(vcmp.shape, BF16)

    def with_tail(new_rows):
        return jnp.concatenate([new_rows, jnp.zeros((KEY_TILE - ts, HD), F32)], axis=0).astype(BF16)

    for g in range(NSA_KV_HEADS):
        for p in range(n_pages):
            rows = slice(p * page, (p + 1) * page)
            ks_aug[g, rows, 0:HD] = slc_pages[p][pl.ds(g, page, stride=4), :].astype(BF16)
            vs[g, rows, :] = slc_pages[p][pl.ds(2 + g, page, stride=4), :].astype(BF16)
        ks_aug[g, past:past + KEY_TILE, 0:HD] = with_tail(ksn_ref[pl.ds(g, ts, stride=KV_ROW), :])
        vs[g, past:past + KEY_TILE, :] = with_tail(ksn_ref[pl.ds(2 + g, ts, stride=KV_ROW), :])
        kw_aug[g, 0:wprev, 0:HD] = wprev_ref[pl.ds(g, wprev, stride=4), :].astype(BF16)
        vw[g, 0:wprev, :] = wprev_ref[pl.ds(2 + g, wprev, stride=4), :].astype(BF16)
        kw_aug[g, wprev:wprev + KEY_TILE, 0:HD] = with_tail(kwn_ref[pl.ds(g, ts, stride=KV_ROW), :])
        vw[g, wprev:wprev + KEY_TILE, :] = with_tail(kwn_ref[pl.ds(2 + g, ts, stride=KV_ROW), :])

    for kv in range(2):
        for g in range(NSA_KV_HEADS):
            c = kv * NSA_KV_HEADS + g
            cols = slice(g * HD, (g + 1) * HD)
            w0 = jnp.concatenate([pw_ref[kv, 0:CMP_STRIDE, cols]] * ch_per_page, axis=0)
            w1 = jnp.concatenate([pw_ref[kv, CMP_STRIDE:CMP_LEN, cols]] * ch_per_page, axis=0)
            for p in range(n_pages):
                x = cmp_pages[p][pl.ds(c, page, stride=4), :]
                if p + 1 < n_pages:
                    nxt = cmp_pages[p + 1][pl.ds(c, CMP_STRIDE, stride=4), :] * w1[0:CMP_STRIDE]
                else:
                    nxt = jnp.zeros((CMP_STRIDE, HD), F32)
                z = x * w0 + jnp.concatenate([(x * w1)[CMP_STRIDE:], nxt], axis=0)
                acc_scr[c, p * ch_per_page:(p + 1) * ch_per_page, :] = jnp.sum(
                    z.reshape(ch_per_page, CMP_STRIDE, HD), axis=1)
            _finish_compress(acc_scr[c, 0:n_ch, :], proj_ref, kcmp_aug, vcmp, kv, g, n_ch, ncp, False)

    R = G * ts
    gates = jax.nn.sigmoid(gate_ref[...])
    scale = HD ** -0.5

    def t_of(shape):
        return past + jnp.bitwise_and(lax.broadcasted_iota(jnp.int32, shape, 0), ts - 1)

    def softmax_pv(s, v):
        m = jnp.max(s, axis=1, keepdims=True)
        e = jnp.exp(s - m)
        return jnp.dot(e.astype(BF16), v, preferred_element_type=F32) / jnp.sum(e, axis=1, keepdims=True)

    kv_groups = range(NSA_KV_HEADS)
    q_heads = [[(q_ref[:, (g * G + r) * HD:(g * G + r + 1) * HD] * scale).astype(BF16) for r in range(G)]
               for g in kv_groups]
    feats = [[_query_features((ts, LANES), g * G + r) for r in range(G)] for g in kv_groups]
    q_plain = [_stack_heads(q_heads[g], feats[g]) for g in kv_groups]

    idx = lax.broadcasted_iota(jnp.int32, (R, wp), 1)
    dist = t_of((R, wp)) - (win_pos0 + idx)
    win_ok = (idx < wprev + ts) & (dist >= 0) & (dist <= WINDOW)
    s_win = [jnp.where(win_ok, lax.dot_general(q_plain[g], kw_aug[g], NT, preferred_element_type=F32), NEG_MASK)
             for g in kv_groups]
    n = lax.broadcasted_iota(jnp.int32, (R, ncp), 1)
    cmp_ok = (n * CMP_STRIDE + (CMP_LEN - 1) <= t_of((R, ncp))) & (n < n_cmp)
    s_cmp = [lax.dot_general(q_plain[g], kcmp_aug[g], NT, preferred_element_type=F32) for g in kv_groups]
    p_c = [_masked_softmax(s, cmp_ok) for s in s_cmp]
    o_cmp = [jnp.dot(p_c[g].astype(BF16), vcmp[g], preferred_element_type=F32) for g in kv_groups]
    o_w = [softmax_pv(s_win[g], vw[g]) for g in kv_groups]
    psum = [jnp.concatenate([p[0:ts] + p[ts:2 * ts] + p[2 * ts:3 * ts] + p[3 * ts:4 * ts],
                             jnp.zeros((LANES - ts, ncp), F32)], axis=0) for p in p_c]
    bias = [_select_blocks(psum[g], past, n_cmp, n_sel, False)[0:ts] for g in kv_groups]
    q_sel = [_stack_heads(q_heads[g], [f + bias[g] for f in feats[g]]) for g in kv_groups]
    sel_ok = lax.broadcasted_iota(jnp.int32, (R, kp), 1) <= t_of((R, kp))
    s_sel = [jnp.where(sel_ok, lax.dot_general(q_sel[g], ks_aug[g], NT, preferred_element_type=F32), NEG_MASK)
             for g in kv_groups]
    o_s = [softmax_pv(s_sel[g], vs[g]) for g in kv_groups]
    for g in kv_groups:
        _write_gated(o_ref, gates, g, ts, o_cmp[g], o_s[g], o_w[g])


def nsa_sample(zb, zs, kv_slc, kv_win, cache_cmp, cache_slc, state_win, page_table, cmp_pos_w, cmp_proj, y_init, *,
               layer, row0, batch, ts, col_q):
    depth, n_pool, page = cache_cmp.shape[:3]
    n_pages = page_table.shape[1]
    past = n_pages * page
    wprev = state_win.shape[2]
    assert ts & (ts - 1) == 0 and ts <= KEY_TILE and row0 % ts == 0
    assert past % KEY_TILE == 0 and (past + ts) // CMP_STRIDE == past // CMP_STRIDE
    assert page % CMP_STRIDE == 0 and -(-(past + ts) // SEL_BLOCK) <= SEL_COLS and wprev % 16 == 0
    qw = NSA_HEADS * NSA_HD
    n_ch = past // CMP_STRIDE
    ncp = -(-n_ch // LANES) * LANES
    r0 = row0 // ts
    pw = cmp_pos_w.reshape(2, CMP_LEN, 2 * NSA_HD)
    cmp_view = cache_cmp.reshape(depth, n_pool, page * 4, NSA_HD)
    slc_view = cache_slc.reshape(depth, n_pool, page * 4, NSA_HD)
    win_view = state_win.reshape(depth, batch, wprev * 4, NSA_HD)

    def page_map(p):
        return lambda b, pt: (layer, pt[b * n_pages + p], 0, 0)

    in_specs = [
        pl.BlockSpec((ts, qw), lambda b, pt: (r0 + b, col_q // qw)),
        pl.BlockSpec((None, ts * KV_ROW, NSA_HD), lambda b, pt: (layer, b, 0)),
        pl.BlockSpec((None, ts * KV_ROW, NSA_HD), lambda b, pt: (layer, b, 0)),
        pl.BlockSpec((ts, LANES), lambda b, pt: (r0 + b, 0)),
        pl.BlockSpec((None, None, wprev * 4, NSA_HD), lambda b, pt: (layer, b, 0, 0)),
        pl.BlockSpec((2, CMP_LEN, 2 * NSA_HD), lambda b, pt: (0, 0, 0)),
        pl.BlockSpec((2, NSA_KV_HEADS, NSA_HD, NSA_HD), lambda b, pt: (0, 0, 0, 0)),
    ]
    in_specs += [pl.BlockSpec((None, None, page * 4, NSA_HD), page_map(p % n_pages)) for p in range(2 * n_pages)]
    in_specs.append(pl.BlockSpec(memory_space=pl.ANY))
    return pl.pallas_call(
        functools.partial(_nsa_sample_body, ts=ts, past=past, n_pages=n_pages),
        out_shape=jax.ShapeDtypeStruct(y_init.shape, F32),
        grid_spec=pltpu.PrefetchScalarGridSpec(
            num_scalar_prefetch=1,
            grid=(batch,),
            in_specs=in_specs,
            out_specs=pl.BlockSpec((ts, qw), lambda b, pt: (r0 + b, 0)),
            scratch_shapes=[
                pltpu.VMEM((NSA_KV_HEADS, past + KEY_TILE, 2 * NSA_HD), BF16),
                pltpu.VMEM((NSA_KV_HEADS, past + KEY_TILE, NSA_HD), BF16),
                pltpu.VMEM((NSA_KV_HEADS, wprev + KEY_TILE, 2 * NSA_HD), BF16),
                pltpu.VMEM((NSA_KV_HEADS, wprev + KEY_TILE, NSA_HD), BF16),
                pltpu.VMEM((NSA_KV_HEADS, ncp, 2 * NSA_HD), BF16),
                pltpu.VMEM((NSA_KV_HEADS, ncp, NSA_HD), BF16),
                pltpu.VMEM((2 * NSA_KV_HEADS, n_ch, NSA_HD), F32),
            ],
        ),
        input_output_aliases={8 + 2 * n_pages: 0},
        compiler_params=_params("arbitrary"),
        name="nsa_sample",
    )(page_table.reshape(-1), zb, kv_slc, kv_win, zs, win_view, pw, cmp_proj,
      *([cmp_view] * n_pages), *([slc_view] * n_pages), y_init)


def _window_state_body(old_ref, new_ref, o_ref):
    keep = o_ref.shape[1] - new_ref.shape[1]
    o_ref[:, 0:keep, :] = old_ref[:, old_ref.shape[1] - keep:, :]
    o_ref[:, keep:, :] = new_ref[...]


def window_state(state_win, kv_win_new, ts):
    depth, batch, w_old = state_win.shape[:3]
    keep = min(WINDOW, w_old + ts) - ts
    assert keep >= 0 and (keep * KV_ROW) % 8 == 0 and (ts * KV_ROW) % 8 == 0
    bt = math.gcd(batch, 4)
    out = pl.pallas_call(
        _window_state_body,
        out_shape=jax.ShapeDtypeStruct((depth, batch, (keep + ts) * KV_ROW, NSA_HD), F32),
        grid=(depth, batch // bt),
        in_specs=[
            pl.BlockSpec((None, bt, w_old * KV_ROW, NSA_HD), lambda l, b: (l, b, 0, 0)),
            pl.BlockSpec((None, bt, ts * KV_ROW, NSA_HD), lambda l, b: (l, b, 0, 0)),
        ],
        out_specs=pl.BlockSpec((None, bt, (keep + ts) * KV_ROW, NSA_HD), lambda l, b: (l, b, 0, 0)),
        compiler_params=_params("parallel", "parallel"),
        name="window_state",
    )(state_win.reshape(depth, batch, w_old * KV_ROW, NSA_HD), kv_win_new.reshape(depth, batch, ts * KV_ROW, NSA_HD))
    return out.reshape(depth, batch, keep + ts, *state_win.shape[3:])


ZB_NQ, ZB_POOL, ZB_MQ, ZB_MK, ZB_MV, ZB_MO, ZB_END = (0, 1024, 1536, 2048, 2560, 3072, 3584)
W_POOL, W_MI, W_NQ, W_CMP, W_NG, W_END = 0, 2560, 2568, 3592, 5128, 5152


def _split_w_in(w_in_l):
    w = w_in_l.astype(BF16)
    big = jnp.concatenate([w[:, W_NQ:W_CMP], w[:, W_POOL:W_MI], w[:, W_CMP:W_NG]], axis=1)
    small = jnp.concatenate([w[:, W_MI:W_NQ], w[:, W_NG:W_END]], axis=1)
    return big, jnp.pad(small, ((0, 0), (0, LANES - small.shape[1])))


def kernel(x_prompt, x_sample, cache_kv_cmp, cache_kv_slc, state_kv_win, state_pool, state_mlstm_C, state_mlstm_n, state_mlstm_m, page_table, ffn1_norm, ffn1_w_gate, ffn1_w_up, ffn1_w_down, mix_norm, w_in, w_out, pool_w, pool_scale, mlstm_if_bias, mlstm_norm, nsa_cmp_pos_w, nsa_cmp_proj, ffn2_norm, ffn2_w_gate, ffn2_w_up, ffn2_w_down, final_norm):
    bp, tp, d = x_prompt.shape
    bs, ts, _ = x_sample.shape
    depth = w_in.shape[0]
    mp, ms = bp * tp, bs * ts
    m_all = mp + ms
    past_len = page_table.shape[1] * cache_kv_cmp.shape[2]
    pd = pool_scale.shape[1]
    kv_row = (2, NSA_KV_HEADS, NSA_HD)
    x = jnp.concatenate([x_prompt.reshape(mp, d), x_sample.reshape(ms, d)], axis=0)
    zeros = lambda *s: jnp.zeros(s, F32)
    ffn1 = [w.astype(BF16) for w in (ffn1_w_gate, ffn1_w_up, ffn1_w_down)]
    ffn2 = [w.astype(BF16) for w in (ffn2_w_gate, ffn2_w_up, ffn2_w_down)]
    w_out_b = w_out.astype(BF16)
    outs = [[] for _ in range(8)]
    kv_bufs = None
    for l in range(depth):
        x = ffn_half_step(x, ffn1_norm[l], *ffn1, l)
        w_big, w_small = _split_w_in(w_in[l])
        zb, zs, kv_bufs = mix_project(x, mix_norm[l], w_big, w_small, l, depth, mp, kv_bufs)
        cmp_p, slc_p, win_p, cmp_s, slc_s, win_s = kv_bufs
        z_pool = zb[:, ZB_POOL:ZB_MQ]

        y_pool = pool_prompt(zb, pool_w[l], pool_scale[l], batch=bp, seq=tp, col=ZB_POOL, out_rows=m_all)
        y_m, p_c, p_n, p_m = mlstm_mix(
            zb, zs, mlstm_if_bias[l], mlstm_norm[l], zeros(bp, MLSTM_HEADS, MLSTM_HD, MLSTM_HD),
            zeros(bp, MLSTM_HEADS, MLSTM_HD), zeros(bp, MLSTM_HEADS), row0=0, batch=bp, seq=tp, col_q=ZB_MQ,
            out_rows=m_all)
        y_nsa = nsa_prompt(zb, zs, cmp_p, slc_p, win_p, nsa_cmp_pos_w[l], nsa_cmp_proj[l], layer=l, batch=bp,
                           seq=tp, col_q=ZB_NQ, out_rows=m_all)

        pool_full = jnp.concatenate([zeros(bs, POOL_HALO - POOL_BUF, pd), state_pool[l],
                                     z_pool[mp:].reshape(bs, ts, pd)], axis=1)
        y_pool = pool_sample(pool_full, pool_w[l], pool_scale[l], y_pool, pos0=past_len, row0=mp)
        y_m, s_c, s_n, s_m = mlstm_mix(
            zb, zs, mlstm_if_bias[l], mlstm_norm[l], state_mlstm_C[l], state_mlstm_n[l], state_mlstm_m[l],
            row0=mp, batch=bs, seq=ts, col_q=ZB_MQ, y_init=y_m)
        y_nsa = nsa_sample(zb, zs, slc_s, win_s, cache_kv_cmp, cache_kv_slc, state_kv_win, page_table,
                           nsa_cmp_pos_w[l], nsa_cmp_proj[l], y_nsa, layer=l, row0=mp, batch=bs, ts=ts,
                           col_q=ZB_NQ)

        x = out_project(x, y_pool, y_m, y_nsa, w_out_b, l)
        x = ffn_half_step(x, ffn2_norm[l], *ffn2, l, gf=final_norm if l == depth - 1 else None)

        pool_p = jnp.concatenate([zeros(bp, POOL_BUF, pd), z_pool[:mp].reshape(bp, tp, pd)], axis=1)
        layer_out = (pool_p[:, -POOL_BUF:], p_c, p_n, p_m, pool_full[:, -POOL_BUF:], s_c, s_n, s_m)
        for acc, a in zip(outs, layer_out):
            acc.append(a)
    y_prompt = x[:mp].reshape(bp, tp, d)
    y_sample = x[mp:].reshape(bs, ts, d)
    p_pool, p_c, p_n, p_m, s_pool, s_c, s_n, s_m = [jnp.stack(a) for a in outs]
    p_kv = [b.reshape(depth, bp, tp, *kv_row) for b in (cmp_p, slc_p, win_p)]
    s_kv = [b.reshape(depth, bs, ts, *kv_row) for b in (cmp_s, slc_s, win_s)]
    s_kv_win = window_state(state_kv_win, win_s, ts)
    return (y_prompt, y_sample, p_kv[0], p_kv[1], p_kv[2][:, :, tp - min(WINDOW, tp):], p_pool, p_c, p_n, p_m,
            s_kv[0], s_kv[1], s_kv_win, s_pool, s_c, s_n, s_m)
```

```python
import functools
import math

import jax
import jax.numpy as jnp
from jax import lax
from jax.experimental import pallas as pl
from jax.experimental.pallas import tpu as pltpu

F32 = jnp.float32
BF16 = jnp.bfloat16
EPS = 1e-6

VMEM_LIMIT_BYTES = 56 * 1024 * 1024
LANES = 128

POOL_WINDOWS = (2, 4, 8, 16)
POOL_BUF = 15
MLSTM_HEADS = 4
MLSTM_HD = 128
MLSTM_CHUNK = 64
NSA_HD = 128
NSA_HEADS = 8
NSA_KV_HEADS = 2
NSA_GROUP = 4
CMP_LEN = 32
CMP_STRIDE = 16
SEL_BLOCK = 64
SEL_TOPN = 16
WINDOW = 512
FORCE_BONUS = 1.0e4

NEG_MASK = -1.0e30
NT = (((1,), (1,)), ((), ()))
TN = (((0,), (0,)), ((), ()))


def _pick_tile(n, pref):
    t = pref
    while t > 8 and n % t:
        t //= 2
    assert n % t == 0, (n, pref)
    return t


def _params(*sem):
    return pltpu.CompilerParams(dimension_semantics=sem, vmem_limit_bytes=VMEM_LIMIT_BYTES)


def _drop_alias_ref(body, index, *refs):
    return body(*refs[:index], *refs[index + 1:])


def _rms_rows(x, g):
    ms = jnp.mean(x * x, axis=-1, keepdims=True)
    return x * lax.rsqrt(ms + EPS) * g


def _ffn_body(x_ref, g_ref, wg_ref, wu_ref, wd_ref, gf_ref, o_ref, n_scr, *, final_norm):
    f = pl.program_id(1)

    @pl.when(f == 0)
    def _():
        x = x_ref[...]
        n_scr[...] = _rms_rows(x, g_ref[...]).astype(BF16)
        o_ref[...] = x

    n = n_scr[...]
    hg = jnp.dot(n, wg_ref[...], preferred_element_type=F32)
    hu = jnp.dot(n, wu_ref[...], preferred_element_type=F32)
    h = (hg * jax.nn.sigmoid(hg) * hu).astype(BF16)
    o_ref[...] += 0.5 * jnp.dot(h, wd_ref[...], preferred_element_type=F32)

    if final_norm:
        @pl.when(f == pl.num_programs(1) - 1)
        def _():
            o_ref[...] = _rms_rows(o_ref[...], gf_ref[...])


def ffn_half_step(x, g, wg, wu, wd, layer, gf=None, *, in_row0=0, rows=None, out_rows=None, out_row0=0,
                  y_init=None):
    d = x.shape[1]
    rows = x.shape[0] if rows is None else rows
    fdim = wg.shape[2]
    tm = _pick_tile(math.gcd(math.gcd(rows, in_row0), out_row0), 512)
    tf = _pick_tile(fdim, 512)
    final_norm = gf is not None
    if gf is None:
        gf = g
    i0, o0 = in_row0 // tm, out_row0 // tm
    body = functools.partial(_ffn_body, final_norm=final_norm)
    extra_specs, extra_args, aliases = [], [], {}
    if y_init is not None:
        body = functools.partial(_drop_alias_ref, body, 6)
        extra_specs, extra_args, aliases = [pl.BlockSpec(memory_space=pl.ANY)], [y_init], {6: 0}
        out_rows = y_init.shape[0]
    return pl.pallas_call(
        body,
        out_shape=jax.ShapeDtypeStruct((out_rows or rows, d), F32),
        grid=(rows // tm, fdim // tf),
        in_specs=[
            pl.BlockSpec((tm, d), lambda i, f: (i0 + i, 0)),
            pl.BlockSpec((1, d), lambda i, f: (0, 0)),
            pl.BlockSpec((None, d, tf), lambda i, f: (layer, 0, f)),
            pl.BlockSpec((None, d, tf), lambda i, f: (layer, 0, f)),
            pl.BlockSpec((None, tf, d), lambda i, f: (layer, f, 0)),
            pl.BlockSpec((1, d), lambda i, f: (0, 0)),
        ] + extra_specs,
        out_specs=pl.BlockSpec((tm, d), lambda i, f: (o0 + i, 0)),
        scratch_shapes=[pltpu.VMEM((tm, d), BF16)],
        input_output_aliases=aliases,
        compiler_params=_params("parallel", "arbitrary"),
        name="ffn_half_step",
    )(x, g.reshape(1, d), wg, wu, wd, gf.reshape(1, d), *extra_args)


KV_SLABS = 3
KV_ROW = 4


def _inproj_body(x_ref, g_ref, w_ref, ws_ref, *rest, main_cols, n_prompt_tiles, n_aliased):
    zb_ref, zs_ref, *kv_refs, n_scr = rest[n_aliased:]
    i, j = pl.program_id(0), pl.program_id(1)
    tm, tn = x_ref.shape[0], w_ref.shape[1]
    slab_w = KV_ROW * LANES

    @pl.when(j == 0)
    def _():
        n_scr[...] = _rms_rows(x_ref[...], g_ref[...]).astype(BF16)
        zs_ref[...] = jnp.dot(n_scr[...], ws_ref[...], preferred_element_type=F32)

    z = jnp.dot(n_scr[...], w_ref[...], preferred_element_type=F32)

    @pl.when(j * tn < main_cols)
    def _():
        zb_ref[...] = z

    def scatter(ref, off):
        for c in range(KV_ROW):
            ref[pl.ds(c, tm, stride=KV_ROW), :] = z[:, off + c * LANES:off + (c + 1) * LANES]

    for k in range(KV_SLABS):
        jt, off = divmod(main_cols + k * slab_w, tn)
        pl.when((j == jt) & (i < n_prompt_tiles))(functools.partial(scatter, kv_refs[k], off))
        pl.when((j == jt) & (i >= n_prompt_tiles))(functools.partial(scatter, kv_refs[KV_SLABS + k], off))


def mix_project(x, g, w, w_small, layer, depth, mp, kv_bufs=None):
    m, d = x.shape
    n = w.shape[1]
    tn = _pick_tile(n, 1024)
    ms = m - mp
    tm = _pick_tile(math.gcd(mp, ms), 512)
    main_cols = n - KV_SLABS * KV_ROW * LANES
    n_main = -(-main_cols // tn)
    npt = mp // tm
    out_shape = [jax.ShapeDtypeStruct((m, n_main * tn), F32), jax.ShapeDtypeStruct((m, LANES), F32)]
    out_shape += [jax.ShapeDtypeStruct((depth, mp * KV_ROW, LANES), F32)] * KV_SLABS
    out_shape += [jax.ShapeDtypeStruct((depth, ms * KV_ROW, LANES), F32)] * KV_SLABS
    out_specs = [pl.BlockSpec((tm, tn), lambda i, j: (i, jnp.minimum(j, n_main - 1))),
                 pl.BlockSpec((tm, LANES), lambda i, j: (i, 0))]
    out_specs += [pl.BlockSpec((None, tm * KV_ROW, LANES), lambda i, j: (layer, jnp.minimum(i, npt - 1), 0))] * KV_SLABS
    out_specs += [pl.BlockSpec((None, tm * KV_ROW, LANES), lambda i, j: (layer, jnp.maximum(i - npt, 0), 0))] * KV_SLABS
    in_specs = [
        pl.BlockSpec((tm, d), lambda i, j: (i, 0)),
        pl.BlockSpec((1, d), lambda i, j: (0, 0)),
        pl.BlockSpec((d, tn), lambda i, j: (0, j)),
        pl.BlockSpec((d, LANES), lambda i, j: (0, 0)),
    ]
    args = [x, g.reshape(1, d), w, w_small]
    aliases = {}
    if kv_bufs is not None:
        in_specs += [pl.BlockSpec(memory_space=pl.ANY)] * len(kv_bufs)
        aliases = {len(args) + k: 2 + k for k in range(len(kv_bufs))}
        args += list(kv_bufs)
    zb, zs, *bufs = pl.pallas_call(
        functools.partial(_inproj_body, main_cols=main_cols, n_prompt_tiles=npt, n_aliased=len(aliases)),
        out_shape=out_shape,
        grid=(m // tm, n // tn),
        in_specs=in_specs,
        out_specs=out_specs,
        scratch_shapes=[pltpu.VMEM((tm, d), BF16)],
        input_output_aliases=aliases,
        compiler_params=_params("arbitrary", "arbitrary"),
        name="mix_project",
    )(*args)
    return zb, zs, bufs


def _outproj_body(x_ref, ya_ref, yb_ref, yc_ref, wa_ref, wb_ref, wc_ref, o_ref):
    acc = x_ref[...]
    acc += jnp.dot(ya_ref[...].astype(BF16), wa_ref[...], preferred_element_type=F32)
    acc += jnp.dot(yb_ref[...].astype(BF16), wb_ref[...], preferred_element_type=F32)
    acc += jnp.dot(yc_ref[...].astype(BF16), wc_ref[...], preferred_element_type=F32)
    o_ref[...] = acc


def out_project(x, y_pool, y_mlstm, y_nsa, w_out, layer):
    m, d = x.shape
    da, db, dc = y_pool.shape[1], y_mlstm.shape[1], y_nsa.shape[1]
    assert da == db and dc % da == 0
    tm = _pick_tile(m, 512)
    tn = d
    return pl.pallas_call(
        _outproj_body,
        out_shape=jax.ShapeDtypeStruct((m, d), F32),
        grid=(m // tm, d // tn),
        in_specs=[
            pl.BlockSpec((tm, tn), lambda i, j: (i, j)),
            pl.BlockSpec((tm, da), lambda i, j: (i, 0)),
            pl.BlockSpec((tm, db), lambda i, j: (i, 0)),
            pl.BlockSpec((tm, dc), lambda i, j: (i, 0)),
            pl.BlockSpec((None, da, tn), lambda i, j: (layer, 0, j)),
            pl.BlockSpec((None, db, tn), lambda i, j: (layer, 1, j)),
            pl.BlockSpec((None, dc, tn), lambda i, j: (layer, (da + db) // dc, j)),
        ],
        out_specs=pl.BlockSpec((tm, tn), lambda i, j: (i, j)),
        compiler_params=_params("parallel", "arbitrary"),
        name="out_project",
    )(x, y_pool, y_mlstm, y_nsa, w_out, w_out, w_out)


POOL_HALO = 16


def _pool_group(load, g, n_avail, w_ref, sc_ref):
    w = POOL_WINDOWS[g]
    z = load(0)
    acc = z
    for j in range(1, w):
        acc = acc + load(j)
    d = acc / jnp.minimum(n_avail, w).astype(F32) - z
    lead = d.shape[:-1]
    gd = d.shape[-1]
    y = jnp.dot(d.reshape(-1, gd).astype(BF16), w_ref[g].astype(BF16), preferred_element_type=F32)
    return (y * sc_ref[:, g * gd:(g + 1) * gd]).reshape(*lead, gd)


def _pool_prompt_body(z_ref, w_ref, sc_ref, o_ref, full_scr, *, chunk):
    seq, pd = z_ref.shape
    gd = pd // len(POOL_WINDOWS)
    full_scr[0:POOL_HALO, :] = jnp.zeros((POOL_HALO, pd), F32)
    full_scr[POOL_HALO:POOL_HALO + seq, :] = z_ref[...]
    for c in range(seq // chunk):
        n_avail = c * chunk + 1 + lax.broadcasted_iota(jnp.int32, (chunk, gd), 0)
        for g in range(len(POOL_WINDOWS)):
            load = lambda j: full_scr[pl.ds(POOL_HALO + c * chunk - j, chunk), g * gd:(g + 1) * gd]
            o_ref[c * chunk:(c + 1) * chunk, g * gd:(g + 1) * gd] = _pool_group(load, g, n_avail, w_ref, sc_ref)


def pool_prompt(zb, pool_w, pool_scale, *, batch, seq, col, out_rows=None):
    pd = pool_scale.shape[0]
    chunk = _pick_tile(seq, 256)
    return pl.pallas_call(
        functools.partial(_pool_prompt_body, chunk=chunk),
        out_shape=jax.ShapeDtypeStruct((out_rows or batch * seq, pd), F32),
        grid=(batch,),
        in_specs=[
            pl.BlockSpec((seq, pd), lambda b: (b, col // pd)),
            pl.BlockSpec(pool_w.shape, lambda b: (0, 0, 0)),
            pl.BlockSpec((1, pd), lambda b: (0, 0)),
        ],
        out_specs=pl.BlockSpec((seq, pd), lambda b: (b, 0)),
        scratch_shapes=[pltpu.VMEM((POOL_HALO + seq, pd), F32)],
        compiler_params=_params("parallel"),
        name="pool_prompt",
    )(zb, pool_w, pool_scale.reshape(1, pd))


def _pool_sample_body(full_ref, w_ref, sc_ref, o_ref, *, pos0):
    bt, rows, pd = full_ref.shape
    ts = rows - POOL_HALO
    gd = pd // len(POOL_WINDOWS)
    n_avail = pos0 + 1 + lax.broadcasted_iota(jnp.int32, (bt, ts, gd), 1)
    for g in range(len(POOL_WINDOWS)):
        load = lambda j: full_ref[:, pl.ds(POOL_HALO - j, ts), g * gd:(g + 1) * gd]
        o_ref[:, g * gd:(g + 1) * gd] = _pool_group(load, g, n_avail, w_ref, sc_ref).reshape(bt * ts, gd)


def pool_sample(full, pool_w, pool_scale, y_init, *, pos0, row0):
    batch, rows, pd = full.shape
    ts = rows - POOL_HALO
    bt = math.gcd(batch, 32)
    assert ts % 8 == 0 and row0 % (bt * ts) == 0
    r0 = row0 // (bt * ts)
    return pl.pallas_call(
        functools.partial(_drop_alias_ref, functools.partial(_pool_sample_body, pos0=pos0), 3),
        out_shape=jax.ShapeDtypeStruct(y_init.shape, F32),
        grid=(batch // bt,),
        in_specs=[
            pl.BlockSpec((bt, rows, pd), lambda b: (b, 0, 0)),
            pl.BlockSpec(pool_w.shape, lambda b: (0, 0, 0)),
            pl.BlockSpec((1, pd), lambda b: (0, 0)),
            pl.BlockSpec(memory_space=pl.ANY),
        ],
        out_specs=pl.BlockSpec((bt * ts, pd), lambda b: (r0 + b, 0)),
        input_output_aliases={3: 0},
        compiler_params=_params("parallel"),
        name="pool_sample",
    )(full, pool_w, pool_scale.reshape(1, pd), y_init)


def _log_sigmoid(x):
    return jnp.minimum(x, 0.0) - jnp.log1p(jnp.exp(-jnp.abs(x)))


def _mlstm_body(q_ref, k_ref, v_ref, og_ref, g_ref, bias_ref, gn_ref, c0_ref, n0_ref, m0_ref,
                y_ref, c_ref, n_ref, m_ref, *, L):
    nseq = q_ref.shape[0] // L
    H, D = MLSTM_HEADS, MLSTM_HD
    hi = lax.Precision.HIGHEST

    @pl.when(pl.program_id(1) == 0)
    def _():
        c_ref[...] = c0_ref[...]
        n_ref[...] = n0_ref[...]
        m_ref[...] = m0_ref[...]

    sel = (lax.broadcasted_iota(jnp.int32, (8, LANES), 0) == lax.broadcasted_iota(jnp.int32, (8, LANES), 1)).astype(F32)
    li = lax.broadcasted_iota(jnp.int32, (L, L), 0)
    si = lax.broadcasted_iota(jnp.int32, (L, L), 1)
    causal = li >= si
    lane = lax.broadcasted_iota(jnp.int32, (1, LANES), 1)
    chains = [(s, h) for s in range(nseq) for h in range(H)]

    gates = []
    for s in range(nseq):
        rows = slice(s * L, (s + 1) * L)
        gz = g_ref[rows, :] + bias_ref[...]
        gz_rows = lax.dot_general(sel, gz, NT, precision=hi, preferred_element_type=F32)
        b_cols = jnp.dot(causal.astype(F32), _log_sigmoid(gz), precision=hi, preferred_element_type=F32)
        b_rows = jnp.dot(_log_sigmoid(gz_rows), (li <= si).astype(F32), precision=hi, preferred_element_type=F32)
        gates.append((gz, gz_rows, b_cols, b_rows, m_ref[s]))

    qk, qc, state = {}, {}, {}
    for s, h in chains:
        rows, cols = slice(s * L, (s + 1) * L), slice(h * D, (h + 1) * D)
        qh = q_ref[rows, cols]
        kh = k_ref[rows, cols] * (D ** -0.5)
        ch = c_ref[s, h]
        nh = n_ref[s, h:h + 1, :]
        qb, kb = qh.astype(BF16), kh.astype(BF16)
        qk[s, h] = lax.dot_general(qb, kb, NT, preferred_element_type=F32)
        qc[s, h] = lax.dot_general(qb, ch.astype(BF16), NT, preferred_element_type=F32)
        state[s, h] = (qh, kh, kb, ch, nh)

    sm, stats = {}, {}
    for s, h in chains:
        gz, gz_rows, b_cols, b_rows, m_all = gates[s]
        bc = b_cols[:, H + h:H + h + 1]
        ic = gz[:, h:h + 1]
        br = b_rows[H + h:H + h + 1, :]
        ir = gz_rows[h:h + 1, :]
        m_prev = m_all[:, h:h + 1]
        dmat = jnp.where(causal, bc - br + ir, NEG_MASK)
        inter = bc + m_prev
        m_t = jnp.maximum(inter, jnp.max(dmat, axis=1, keepdims=True))
        sm[s, h] = qk[s, h] * jnp.exp(dmat - m_t)
        m_new = m_t[L - 1:L, :]
        b_last = bc[L - 1:L, :]
        stats[s, h] = (jnp.exp(inter - m_t), m_t, m_new, jnp.exp(b_last + m_prev - m_new),
                       jnp.exp(b_last - bc + ic - m_new))

    num, c_new = {}, {}
    for s, h in chains:
        rows, cols = slice(s * L, (s + 1) * L), slice(h * D, (h + 1) * D)
        a_inter, _, _, decay, w_col = stats[s, h]
        qh, kh, kb, ch, nh = state[s, h]
        vh = v_ref[rows, cols]
        num[s, h] = jnp.dot(sm[s, h].astype(BF16), vh.astype(BF16), preferred_element_type=F32) + a_inter * qc[s, h]
        c_new[s, h] = decay * ch + lax.dot_general((vh * w_col).astype(BF16), kb, TN, preferred_element_type=F32)

    m_out = [gates[s][4] for s in range(nseq)]
    for s, h in chains:
        rows, cols = slice(s * L, (s + 1) * L), slice(h * D, (h + 1) * D)
        a_inter, m_t, m_new, decay, w_col = stats[s, h]
        qh, kh, kb, ch, nh = state[s, h]
        den = jnp.sum(sm[s, h], axis=1, keepdims=True) + a_inter * jnp.sum(qh * nh, axis=1, keepdims=True)
        den = jnp.maximum(jnp.abs(den), jnp.exp(-m_t))
        hh = num[s, h] / den
        mu = jnp.mean(hh, axis=1, keepdims=True)
        var = jnp.mean(jnp.square(hh - mu), axis=1, keepdims=True)
        hn = (hh - mu) * lax.rsqrt(var + EPS) * gn_ref[:, cols]
        y_ref[rows, cols] = jax.nn.sigmoid(og_ref[rows, cols]) * hn
        c_ref[s, h] = c_new[s, h]
        n_ref[s, h:h + 1, :] = decay * nh + jnp.sum(kh * w_col, axis=0, keepdims=True)
        m_out[s] = jnp.where(lane == h, m_new, m_out[s])
    for s in range(nseq):
        m_ref[s] = m_out[s]


def mlstm_mix(zb, zs, if_bias, mnorm, c0, n0, m0, *, row0, batch, seq, col_q, y_init=None, out_rows=None):
    H, D = MLSTM_HEADS, MLSTM_HD
    dim = H * D
    L = math.gcd(seq, MLSTM_CHUNK)
    nc = seq // L
    nseq = math.gcd(batch, min(4, MLSTM_CHUNK // L)) if nc == 1 else 1
    rows = nseq * L
    assert L % 8 == 0 and row0 % rows == 0 and col_q % dim == 0
    r0 = row0 // rows
    cq = col_q // dim
    bias = jnp.pad(if_bias, (0, LANES - if_bias.shape[0])).reshape(1, LANES)
    m0p = jnp.pad(m0, ((0, 0), (0, LANES - H))).reshape(batch, 1, LANES)
    row = lambda b, c: r0 + b * nc + c
    y_rows = (out_rows or batch * seq) if y_init is None else y_init.shape[0]
    y_r0 = 0 if y_init is None else r0
    in_specs = [
        pl.BlockSpec((rows, dim), lambda b, c: (row(b, c), cq)),
        pl.BlockSpec((rows, dim), lambda b, c: (row(b, c), cq + 1)),
        pl.BlockSpec((rows, dim), lambda b, c: (row(b, c), cq + 2)),
        pl.BlockSpec((rows, dim), lambda b, c: (row(b, c), cq + 3)),
        pl.BlockSpec((rows, LANES), lambda b, c: (row(b, c), 0)),
        pl.BlockSpec((1, LANES), lambda b, c: (0, 0)),
        pl.BlockSpec((1, dim), lambda b, c: (0, 0)),
        pl.BlockSpec((nseq, H, D, D), lambda b, c: (b, 0, 0, 0)),
        pl.BlockSpec((nseq, H, D), lambda b, c: (b, 0, 0)),
        pl.BlockSpec((nseq, 1, LANES), lambda b, c: (b, 0, 0)),
    ]
    args = [zb, zb, zb, zb, zs, bias, mnorm.reshape(1, dim), c0, n0, m0p]
    aliases = {}
    body = functools.partial(_mlstm_body, L=L)
    if y_init is not None:
        in_specs.append(pl.BlockSpec(memory_space=pl.ANY))
        args.append(y_init)
        aliases = {len(args) - 1: 0}
        body = functools.partial(_drop_alias_ref, body, len(args) - 1)
    y, c_out, n_out, m_out = pl.pallas_call(
        body,
        out_shape=(jax.ShapeDtypeStruct((y_rows, dim), F32),
                   jax.ShapeDtypeStruct((batch, H, D, D), F32),
                   jax.ShapeDtypeStruct((batch, H, D), F32),
                   jax.ShapeDtypeStruct((batch, 1, LANES), F32)),
        grid=(batch // nseq, nc),
        in_specs=in_specs,
        out_specs=(
            pl.BlockSpec((rows, dim), lambda b, c: (y_r0 + b * nc + c, 0)),
            pl.BlockSpec((nseq, H, D, D), lambda b, c: (b, 0, 0, 0)),
            pl.BlockSpec((nseq, H, D), lambda b, c: (b, 0, 0)),
            pl.BlockSpec((nseq, 1, LANES), lambda b, c: (b, 0, 0)),
        ),
        input_output_aliases=aliases,
        compiler_params=_params("parallel", "arbitrary"),
        name="mlstm_mix",
    )(*args)
    return y, c_out, n_out, m_out[:, 0, :H]


KEY_TILE = 128
KEY_CHUNK = 256
SEL_COLS = 64
POS_HI, POS_LO = SEL_COLS, SEL_COLS + 1
NEG_SEL = -1.0e9


def _slope(h):
    return 2.0 ** (-(8.0 / NSA_HEADS) * (h + 1))


def _key_features(pos, onehot):
    lane = lax.broadcasted_iota(jnp.int32, pos.shape, 1)
    hi = lax.shift_right_logical(pos, 6)
    lo = jnp.bitwise_and(pos, SEL_BLOCK - 1)
    f = jnp.where(lane == POS_HI, hi.astype(F32), jnp.where(lane == POS_LO, lo.astype(F32), 0.0))
    if onehot:
        f = jnp.where(lane == hi, 1.0, f)
    return f


def _query_features(shape, h):
    lane = lax.broadcasted_iota(jnp.int32, shape, 1)
    return jnp.where(lane == POS_HI, SEL_BLOCK * _slope(h), jnp.where(lane == POS_LO, _slope(h), 0.0))


def _compress_block_rows(load, pw_ref, kv, g):
    cols = slice(g * NSA_HD, (g + 1) * NSA_HD)
    a0 = a1 = None
    for j in range(CMP_STRIDE):
        rows = load(j)
        t0 = rows * pw_ref[kv, j:j + 1, cols]
        t1 = rows * pw_ref[kv, CMP_STRIDE + j:CMP_STRIDE + j + 1, cols]
        a0 = t0 if a0 is None else a0 + t0
        a1 = t1 if a1 is None else a1 + t1
    return a0, a1


def _finish_compress(acc, proj_ref, kcmp_aug, vcmp, kv, g, n_ch, ncp, v_transposed):
    c = jnp.dot(acc.astype(BF16), proj_ref[kv, g].astype(BF16), preferred_element_type=F32)
    if kv == 0:
        kcmp_aug[g, 0:n_ch, 0:NSA_HD] = c.astype(BF16)
        n = lax.broadcasted_iota(jnp.int32, (ncp, LANES), 0)
        kcmp_aug[g, :, NSA_HD:2 * NSA_HD] = _key_features(n * CMP_STRIDE + (CMP_LEN - 1), False).astype(BF16)
    elif v_transposed:
        if ncp > n_ch:
            c = jnp.concatenate([c, jnp.zeros((ncp - n_ch, NSA_HD), F32)], axis=0)
        for blk in range(ncp // LANES):
            vcmp[g, :, blk * LANES:(blk + 1) * LANES] = c[blk * LANES:(blk + 1) * LANES].T.astype(BF16)
    else:
        vcmp[g, 0:n_ch, :] = c.astype(BF16)


def _masked_softmax(s, mask):
    s = jnp.where(mask, s, NEG_MASK)
    m = jnp.max(s, axis=1, keepdims=True)
    e = jnp.where(mask, jnp.exp(s - m), 0.0)
    return e / jnp.maximum(jnp.sum(e, axis=1, keepdims=True), 1e-30)


def _select_blocks(psum, t0, n_cmp, n_sel, queries_on_lanes):
    ncp = psum.shape[0] if queries_on_lanes else psum.shape[1]
    nsp = -(-n_sel // 8) * 8
    j = lax.broadcasted_iota(jnp.int32, (nsp, ncp), 0)
    n = lax.broadcasted_iota(jnp.int32, (nsp, ncp), 1)
    cover = ((n * CMP_STRIDE < j * SEL_BLOCK + SEL_BLOCK) & (n * CMP_STRIDE + CMP_LEN > j * SEL_BLOCK)
             & (n < n_cmp)).astype(F32)
    if queries_on_lanes:
        imp = jnp.dot(cover, psum, precision=lax.Precision.HIGHEST, preferred_element_type=F32)
    else:
        imp = lax.dot_general(cover, psum, NT, precision=lax.Precision.HIGHEST, preferred_element_type=F32)
    jq = lax.broadcasted_iota(jnp.int32, (nsp, LANES), 0)
    t = t0 + lax.broadcasted_iota(jnp.int32, (nsp, LANES), 1)
    cur = lax.shift_right_logical(t, 6)
    forced = (jq == 0) | (jq == cur) | (jq == cur - 1)
    valid = (jq * SEL_BLOCK <= t) & (jq < n_sel)
    score = jnp.where(valid, imp + jnp.where(forced, FORCE_BONUS, 0.0), -jnp.inf)
    jf = jq.astype(F32)
    sel = jnp.zeros((nsp, LANES), F32)
    for _ in range(min(SEL_TOPN, n_sel)):
        mx = jnp.max(score, axis=0, keepdims=True)
        first = jnp.min(jnp.where(score == mx, jf, 1.0e9), axis=0, keepdims=True)
        pick = jf == first
        sel = jnp.where(pick, 1.0, sel)
        score = jnp.where(pick, -jnp.inf, score)
    bias = jnp.where((sel > 0.5) | (jq >= n_sel), 0.0, NEG_SEL)
    bias = jnp.concatenate([bias, jnp.zeros((LANES - nsp, LANES), F32)], axis=0)
    return bias.T


def _online_step_t(state, s_t, v_t):
    m, l, acc = state
    m_new = jnp.maximum(m, jnp.max(s_t, axis=0, keepdims=True))
    alpha = jnp.exp(m - m_new)
    p = jnp.exp(s_t - m_new)
    l = alpha * l + jnp.sum(p, axis=0, keepdims=True)
    acc = alpha * acc + jnp.dot(v_t, p.astype(BF16), preferred_element_type=F32)
    return m_new, l, acc


def _stack_heads(q_heads, feats):
    return jnp.concatenate(
        [jnp.concatenate([q, f.astype(BF16)], axis=1) for q, f in zip(q_heads, feats)], axis=0)


def _write_gated(o_ref, gates, g, rows, o_cmp, o_s, o_w):
    for r in range(NSA_GROUP):
        h = g * NSA_GROUP + r
        sl = slice(r * rows, (r + 1) * rows)
        c = 8 + h
        o = gates[:, c:c + 1] * o_cmp[sl] + gates[:, c + 8:c + 9] * o_s[sl] + gates[:, c + 16:c + 17] * o_w[sl]
        o_ref[:, h * NSA_HD:(h + 1) * NSA_HD] = o


def _nsa_prompt_body(q_ref, kc_ref, ks_ref, kw_ref, gate_ref, pw_ref, proj_ref, o_ref,
                     ks_aug, vs_t, kw_aug, vw_t, kcmp_aug, vcmp_t, a1_scr, *, seq):
    i = pl.program_id(1)
    tq = KEY_TILE
    n_ch = seq // CMP_STRIDE
    n_cmp = n_ch - 1
    n_sel = seq // SEL_BLOCK
    ncp = kcmp_aug.shape[1]
    G, HD = NSA_GROUP, NSA_HD
    Q = G * tq

    @pl.when(i == 0)
    def _build():
        pos = lax.broadcasted_iota(jnp.int32, (seq, LANES), 0)
        f_sel = _key_features(pos, True).astype(BF16)
        f_win = _key_features(pos, False).astype(BF16)
        for g in range(NSA_KV_HEADS):
            ks_aug[g, :, 0:HD] = ks_ref[pl.ds(g, seq, stride=KV_ROW), :].astype(BF16)
            ks_aug[g, :, HD:2 * HD] = f_sel
            kw_aug[g, :, 0:HD] = kw_ref[pl.ds(g, seq, stride=KV_ROW), :].astype(BF16)
            kw_aug[g, :, HD:2 * HD] = f_win
            for kt in range(seq // KEY_TILE):
                rows = slice(kt * KEY_TILE, (kt + 1) * KEY_TILE)
                src_rows = pl.ds(kt * KEY_TILE * KV_ROW + 2 + g, KEY_TILE, stride=KV_ROW)
                vs_t[g, :, rows] = ks_ref[src_rows, :].T.astype(BF16)
                vw_t[g, :, rows] = kw_ref[src_rows, :].T.astype(BF16)
        kcmp_aug[...] = jnp.zeros(kcmp_aug.shape, BF16)
        a1_scr[n_ch:n_ch + 8, :] = jnp.zeros((8, HD), F32)
        for kv in range(2):
            for g in range(NSA_KV_HEADS):
                c = kv * NSA_KV_HEADS + g
                a0, a1 = _compress_block_rows(
                    lambda j: kc_ref[pl.ds(j * KV_ROW + c, n_ch, stride=KV_ROW * CMP_STRIDE), :], pw_ref, kv, g)
                a1_scr[0:n_ch, :] = a1
                _finish_compress(a0 + a1_scr[pl.ds(1, n_ch), :], proj_ref, kcmp_aug, vcmp_t, kv, g, n_ch, ncp, True)

    t0 = i * tq
    key = lax.broadcasted_iota(jnp.int32, (KEY_CHUNK, Q), 0)
    t_cols = t0 + jnp.bitwise_and(lax.broadcasted_iota(jnp.int32, (KEY_CHUNK, Q), 1), tq - 1)
    gates_t = jax.nn.sigmoid(gate_ref[...]).T
    scale = HD ** -0.5

    n = lax.broadcasted_iota(jnp.int32, (ncp, Q), 0)
    t_c = t0 + jnp.bitwise_and(lax.broadcasted_iota(jnp.int32, (ncp, Q), 1), tq - 1)
    mask = (n * CMP_STRIDE + (CMP_LEN - 1) <= t_c) & (n < n_cmp)
    kv_groups = range(NSA_KV_HEADS)
    q_heads = [[(q_ref[:, (g * G + r) * HD:(g * G + r + 1) * HD] * scale).astype(BF16) for r in range(G)]
               for g in kv_groups]
    feats = [[_query_features((tq, LANES), g * G + r) for r in range(G)] for g in kv_groups]
    q_plain = [_stack_heads(q_heads[g], feats[g]) for g in kv_groups]
    s_c = [jnp.where(mask, lax.dot_general(kcmp_aug[g], q_plain[g], NT, preferred_element_type=F32), NEG_MASK)
           for g in kv_groups]
    e_c = [jnp.where(mask, jnp.exp(s - jnp.max(s, axis=0, keepdims=True)), 0.0) for s in s_c]
    p_c = [e / jnp.maximum(jnp.sum(e, axis=0, keepdims=True), 1e-30) for e in e_c]
    o_cmp = [jnp.dot(vcmp_t[g], p_c[g].astype(BF16), preferred_element_type=F32) for g in kv_groups]
    psum = [p[:, 0:tq] + p[:, tq:2 * tq] + p[:, 2 * tq:3 * tq] + p[:, 3 * tq:4 * tq] for p in p_c]
    bias = [_select_blocks(psum[g], t0, n_cmp, n_sel, True) for g in kv_groups]
    q_sel = [_stack_heads(q_heads[g], [f + bias[g] for f in feats[g]]) for g in kv_groups]

    last = t0 // KEY_CHUNK
    first_w = jnp.maximum(t0 - WINDOW, 0) // KEY_CHUNK
    init = (jnp.full((1, Q), NEG_MASK, F32), jnp.zeros((1, Q), F32), jnp.zeros((HD, Q), F32))

    def scores(c, k_aug, q, g, valid):
        off = pl.multiple_of(c * KEY_CHUNK, KEY_CHUNK)
        s_t = lax.dot_general(k_aug[g, pl.ds(off, KEY_CHUNK), :], q, NT, preferred_element_type=F32)
        return s_t if valid is None else jnp.where(valid(off + key), s_t, NEG_MASK)

    def values(c, v_t, g):
        return v_t[g, :, pl.ds(pl.multiple_of(c * KEY_CHUNK, KEY_CHUNK), KEY_CHUNK)]

    def causal(kpos):
        return kpos <= t_cols

    def band(kpos):
        return (kpos <= t_cols) & (t_cols - kpos <= WINDOW)

    def early(c, sel):
        s = [scores(c, ks_aug, q_sel[g], g, None) for g in kv_groups]
        return tuple(_online_step_t(sel[g], s[g], values(c, vs_t, g)) for g in kv_groups)

    def late(c, sts):
        sel, win = sts
        s_sel = [scores(c, ks_aug, q_sel[g], g, causal) for g in kv_groups]
        s_win = [scores(c, kw_aug, q_plain[g], g, band) for g in kv_groups]
        sel = tuple(_online_step_t(sel[g], s_sel[g], values(c, vs_t, g)) for g in kv_groups)
        win = tuple(_online_step_t(win[g], s_win[g], values(c, vw_t, g)) for g in kv_groups)
        return sel, win

    inits = (init,) * NSA_KV_HEADS
    sel = lax.fori_loop(0, first_w, early, inits)
    sel, win = lax.fori_loop(first_w, last + 1, late, (sel, inits))

    for g in kv_groups:
        o_s = sel[g][2] / sel[g][1]
        o_w = win[g][2] / win[g][1]
        for r in range(G):
            h = g * G + r
            cols = slice(r * tq, (r + 1) * tq)
            c = 8 + h
            o_t = (gates_t[c:c + 1, :] * o_cmp[g][:, cols] + gates_t[c + 8:c + 9, :] * o_s[:, cols]
                   + gates_t[c + 16:c + 17, :] * o_w[:, cols])
            o_ref[:, h * HD:(h + 1) * HD] = o_t.T


def nsa_prompt(zb, zs, kv_cmp, kv_slc, kv_win, cmp_pos_w, cmp_proj, *, layer, batch, seq, col_q, out_rows=None):
    assert seq % KEY_CHUNK == 0 and seq // SEL_BLOCK <= SEL_COLS
    nq = seq // KEY_TILE
    n_ch = seq // CMP_STRIDE
    ncp = -(-n_ch // LANES) * LANES
    qw = NSA_HEADS * NSA_HD
    pw = cmp_pos_w.reshape(2, CMP_LEN, 2 * NSA_HD)
    return pl.pallas_call(
        functools.partial(_nsa_prompt_body, seq=seq),
        out_shape=jax.ShapeDtypeStruct((out_rows or batch * seq, qw), F32),
        grid=(batch, nq),
        in_specs=[
            pl.BlockSpec((KEY_TILE, qw), lambda b, i: (b * nq + i, col_q // qw)),
            pl.BlockSpec((None, seq * KV_ROW, NSA_HD), lambda b, i: (layer, b, 0)),
            pl.BlockSpec((None, seq * KV_ROW, NSA_HD), lambda b, i: (layer, b, 0)),
            pl.BlockSpec((None, seq * KV_ROW, NSA_HD), lambda b, i: (layer, b, 0)),
            pl.BlockSpec((KEY_TILE, LANES), lambda b, i: (b * nq + i, 0)),
            pl.BlockSpec((2, CMP_LEN, 2 * NSA_HD), lambda b, i: (0, 0, 0)),
            pl.BlockSpec((2, NSA_KV_HEADS, NSA_HD, NSA_HD), lambda b, i: (0, 0, 0, 0)),
        ],
        out_specs=pl.BlockSpec((KEY_TILE, qw), lambda b, i: (b * nq + i, 0)),
        scratch_shapes=[
            pltpu.VMEM((NSA_KV_HEADS, seq, 2 * NSA_HD), BF16),
            pltpu.VMEM((NSA_KV_HEADS, NSA_HD, seq), BF16),
            pltpu.VMEM((NSA_KV_HEADS, seq, 2 * NSA_HD), BF16),
            pltpu.VMEM((NSA_KV_HEADS, NSA_HD, seq), BF16),
            pltpu.VMEM((NSA_KV_HEADS, ncp, 2 * NSA_HD), BF16),
            pltpu.VMEM((NSA_KV_HEADS, NSA_HD, ncp), BF16),
            pltpu.VMEM((n_ch + 8, NSA_HD), F32),
        ],
        compiler_params=_params("parallel", "arbitrary"),
        name="nsa_prompt",
    )(zb, kv_cmp, kv_slc, kv_win, zs, pw, cmp_proj)


def _nsa_sample_body(pt_ref, q_ref, ksn_ref, kwn_ref, gate_ref, wprev_ref, pw_ref, proj_ref, *rest,
                     ts, past, n_pages):
    del pt_ref
    cmp_pages = rest[:n_pages]
    slc_pages = rest[n_pages:2 * n_pages]
    o_ref = rest[2 * n_pages + 1]
    ks_aug, vs, kw_aug, vw, kcmp_aug, vcmp, acc_scr = rest[2 * n_pages + 2:]
    G, HD = NSA_GROUP, NSA_HD
    page = slc_pages[0].shape[0] // 4
    kp = ks_aug.shape[1]
    wprev = wprev_ref.shape[0] // 4
    wp = kw_aug.shape[1]
    win_pos0 = past - wprev
    n_ch = (past + ts) // CMP_STRIDE
    n_cmp = n_ch - 1
    n_sel = -(-(past + ts) // SEL_BLOCK)
    ncp = kcmp_aug.shape[1]
    ch_per_page = page // CMP_STRIDE

    @pl.when(pl.program_id(0) == 0)
    def _constants():
        pos = lax.broadcasted_iota(jnp.int32, (kp, LANES), 0)
        f_sel = _key_features(pos, True).astype(BF16)
        posw = win_pos0 + lax.broadcasted_iota(jnp.int32, (wp, LANES), 0)
        f_win = _key_features(posw, False).astype(BF16)
        for g in range(NSA_KV_HEADS):
            ks_aug[g, :, HD:2 * HD] = f_sel
            kw_aug[g, :, HD:2 * HD] = f_win
        kcmp_aug[...] = jnp.zeros(kcmp_aug.shape, BF16)
        vcmp[...] = jnp.zeros(vcmp.shape, BF16)

    def with_tail(new_rows):
        return jnp.concatenate([new_rows, jnp.zeros((KEY_TILE - ts, HD), F32)], axis=0).astype(BF16)

    for g in range(NSA_KV_HEADS):
        for p in range(n_pages):
            rows = slice(p * page, (p + 1) * page)
            ks_aug[g, rows, 0:HD] = slc_pages[p][pl.ds(g, page, stride=4), :].astype(BF16)
            vs[g, rows, :] = slc_pages[p][pl.ds(2 + g, page, stride=4), :].astype(BF16)
        ks_aug[g, past:past + KEY_TILE, 0:HD] = with_tail(ksn_ref[pl.ds(g, ts, stride=KV_ROW), :])
        vs[g, past:past + KEY_TILE, :] = with_tail(ksn_ref[pl.ds(2 + g, ts, stride=KV_ROW), :])
        kw_aug[g, 0:wprev, 0:HD] = wprev_ref[pl.ds(g, wprev, stride=4), :].astype(BF16)
        vw[g, 0:wprev, :] = wprev_ref[pl.ds(2 + g, wprev, stride=4), :].astype(BF16)
        kw_aug[g, wprev:wprev + KEY_TILE, 0:HD] = with_tail(kwn_ref[pl.ds(g, ts, stride=KV_ROW), :])
        vw[g, wprev:wprev + KEY_TILE, :] = with_tail(kwn_ref[pl.ds(2 + g, ts, stride=KV_ROW), :])

    for kv in range(2):
        for g in range(NSA_KV_HEADS):
            c = kv * NSA_KV_HEADS + g
            cols = slice(g * HD, (g + 1) * HD)
            w0 = jnp.concatenate([pw_ref[kv, 0:CMP_STRIDE, cols]] * ch_per_page, axis=0)
            w1 = jnp.concatenate([pw_ref[kv, CMP_STRIDE:CMP_LEN, cols]] * ch_per_page, axis=0)
            for p in range(n_pages):
                x = cmp_pages[p][pl.ds(c, page, stride=4), :]
                if p + 1 < n_pages:
                    nxt = cmp_pages[p + 1][pl.ds(c, CMP_STRIDE, stride=4), :] * w1[0:CMP_STRIDE]
                else:
                    nxt = jnp.zeros((CMP_STRIDE, HD), F32)
                z = x * w0 + jnp.concatenate([(x * w1)[CMP_STRIDE:], nxt], axis=0)
                acc_scr[c, p * ch_per_page:(p + 1) * ch_per_page, :] = jnp.sum(
                    z.reshape(ch_per_page, CMP_STRIDE, HD), axis=1)
            _finish_compress(acc_scr[c, 0:n_ch, :], proj_ref, kcmp_aug, vcmp, kv, g, n_ch, ncp, False)

    R = G * ts
    gates = jax.nn.sigmoid(gate_ref[...])
    scale = HD ** -0.5

    def t_of(shape):
        return past + jnp.bitwise_and(lax.broadcasted_iota(jnp.int32, shape, 0), ts - 1)

    def softmax_pv(s, v):
        m = jnp.max(s, axis=1, keepdims=True)
        e = jnp.exp(s - m)
        return jnp.dot(e.astype(BF16), v, preferred_element_type=F32) / jnp.sum(e, axis=1, keepdims=True)

    kv_groups = range(NSA_KV_HEADS)
    q_heads = [[(q_ref[:, (g * G + r) * HD:(g * G + r + 1) * HD] * scale).astype(BF16) for r in range(G)]
               for g in kv_groups]
    feats = [[_query_features((ts, LANES), g * G + r) for r in range(G)] for g in kv_groups]
    q_plain = [_stack_heads(q_heads[g], feats[g]) for g in kv_groups]

    idx = lax.broadcasted_iota(jnp.int32, (R, wp), 1)
    dist = t_of((R, wp)) - (win_pos0 + idx)
    win_ok = (idx < wprev + ts) & (dist >= 0) & (dist <= WINDOW)
    s_win = [jnp.where(win_ok, lax.dot_general(q_plain[g], kw_aug[g], NT, preferred_element_type=F32), NEG_MASK)
             for g in kv_groups]
    n = lax.broadcasted_iota(jnp.int32, (R, ncp), 1)
    cmp_ok = (n * CMP_STRIDE + (CMP_LEN - 1) <= t_of((R, ncp))) & (n < n_cmp)
    s_cmp = [lax.dot_general(q_plain[g], kcmp_aug[g], NT, preferred_element_type=F32) for g in kv_groups]
    p_c = [_masked_softmax(s, cmp_ok) for s in s_cmp]
    o_cmp = [jnp.dot(p_c[g].astype(BF16), vcmp[g], preferred_element_type=F32) for g in kv_groups]
    o_w = [softmax_pv(s_win[g], vw[g]) for g in kv_groups]
    psum = [jnp.concatenate([p[0:ts] + p[ts:2 * ts] + p[2 * ts:3 * ts] + p[3 * ts:4 * ts],
                             jnp.zeros((LANES - ts, ncp), F32)], axis=0) for p in p_c]
    bias = [_select_blocks(psum[g], past, n_cmp, n_sel, False)[0:ts] for g in kv_groups]
    q_sel = [_stack_heads(q_heads[g], [f + bias[g] for f in feats[g]]) for g in kv_groups]
    sel_ok = lax.broadcasted_iota(jnp.int32, (R, kp), 1) <= t_of((R, kp))
    s_sel = [jnp.where(sel_ok, lax.dot_general(q_sel[g], ks_aug[g], NT, preferred_element_type=F32), NEG_MASK)
             for g in kv_groups]
    o_s = [softmax_pv(s_sel[g], vs[g]) for g in kv_groups]
    for g in kv_groups:
        _write_gated(o_ref, gates, g, ts, o_cmp[g], o_s[g], o_w[g])


def nsa_sample(zb, zs, kv_slc, kv_win, cache_cmp, cache_slc, state_win, page_table, cmp_pos_w, cmp_proj, y_init, *,
               layer, row0, batch, ts, col_q):
    depth, n_pool, page = cache_cmp.shape[:3]
    n_pages = page_table.shape[1]
    past = n_pages * page
    wprev = state_win.shape[2]
    assert ts & (ts - 1) == 0 and ts <= KEY_TILE and row0 % ts == 0
    assert past % KEY_TILE == 0 and (past + ts) // CMP_STRIDE == past // CMP_STRIDE
    assert page % CMP_STRIDE == 0 and -(-(past + ts) // SEL_BLOCK) <= SEL_COLS and wprev % 16 == 0
    qw = NSA_HEADS * NSA_HD
    n_ch = past // CMP_STRIDE
    ncp = -(-n_ch // LANES) * LANES
    r0 = row0 // ts
    pw = cmp_pos_w.reshape(2, CMP_LEN, 2 * NSA_HD)
    cmp_view = cache_cmp.reshape(depth, n_pool, page * 4, NSA_HD)
    slc_view = cache_slc.reshape(depth, n_pool, page * 4, NSA_HD)
    win_view = state_win.reshape(depth, batch, wprev * 4, NSA_HD)

    def page_map(p):
        return lambda b, pt: (layer, pt[b * n_pages + p], 0, 0)

    in_specs = [
        pl.BlockSpec((ts, qw), lambda b, pt: (r0 + b, col_q // qw)),
        pl.BlockSpec((None, ts * KV_ROW, NSA_HD), lambda b, pt: (layer, b, 0)),
        pl.BlockSpec((None, ts * KV_ROW, NSA_HD), lambda b, pt: (layer, b, 0)),
        pl.BlockSpec((ts, LANES), lambda b, pt: (r0 + b, 0)),
        pl.BlockSpec((None, None, wprev * 4, NSA_HD), lambda b, pt: (layer, b, 0, 0)),
        pl.BlockSpec((2, CMP_LEN, 2 * NSA_HD), lambda b, pt: (0, 0, 0)),
        pl.BlockSpec((2, NSA_KV_HEADS, NSA_HD, NSA_HD), lambda b, pt: (0, 0, 0, 0)),
    ]
    in_specs += [pl.BlockSpec((None, None, page * 4, NSA_HD), page_map(p % n_pages)) for p in range(2 * n_pages)]
    in_specs.append(pl.BlockSpec(memory_space=pl.ANY))
    return pl.pallas_call(
        functools.partial(_nsa_sample_body, ts=ts, past=past, n_pages=n_pages),
        out_shape=jax.ShapeDtypeStruct(y_init.shape, F32),
        grid_spec=pltpu.PrefetchScalarGridSpec(
            num_scalar_prefetch=1,
            grid=(batch,),
            in_specs=in_specs,
            out_specs=pl.BlockSpec((ts, qw), lambda b, pt: (r0 + b, 0)),
            scratch_shapes=[
                pltpu.VMEM((NSA_KV_HEADS, past + KEY_TILE, 2 * NSA_HD), BF16),
                pltpu.VMEM((NSA_KV_HEADS, past + KEY_TILE, NSA_HD), BF16),
                pltpu.VMEM((NSA_KV_HEADS, wprev + KEY_TILE, 2 * NSA_HD), BF16),
                pltpu.VMEM((NSA_KV_HEADS, wprev + KEY_TILE, NSA_HD), BF16),
                pltpu.VMEM((NSA_KV_HEADS, ncp, 2 * NSA_HD), BF16),
                pltpu.VMEM((NSA_KV_HEADS, ncp, NSA_HD), BF16),
                pltpu.VMEM((2 * NSA_KV_HEADS, n_ch, NSA_HD), F32),
            ],
        ),
        input_output_aliases={8 + 2 * n_pages: 0},
        compiler_params=_params("arbitrary"),
        name="nsa_sample",
    )(page_table.reshape(-1), zb, kv_slc, kv_win, zs, win_view, pw, cmp_proj,
      *([cmp_view] * n_pages), *([slc_view] * n_pages), y_init)


def _window_state_body(old_ref, new_ref, o_ref):
    keep = o_ref.shape[1] - new_ref.shape[1]
    o_ref[:, 0:keep, :] = old_ref[:, old_ref.shape[1] - keep:, :]
    o_ref[:, keep:, :] = new_ref[...]


def window_state(state_win, kv_win_new, ts):
    depth, batch, w_old = state_win.shape[:3]
    keep = min(WINDOW, w_old + ts) - ts
    assert keep >= 0 and (keep * KV_ROW) % 8 == 0 and (ts * KV_ROW) % 8 == 0
    bt = math.gcd(batch, 4)
    out = pl.pallas_call(
        _window_state_body,
        out_shape=jax.ShapeDtypeStruct((depth, batch, (keep + ts) * KV_ROW, NSA_HD), F32),
        grid=(depth, batch // bt),
        in_specs=[
            pl.BlockSpec((None, bt, w_old * KV_ROW, NSA_HD), lambda l, b: (l, b, 0, 0)),
            pl.BlockSpec((None, bt, ts * KV_ROW, NSA_HD), lambda l, b: (l, b, 0, 0)),
        ],
        out_specs=pl.BlockSpec((None, bt, (keep + ts) * KV_ROW, NSA_HD), lambda l, b: (l, b, 0, 0)),
        compiler_params=_params("parallel", "parallel"),
        name="window_state",
    )(state_win.reshape(depth, batch, w_old * KV_ROW, NSA_HD), kv_win_new.reshape(depth, batch, ts * KV_ROW, NSA_HD))
    return out.reshape(depth, batch, keep + ts, *state_win.shape[3:])


ZB_NQ, ZB_POOL, ZB_MQ, ZB_MK, ZB_MV, ZB_MO, ZB_END = (0, 1024, 1536, 2048, 2560, 3072, 3584)
W_POOL, W_MI, W_NQ, W_CMP, W_NG, W_END = 0, 2560, 2568, 3592, 5128, 5152


def _split_w_in(w_in_l):
    w = w_in_l.astype(BF16)
    big = jnp.concatenate([w[:, W_NQ:W_CMP], w[:, W_POOL:W_MI], w[:, W_CMP:W_NG]], axis=1)
    small = jnp.concatenate([w[:, W_MI:W_NQ], w[:, W_NG:W_END]], axis=1)
    return big, jnp.pad(small, ((0, 0), (0, LANES - small.shape[1])))


def kernel(x_prompt, x_sample, cache_kv_cmp, cache_kv_slc, state_kv_win, state_pool, state_mlstm_C, state_mlstm_n, state_mlstm_m, page_table, ffn1_norm, ffn1_w_gate, ffn1_w_up, ffn1_w_down, mix_norm, w_in, w_out, pool_w, pool_scale, mlstm_if_bias, mlstm_norm, nsa_cmp_pos_w, nsa_cmp_proj, ffn2_norm, ffn2_w_gate, ffn2_w_up, ffn2_w_down, final_norm):
    bp, tp, d = x_prompt.shape
    bs, ts, _ = x_sample.shape
    depth = w_in.shape[0]
    mp, ms = bp * tp, bs * ts
    m_all = mp + ms
    past_len = page_table.shape[1] * cache_kv_cmp.shape[2]
    pd = pool_scale.shape[1]
    kv_row = (2, NSA_KV_HEADS, NSA_HD)
    zeros = lambda *s: jnp.zeros(s, F32)
    ffn1 = [w.astype(BF16) for w in (ffn1_w_gate, ffn1_w_up, ffn1_w_down)]
    ffn2 = [w.astype(BF16) for w in (ffn2_w_gate, ffn2_w_up, ffn2_w_down)]
    w_out_b = w_out.astype(BF16)
    outs = [[] for _ in range(8)]
    kv_bufs = None
    for l in range(depth):
        if l == 0:
            x = ffn_half_step(x_prompt.reshape(mp, d), ffn1_norm[l], *ffn1, l, out_rows=m_all)
            x = ffn_half_step(x_sample.reshape(ms, d), ffn1_norm[l], *ffn1, l, out_row0=mp, y_init=x)
        else:
            x = ffn_half_step(x, ffn1_norm[l], *ffn1, l)
        w_big, w_small = _split_w_in(w_in[l])
        zb, zs, kv_bufs = mix_project(x, mix_norm[l], w_big, w_small, l, depth, mp, kv_bufs)
        cmp_p, slc_p, win_p, cmp_s, slc_s, win_s = kv_bufs
        z_pool = zb[:, ZB_POOL:ZB_MQ]

        y_pool = pool_prompt(zb, pool_w[l], pool_scale[l], batch=bp, seq=tp, col=ZB_POOL, out_rows=m_all)
        y_m, p_c, p_n, p_m = mlstm_mix(
            zb, zs, mlstm_if_bias[l], mlstm_norm[l], zeros(bp, MLSTM_HEADS, MLSTM_HD, MLSTM_HD),
            zeros(bp, MLSTM_HEADS, MLSTM_HD), zeros(bp, MLSTM_HEADS), row0=0, batch=bp, seq=tp, col_q=ZB_MQ,
            out_rows=m_all)
        y_nsa = nsa_prompt(zb, zs, cmp_p, slc_p, win_p, nsa_cmp_pos_w[l], nsa_cmp_proj[l], layer=l, batch=bp,
                           seq=tp, col_q=ZB_NQ, out_rows=m_all)

        pool_full = jnp.concatenate([zeros(bs, POOL_HALO - POOL_BUF, pd), state_pool[l],
                                     z_pool[mp:].reshape(bs, ts, pd)], axis=1)
        y_pool = pool_sample(pool_full, pool_w[l], pool_scale[l], y_pool, pos0=past_len, row0=mp)
        y_m, s_c, s_n, s_m = mlstm_mix(
            zb, zs, mlstm_if_bias[l], mlstm_norm[l], state_mlstm_C[l], state_mlstm_n[l], state_mlstm_m[l],
            row0=mp, batch=bs, seq=ts, col_q=ZB_MQ, y_init=y_m)
        y_nsa = nsa_sample(zb, zs, slc_s, win_s, cache_kv_cmp, cache_kv_slc, state_kv_win, page_table,
                           nsa_cmp_pos_w[l], nsa_cmp_proj[l], y_nsa, layer=l, row0=mp, batch=bs, ts=ts,
                           col_q=ZB_NQ)

        x = out_project(x, y_pool, y_m, y_nsa, w_out_b, l)
        if l < depth - 1:
            x = ffn_half_step(x, ffn2_norm[l], *ffn2, l)
        else:
            y_prompt = ffn_half_step(x, ffn2_norm[l], *ffn2, l, gf=final_norm, rows=mp).reshape(bp, tp, d)
            y_sample = ffn_half_step(x, ffn2_norm[l], *ffn2, l, gf=final_norm, in_row0=mp, rows=ms).reshape(bs, ts, d)

        pool_p = jnp.concatenate([zeros(bp, POOL_BUF, pd), z_pool[:mp].reshape(bp, tp, pd)], axis=1)
        layer_out = (pool_p[:, -POOL_BUF:], p_c, p_n, p_m, pool_full[:, -POOL_BUF:], s_c, s_n, s_m)
        for acc, a in zip(outs, layer_out):
            acc.append(a)
    p_pool, p_c, p_n, p_m, s_pool, s_c, s_n, s_m = [jnp.stack(a) for a in outs]
    p_kv = [b.reshape(depth, bp, tp, *kv_row) for b in (cmp_p, slc_p, win_p)]
    s_kv = [b.reshape(depth, bs, ts, *kv_row) for b in (cmp_s, slc_s, win_s)]
    s_kv_win = window_state(state_kv_win, win_s, ts)
    return (y_prompt, y_sample, p_kv[0], p_kv[1], p_kv[2][:, :, tp - min(WINDOW, tp):], p_pool, p_c, p_n, p_m,
            s_kv[0], s_kv[1], s_kv_win, s_pool, s_c, s_n, s_m)
```

```python
import functools
import math

import jax
import jax.numpy as jnp
from jax import lax
from jax.experimental import pallas as pl
from jax.experimental.pallas import tpu as pltpu

F32 = jnp.float32
BF16 = jnp.bfloat16
EPS = 1e-6

VMEM_LIMIT_BYTES = 56 * 1024 * 1024
LANES = 128

POOL_WINDOWS = (2, 4, 8, 16)
POOL_BUF = 15
MLSTM_HEADS = 4
MLSTM_HD = 128
MLSTM_CHUNK = 64
NSA_HD = 128
NSA_HEADS = 8
NSA_KV_HEADS = 2
NSA_GROUP = 4
CMP_LEN = 32
CMP_STRIDE = 16
SEL_BLOCK = 64
SEL_TOPN = 16
WINDOW = 512
FORCE_BONUS = 1.0e4

NEG_MASK = -1.0e30
NT = (((1,), (1,)), ((), ()))
TN = (((0,), (0,)), ((), ()))


def _pick_tile(n, pref):
    t = pref
    while t > 8 and n % t:
        t //= 2
    assert n % t == 0, (n, pref)
    return t


def _params(*sem):
    return pltpu.CompilerParams(dimension_semantics=sem, vmem_limit_bytes=VMEM_LIMIT_BYTES)


def _drop_alias_ref(body, index, *refs):
    return body(*refs[:index], *refs[index + 1:])


def _rms_rows(x, g):
    ms = jnp.mean(x * x, axis=-1, keepdims=True)
    return x * lax.rsqrt(ms + EPS) * g


def _ffn_body(x_ref, g_ref, wg_ref, wu_ref, wd_ref, gf_ref, o_ref, n_scr, *, final_norm):
    f = pl.program_id(1)

    @pl.when(f == 0)
    def _():
        x = x_ref[...]
        n_scr[...] = _rms_rows(x, g_ref[...]).astype(BF16)
        o_ref[...] = x

    n = n_scr[...]
    hg = jnp.dot(n, wg_ref[...], preferred_element_type=F32)
    hu = jnp.dot(n, wu_ref[...], preferred_element_type=F32)
    h = (hg * jax.nn.sigmoid(hg) * hu).astype(BF16)
    o_ref[...] += 0.5 * jnp.dot(h, wd_ref[...], preferred_element_type=F32)

    if final_norm:
        @pl.when(f == pl.num_programs(1) - 1)
        def _():
            o_ref[...] = _rms_rows(o_ref[...], gf_ref[...])


def ffn_half_step(x, g, wg, wu, wd, layer, gf=None, *, in_row0=0, rows=None, out_rows=None, out_row0=0,
                  y_init=None):
    d = x.shape[1]
    rows = x.shape[0] if rows is None else rows
    fdim = wg.shape[2]
    tm = _pick_tile(math.gcd(math.gcd(rows, in_row0), out_row0), 512)
    tf = _pick_tile(fdim, 512)
    final_norm = gf is not None
    if gf is None:
        gf = g
    i0, o0 = in_row0 // tm, out_row0 // tm
    body = functools.partial(_ffn_body, final_norm=final_norm)
    extra_specs, extra_args, aliases = [], [], {}
    if y_init is not None:
        body = functools.partial(_drop_alias_ref, body, 6)
        extra_specs, extra_args, aliases = [pl.BlockSpec(memory_space=pl.ANY)], [y_init], {6: 0}
        out_rows = y_init.shape[0]
    return pl.pallas_call(
        body,
        out_shape=jax.ShapeDtypeStruct((out_rows or rows, d), F32),
        grid=(rows // tm, fdim // tf),
        in_specs=[
            pl.BlockSpec((tm, d), lambda i, f: (i0 + i, 0)),
            pl.BlockSpec((1, d), lambda i, f: (0, 0)),
            pl.BlockSpec((None, d, tf), lambda i, f: (layer, 0, f)),
            pl.BlockSpec((None, d, tf), lambda i, f: (layer, 0, f)),
            pl.BlockSpec((None, tf, d), lambda i, f: (layer, f, 0)),
            pl.BlockSpec((1, d), lambda i, f: (0, 0)),
        ] + extra_specs,
        out_specs=pl.BlockSpec((tm, d), lambda i, f: (o0 + i, 0)),
        scratch_shapes=[pltpu.VMEM((tm, d), BF16)],
        input_output_aliases=aliases,
        compiler_params=_params("parallel", "arbitrary"),
        name="ffn_half_step",
    )(x, g.reshape(1, d), wg, wu, wd, gf.reshape(1, d), *extra_args)


KV_SLABS = 3
KV_ROW = 4


def _inproj_body(x_ref, g_ref, w_ref, ws_ref, *rest, main_cols, n_prompt_tiles, n_aliased):
    zb_ref, zs_ref, *kv_refs, n_scr = rest[n_aliased:]
    i, j = pl.program_id(0), pl.program_id(1)
    tm, tn = x_ref.shape[0], w_ref.shape[1]
    slab_w = KV_ROW * LANES

    @pl.when(j == 0)
    def _():
        n_scr[...] = _rms_rows(x_ref[...], g_ref[...]).astype(BF16)
        zs_ref[...] = jnp.dot(n_scr[...], ws_ref[...], preferred_element_type=F32)

    z = jnp.dot(n_scr[...], w_ref[...], preferred_element_type=F32)

    @pl.when(j * tn < main_cols)
    def _():
        zb_ref[...] = z

    def scatter(ref, off):
        for c in range(KV_ROW):
            ref[pl.ds(c, tm, stride=KV_ROW), :] = z[:, off + c * LANES:off + (c + 1) * LANES]

    for k in range(KV_SLABS):
        jt, off = divmod(main_cols + k * slab_w, tn)
        pl.when((j == jt) & (i < n_prompt_tiles))(functools.partial(scatter, kv_refs[k], off))
        pl.when((j == jt) & (i >= n_prompt_tiles))(functools.partial(scatter, kv_refs[KV_SLABS + k], off))


def mix_project(x, g, w, w_small, layer, depth, mp, kv_bufs=None):
    m, d = x.shape
    n = w.shape[1]
    tn = _pick_tile(n, 1024)
    ms = m - mp
    tm = _pick_tile(math.gcd(mp, ms), 512)
    main_cols = n - KV_SLABS * KV_ROW * LANES
    n_main = -(-main_cols // tn)
    npt = mp // tm
    out_shape = [jax.ShapeDtypeStruct((m, n_main * tn), F32), jax.ShapeDtypeStruct((m, LANES), F32)]
    out_shape += [jax.ShapeDtypeStruct((depth, mp * KV_ROW, LANES), F32)] * KV_SLABS
    out_shape += [jax.ShapeDtypeStruct((depth, ms * KV_ROW, LANES), F32)] * KV_SLABS
    out_specs = [pl.BlockSpec((tm, tn), lambda i, j: (i, jnp.minimum(j, n_main - 1))),
                 pl.BlockSpec((tm, LANES), lambda i, j: (i, 0))]
    out_specs += [pl.BlockSpec((None, tm * KV_ROW, LANES), lambda i, j: (layer, jnp.minimum(i, npt - 1), 0))] * KV_SLABS
    out_specs += [pl.BlockSpec((None, tm * KV_ROW, LANES), lambda i, j: (layer, jnp.maximum(i - npt, 0), 0))] * KV_SLABS
    in_specs = [
        pl.BlockSpec((tm, d), lambda i, j: (i, 0)),
        pl.BlockSpec((1, d), lambda i, j: (0, 0)),
        pl.BlockSpec((d, tn), lambda i, j: (0, j)),
        pl.BlockSpec((d, LANES), lambda i, j: (0, 0)),
    ]
    args = [x, g.reshape(1, d), w, w_small]
    aliases = {}
    if kv_bufs is not None:
        in_specs += [pl.BlockSpec(memory_space=pl.ANY)] * len(kv_bufs)
        aliases = {len(args) + k: 2 + k for k in range(len(kv_bufs))}
        args += list(kv_bufs)
    zb, zs, *bufs = pl.pallas_call(
        functools.partial(_inproj_body, main_cols=main_cols, n_prompt_tiles=npt, n_aliased=len(aliases)),
        out_shape=out_shape,
        grid=(m // tm, n // tn),
        in_specs=in_specs,
        out_specs=out_specs,
        scratch_shapes=[pltpu.VMEM((tm, d), BF16)],
        input_output_aliases=aliases,
        compiler_params=_params("arbitrary", "arbitrary"),
        name="mix_project",
    )(*args)
    return zb, zs, bufs


def _outproj_body(x_ref, ya_ref, yb_ref, yc_ref, wa_ref, wb_ref, wc_ref, o_ref):
    acc = x_ref[...]
    acc += jnp.dot(ya_ref[...].astype(BF16), wa_ref[...], preferred_element_type=F32)
    acc += jnp.dot(yb_ref[...].astype(BF16), wb_ref[...], preferred_element_type=F32)
    acc += jnp.dot(yc_ref[...].astype(BF16), wc_ref[...], preferred_element_type=F32)
    o_ref[...] = acc


def out_project(x, y_pool, y_mlstm, y_nsa, w_out, layer):
    m, d = x.shape
    da, db, dc = y_pool.shape[1], y_mlstm.shape[1], y_nsa.shape[1]
    assert da == db and dc % da == 0
    tm = _pick_tile(m, 512)
    tn = d
    return pl.pallas_call(
        _outproj_body,
        out_shape=jax.ShapeDtypeStruct((m, d), F32),
        grid=(m // tm, d // tn),
        in_specs=[
            pl.BlockSpec((tm, tn), lambda i, j: (i, j)),
            pl.BlockSpec((tm, da), lambda i, j: (i, 0)),
            pl.BlockSpec((tm, db), lambda i, j: (i, 0)),
            pl.BlockSpec((tm, dc), lambda i, j: (i, 0)),
            pl.BlockSpec((None, da, tn), lambda i, j: (layer, 0, j)),
            pl.BlockSpec((None, db, tn), lambda i, j: (layer, 1, j)),
            pl.BlockSpec((None, dc, tn), lambda i, j: (layer, (da + db) // dc, j)),
        ],
        out_specs=pl.BlockSpec((tm, tn), lambda i, j: (i, j)),
        compiler_params=_params("parallel", "arbitrary"),
        name="out_project",
    )(x, y_pool, y_mlstm, y_nsa, w_out, w_out, w_out)


POOL_HALO = 16


def _pool_group(load, g, n_avail, w_ref, sc_ref):
    w = POOL_WINDOWS[g]
    z = load(0)
    acc = z
    for j in range(1, w):
        acc = acc + load(j)
    d = acc / jnp.minimum(n_avail, w).astype(F32) - z
    lead = d.shape[:-1]
    gd = d.shape[-1]
    y = jnp.dot(d.reshape(-1, gd).astype(BF16), w_ref[g].astype(BF16), preferred_element_type=F32)
    return (y * sc_ref[:, g * gd:(g + 1) * gd]).reshape(*lead, gd)


def _pool_prompt_body(z_ref, w_ref, sc_ref, o_ref, full_scr, *, chunk):
    seq, pd = z_ref.shape
    gd = pd // len(POOL_WINDOWS)
    full_scr[0:POOL_HALO, :] = jnp.zeros((POOL_HALO, pd), F32)
    full_scr[POOL_HALO:POOL_HALO + seq, :] = z_ref[...]
    for c in range(seq // chunk):
        n_avail = c * chunk + 1 + lax.broadcasted_iota(jnp.int32, (chunk, gd), 0)
        for g in range(len(POOL_WINDOWS)):
            load = lambda j: full_scr[pl.ds(POOL_HALO + c * chunk - j, chunk), g * gd:(g + 1) * gd]
            o_ref[c * chunk:(c + 1) * chunk, g * gd:(g + 1) * gd] = _pool_group(load, g, n_avail, w_ref, sc_ref)


def pool_prompt(zb, pool_w, pool_scale, *, batch, seq, col, out_rows=None):
    pd = pool_scale.shape[0]
    chunk = _pick_tile(seq, 256)
    return pl.pallas_call(
        functools.partial(_pool_prompt_body, chunk=chunk),
        out_shape=jax.ShapeDtypeStruct((out_rows or batch * seq, pd), F32),
        grid=(batch,),
        in_specs=[
            pl.BlockSpec((seq, pd), lambda b: (b, col // pd)),
            pl.BlockSpec(pool_w.shape, lambda b: (0, 0, 0)),
            pl.BlockSpec((1, pd), lambda b: (0, 0)),
        ],
        out_specs=pl.BlockSpec((seq, pd), lambda b: (b, 0)),
        scratch_shapes=[pltpu.VMEM((POOL_HALO + seq, pd), F32)],
        compiler_params=_params("parallel"),
        name="pool_prompt",
    )(zb, pool_w, pool_scale.reshape(1, pd))


def _pool_sample_body(full_ref, w_ref, sc_ref, o_ref, *, pos0):
    bt, rows, pd = full_ref.shape
    ts = rows - POOL_HALO
    gd = pd // len(POOL_WINDOWS)
    n_avail = pos0 + 1 + lax.broadcasted_iota(jnp.int32, (bt, ts, gd), 1)
    for g in range(len(POOL_WINDOWS)):
        load = lambda j: full_ref[:, pl.ds(POOL_HALO - j, ts), g * gd:(g + 1) * gd]
        o_ref[:, g * gd:(g + 1) * gd] = _pool_group(load, g, n_avail, w_ref, sc_ref).reshape(bt * ts, gd)


def pool_sample(full, pool_w, pool_scale, y_init, *, pos0, row0):
    batch, rows, pd = full.shape
    ts = rows - POOL_HALO
    bt = math.gcd(batch, 32)
    assert ts % 8 == 0 and row0 % (bt * ts) == 0
    r0 = row0 // (bt * ts)
    return pl.pallas_call(
        functools.partial(_drop_alias_ref, functools.partial(_pool_sample_body, pos0=pos0), 3),
        out_shape=jax.ShapeDtypeStruct(y_init.shape, F32),
        grid=(batch // bt,),
        in_specs=[
            pl.BlockSpec((bt, rows, pd), lambda b: (b, 0, 0)),
            pl.BlockSpec(pool_w.shape, lambda b: (0, 0, 0)),
            pl.BlockSpec((1, pd), lambda b: (0, 0)),
            pl.BlockSpec(memory_space=pl.ANY),
        ],
        out_specs=pl.BlockSpec((bt * ts, pd), lambda b: (r0 + b, 0)),
        input_output_aliases={3: 0},
        compiler_params=_params("parallel"),
        name="pool_sample",
    )(full, pool_w, pool_scale.reshape(1, pd), y_init)


def _log_sigmoid(x):
    return jnp.minimum(x, 0.0) - jnp.log1p(jnp.exp(-jnp.abs(x)))


def _mlstm_body(q_ref, k_ref, v_ref, og_ref, g_ref, bias_ref, gn_ref, c0_ref, n0_ref, m0_ref,
                y_ref, c_ref, n_ref, m_ref, *, L):
    nseq = q_ref.shape[0] // L
    H, D = MLSTM_HEADS, MLSTM_HD
    hi = lax.Precision.HIGHEST

    @pl.when(pl.program_id(1) == 0)
    def _():
        c_ref[...] = c0_ref[...]
        n_ref[...] = n0_ref[...]
        m_ref[...] = m0_ref[...]

    sel = (lax.broadcasted_iota(jnp.int32, (8, LANES), 0) == lax.broadcasted_iota(jnp.int32, (8, LANES), 1)).astype(F32)
    li = lax.broadcasted_iota(jnp.int32, (L, L), 0)
    si = lax.broadcasted_iota(jnp.int32, (L, L), 1)
    causal = li >= si
    lane = lax.broadcasted_iota(jnp.int32, (1, LANES), 1)
    chains = [(s, h) for s in range(nseq) for h in range(H)]

    gates = []
    for s in range(nseq):
        rows = slice(s * L, (s + 1) * L)
        gz = g_ref[rows, :] + bias_ref[...]
        gz_rows = lax.dot_general(sel, gz, NT, precision=hi, preferred_element_type=F32)
        b_cols = jnp.dot(causal.astype(F32), _log_sigmoid(gz), precision=hi, preferred_element_type=F32)
        b_rows = jnp.dot(_log_sigmoid(gz_rows), (li <= si).astype(F32), precision=hi, preferred_element_type=F32)
        gates.append((gz, gz_rows, b_cols, b_rows, m_ref[s]))

    qk, qc, state = {}, {}, {}
    for s, h in chains:
        rows, cols = slice(s * L, (s + 1) * L), slice(h * D, (h + 1) * D)
        qh = q_ref[rows, cols]
        kh = k_ref[rows, cols] * (D ** -0.5)
        ch = c_ref[s, h]
        nh = n_ref[s, h:h + 1, :]
        qb, kb = qh.astype(BF16), kh.astype(BF16)
        qk[s, h] = lax.dot_general(qb, kb, NT, preferred_element_type=F32)
        qc[s, h] = lax.dot_general(qb, ch.astype(BF16), NT, preferred_element_type=F32)
        state[s, h] = (qh, kh, kb, ch, nh)

    sm, stats = {}, {}
    for s, h in chains:
        gz, gz_rows, b_cols, b_rows, m_all = gates[s]
        bc = b_cols[:, H + h:H + h + 1]
        ic = gz[:, h:h + 1]
        br = b_rows[H + h:H + h + 1, :]
        ir = gz_rows[h:h + 1, :]
        m_prev = m_all[:, h:h + 1]
        dmat = jnp.where(causal, bc - br + ir, NEG_MASK)
        inter = bc + m_prev
        m_t = jnp.maximum(inter, jnp.max(dmat, axis=1, keepdims=True))
        sm[s, h] = qk[s, h] * jnp.exp(dmat - m_t)
        m_new = m_t[L - 1:L, :]
        b_last = bc[L - 1:L, :]
        stats[s, h] = (jnp.exp(inter - m_t), m_t, m_new, jnp.exp(b_last + m_prev - m_new),
                       jnp.exp(b_last - bc + ic - m_new))

    num, c_new = {}, {}
    for s, h in chains:
        rows, cols = slice(s * L, (s + 1) * L), slice(h * D, (h + 1) * D)
        a_inter, _, _, decay, w_col = stats[s, h]
        qh, kh, kb, ch, nh = state[s, h]
        vh = v_ref[rows, cols]
        num[s, h] = jnp.dot(sm[s, h].astype(BF16), vh.astype(BF16), preferred_element_type=F32) + a_inter * qc[s, h]
        c_new[s, h] = decay * ch + lax.dot_general((vh * w_col).astype(BF16), kb, TN, preferred_element_type=F32)

    m_out = [gates[s][4] for s in range(nseq)]
    for s, h in chains:
        rows, cols = slice(s * L, (s + 1) * L), slice(h * D, (h + 1) * D)
        a_inter, m_t, m_new, decay, w_col = stats[s, h]
        qh, kh, kb, ch, nh = state[s, h]
        den = jnp.sum(sm[s, h], axis=1, keepdims=True) + a_inter * jnp.sum(qh * nh, axis=1, keepdims=True)
        den = jnp.maximum(jnp.abs(den), jnp.exp(-m_t))
        hh = num[s, h] / den
        mu = jnp.mean(hh, axis=1, keepdims=True)
        var = jnp.mean(jnp.square(hh - mu), axis=1, keepdims=True)
        hn = (hh - mu) * lax.rsqrt(var + EPS) * gn_ref[:, cols]
        y_ref[rows, cols] = jax.nn.sigmoid(og_ref[rows, cols]) * hn
        c_ref[s, h] = c_new[s, h]
        n_ref[s, h:h + 1, :] = decay * nh + jnp.sum(kh * w_col, axis=0, keepdims=True)
        m_out[s] = jnp.where(lane == h, m_new, m_out[s])
    for s in range(nseq):
        m_ref[s] = m_out[s]


def mlstm_mix(zb, zs, if_bias, mnorm, c0, n0, m0, *, row0, batch, seq, col_q, layer, depth, c_all=None,
              y_init=None, out_rows=None):
    H, D = MLSTM_HEADS, MLSTM_HD
    dim = H * D
    L = math.gcd(seq, MLSTM_CHUNK)
    nc = seq // L
    nseq = math.gcd(batch, min(4, MLSTM_CHUNK // L)) if nc == 1 else 1
    rows = nseq * L
    assert L % 8 == 0 and row0 % rows == 0 and col_q % dim == 0
    r0 = row0 // rows
    cq = col_q // dim
    bias = jnp.pad(if_bias, (0, LANES - if_bias.shape[0])).reshape(1, LANES)
    m0p = jnp.pad(m0, ((0, 0), (0, LANES - H))).reshape(batch, 1, LANES)
    row = lambda b, c: r0 + b * nc + c
    y_rows = (out_rows or batch * seq) if y_init is None else y_init.shape[0]
    y_r0 = 0 if y_init is None else r0
    in_specs = [
        pl.BlockSpec((rows, dim), lambda b, c: (row(b, c), cq)),
        pl.BlockSpec((rows, dim), lambda b, c: (row(b, c), cq + 1)),
        pl.BlockSpec((rows, dim), lambda b, c: (row(b, c), cq + 2)),
        pl.BlockSpec((rows, dim), lambda b, c: (row(b, c), cq + 3)),
        pl.BlockSpec((rows, LANES), lambda b, c: (row(b, c), 0)),
        pl.BlockSpec((1, LANES), lambda b, c: (0, 0)),
        pl.BlockSpec((1, dim), lambda b, c: (0, 0)),
        pl.BlockSpec((None, nseq, H, D, D), lambda b, c: (layer, b, 0, 0, 0)),
        pl.BlockSpec((nseq, H, D), lambda b, c: (b, 0, 0)),
        pl.BlockSpec((nseq, 1, LANES), lambda b, c: (b, 0, 0)),
    ]
    args = [zb, zb, zb, zb, zs, bias, mnorm.reshape(1, dim), c0, n0, m0p]
    aliases = {}
    body = functools.partial(_mlstm_body, L=L)
    for out_index, init in ((1, c_all), (0, y_init)):
        if init is not None:
            in_specs.append(pl.BlockSpec(memory_space=pl.ANY))
            args.append(init)
            aliases[len(args) - 1] = out_index
            body = functools.partial(_drop_alias_ref, body, len(args) - 1)
    y, c_out, n_out, m_out = pl.pallas_call(
        body,
        out_shape=(jax.ShapeDtypeStruct((y_rows, dim), F32),
                   jax.ShapeDtypeStruct((depth, batch, H, D, D), F32),
                   jax.ShapeDtypeStruct((batch, H, D), F32),
                   jax.ShapeDtypeStruct((batch, 1, LANES), F32)),
        grid=(batch // nseq, nc),
        in_specs=in_specs,
        out_specs=(
            pl.BlockSpec((rows, dim), lambda b, c: (y_r0 + b * nc + c, 0)),
            pl.BlockSpec((None, nseq, H, D, D), lambda b, c: (layer, b, 0, 0, 0)),
            pl.BlockSpec((nseq, H, D), lambda b, c: (b, 0, 0)),
            pl.BlockSpec((nseq, 1, LANES), lambda b, c: (b, 0, 0)),
        ),
        input_output_aliases=aliases,
        compiler_params=_params("parallel", "arbitrary"),
        name="mlstm_mix",
    )(*args)
    return y, c_out, n_out, m_out[:, 0, :H]


KEY_TILE = 128
KEY_CHUNK = 256
SEL_COLS = 64
POS_HI, POS_LO = SEL_COLS, SEL_COLS + 1
NEG_SEL = -1.0e9


def _slope(h):
    return 2.0 ** (-(8.0 / NSA_HEADS) * (h + 1))


def _key_features(pos, onehot):
    lane = lax.broadcasted_iota(jnp.int32, pos.shape, 1)
    hi = lax.shift_right_logical(pos, 6)
    lo = jnp.bitwise_and(pos, SEL_BLOCK - 1)
    f = jnp.where(lane == POS_HI, hi.astype(F32), jnp.where(lane == POS_LO, lo.astype(F32), 0.0))
    if onehot:
        f = jnp.where(lane == hi, 1.0, f)
    return f


def _query_features(shape, h):
    lane = lax.broadcasted_iota(jnp.int32, shape, 1)
    return jnp.where(lane == POS_HI, SEL_BLOCK * _slope(h), jnp.where(lane == POS_LO, _slope(h), 0.0))


def _compress_block_rows(load, pw_ref, kv, g):
    cols = slice(g * NSA_HD, (g + 1) * NSA_HD)
    a0 = a1 = None
    for j in range(CMP_STRIDE):
        rows = load(j)
        t0 = rows * pw_ref[kv, j:j + 1, cols]
        t1 = rows * pw_ref[kv, CMP_STRIDE + j:CMP_STRIDE + j + 1, cols]
        a0 = t0 if a0 is None else a0 + t0
        a1 = t1 if a1 is None else a1 + t1
    return a0, a1


def _finish_compress(acc, proj_ref, kcmp_aug, vcmp, kv, g, n_ch, ncp, v_transposed):
    c = jnp.dot(acc.astype(BF16), proj_ref[kv, g].astype(BF16), preferred_element_type=F32)
    if kv == 0:
        kcmp_aug[g, 0:n_ch, 0:NSA_HD] = c.astype(BF16)
        n = lax.broadcasted_iota(jnp.int32, (ncp, LANES), 0)
        kcmp_aug[g, :, NSA_HD:2 * NSA_HD] = _key_features(n * CMP_STRIDE + (CMP_LEN - 1), False).astype(BF16)
    elif v_transposed:
        if ncp > n_ch:
            c = jnp.concatenate([c, jnp.zeros((ncp - n_ch, NSA_HD), F32)], axis=0)
        for blk in range(ncp // LANES):
            vcmp[g, :, blk * LANES:(blk + 1) * LANES] = c[blk * LANES:(blk + 1) * LANES].T.astype(BF16)
    else:
        vcmp[g, 0:n_ch, :] = c.astype(BF16)


def _masked_softmax(s, mask):
    s = jnp.where(mask, s, NEG_MASK)
    m = jnp.max(s, axis=1, keepdims=True)
    e = jnp.where(mask, jnp.exp(s - m), 0.0)
    return e / jnp.maximum(jnp.sum(e, axis=1, keepdims=True), 1e-30)


def _select_blocks(psum, t0, n_cmp, n_sel, queries_on_lanes):
    ncp = psum.shape[0] if queries_on_lanes else psum.shape[1]
    nsp = -(-n_sel // 8) * 8
    j = lax.broadcasted_iota(jnp.int32, (nsp, ncp), 0)
    n = lax.broadcasted_iota(jnp.int32, (nsp, ncp), 1)
    cover = ((n * CMP_STRIDE < j * SEL_BLOCK + SEL_BLOCK) & (n * CMP_STRIDE + CMP_LEN > j * SEL_BLOCK)
             & (n < n_cmp)).astype(F32)
    if queries_on_lanes:
        imp = jnp.dot(cover, psum, precision=lax.Precision.HIGHEST, preferred_element_type=F32)
    else:
        imp = lax.dot_general(cover, psum, NT, precision=lax.Precision.HIGHEST, preferred_element_type=F32)
    jq = lax.broadcasted_iota(jnp.int32, (nsp, LANES), 0)
    t = t0 + lax.broadcasted_iota(jnp.int32, (nsp, LANES), 1)
    cur = lax.shift_right_logical(t, 6)
    forced = (jq == 0) | (jq == cur) | (jq == cur - 1)
    valid = (jq * SEL_BLOCK <= t) & (jq < n_sel)
    score = jnp.where(valid, imp + jnp.where(forced, FORCE_BONUS, 0.0), -jnp.inf)
    jf = jq.astype(F32)
    sel = jnp.zeros((nsp, LANES), F32)
    for _ in range(min(SEL_TOPN, n_sel)):
        mx = jnp.max(score, axis=0, keepdims=True)
        first = jnp.min(jnp.where(score == mx, jf, 1.0e9), axis=0, keepdims=True)
        pick = jf == first
        sel = jnp.where(pick, 1.0, sel)
        score = jnp.where(pick, -jnp.inf, score)
    bias = jnp.where((sel > 0.5) | (jq >= n_sel), 0.0, NEG_SEL)
    bias = jnp.concatenate([bias, jnp.zeros((LANES - nsp, LANES), F32)], axis=0)
    return bias.T


def _online_step_t(state, s_t, v_t):
    m, l, acc = state
    m_new = jnp.maximum(m, jnp.max(s_t, axis=0, keepdims=True))
    alpha = jnp.exp(m - m_new)
    p = jnp.exp(s_t - m_new)
    l = alpha * l + jnp.sum(p, axis=0, keepdims=True)
    acc = alpha * acc + jnp.dot(v_t, p.astype(BF16), preferred_element_type=F32)
    return m_new, l, acc


def _stack_heads(q_heads, feats):
    return jnp.concatenate(
        [jnp.concatenate([q, f.astype(BF16)], axis=1) for q, f in zip(q_heads, feats)], axis=0)


def _write_gated(o_ref, gates, g, rows, o_cmp, o_s, o_w):
    for r in range(NSA_GROUP):
        h = g * NSA_GROUP + r
        sl = slice(r * rows, (r + 1) * rows)
        c = 8 + h
        o = gates[:, c:c + 1] * o_cmp[sl] + gates[:, c + 8:c + 9] * o_s[sl] + gates[:, c + 16:c + 17] * o_w[sl]
        o_ref[:, h * NSA_HD:(h + 1) * NSA_HD] = o


def _nsa_prompt_body(q_ref, kc_ref, ks_ref, kw_ref, gate_ref, pw_ref, proj_ref, o_ref,
                     ks_aug, vs_t, kw_aug, vw_t, kcmp_aug, vcmp_t, a1_scr, *, seq):
    i = pl.program_id(1)
    tq = KEY_TILE
    n_ch = seq // CMP_STRIDE
    n_cmp = n_ch - 1
    n_sel = seq // SEL_BLOCK
    ncp = kcmp_aug.shape[1]
    G, HD = NSA_GROUP, NSA_HD
    Q = G * tq

    @pl.when(i == 0)
    def _build():
        pos = lax.broadcasted_iota(jnp.int32, (seq, LANES), 0)
        f_sel = _key_features(pos, True).astype(BF16)
        f_win = _key_features(pos, False).astype(BF16)
        for g in range(NSA_KV_HEADS):
            ks_aug[g, :, 0:HD] = ks_ref[pl.ds(g, seq, stride=KV_ROW), :].astype(BF16)
            ks_aug[g, :, HD:2 * HD] = f_sel
            kw_aug[g, :, 0:HD] = kw_ref[pl.ds(g, seq, stride=KV_ROW), :].astype(BF16)
            kw_aug[g, :, HD:2 * HD] = f_win
            for kt in range(seq // KEY_TILE):
                rows = slice(kt * KEY_TILE, (kt + 1) * KEY_TILE)
                src_rows = pl.ds(kt * KEY_TILE * KV_ROW + 2 + g, KEY_TILE, stride=KV_ROW)
                vs_t[g, :, rows] = ks_ref[src_rows, :].T.astype(BF16)
                vw_t[g, :, rows] = kw_ref[src_rows, :].T.astype(BF16)
        kcmp_aug[...] = jnp.zeros(kcmp_aug.shape, BF16)
        a1_scr[n_ch:n_ch + 8, :] = jnp.zeros((8, HD), F32)
        for kv in range(2):
            for g in range(NSA_KV_HEADS):
                c = kv * NSA_KV_HEADS + g
                a0, a1 = _compress_block_rows(
                    lambda j: kc_ref[pl.ds(j * KV_ROW + c, n_ch, stride=KV_ROW * CMP_STRIDE), :], pw_ref, kv, g)
                a1_scr[0:n_ch, :] = a1
                _finish_compress(a0 + a1_scr[pl.ds(1, n_ch), :], proj_ref, kcmp_aug, vcmp_t, kv, g, n_ch, ncp, True)

    t0 = i * tq
    key = lax.broadcasted_iota(jnp.int32, (KEY_CHUNK, Q), 0)
    t_cols = t0 + jnp.bitwise_and(lax.broadcasted_iota(jnp.int32, (KEY_CHUNK, Q), 1), tq - 1)
    gates_t = jax.nn.sigmoid(gate_ref[...]).T
    scale = HD ** -0.5

    n = lax.broadcasted_iota(jnp.int32, (ncp, Q), 0)
    t_c = t0 + jnp.bitwise_and(lax.broadcasted_iota(jnp.int32, (ncp, Q), 1), tq - 1)
    mask = (n * CMP_STRIDE + (CMP_LEN - 1) <= t_c) & (n < n_cmp)
    kv_groups = range(NSA_KV_HEADS)
    q_heads = [[(q_ref[:, (g * G + r) * HD:(g * G + r + 1) * HD] * scale).astype(BF16) for r in range(G)]
               for g in kv_groups]
    feats = [[_query_features((tq, LANES), g * G + r) for r in range(G)] for g in kv_groups]
    q_plain = [_stack_heads(q_heads[g], feats[g]) for g in kv_groups]
    s_c = [jnp.where(mask, lax.dot_general(kcmp_aug[g], q_plain[g], NT, preferred_element_type=F32), NEG_MASK)
           for g in kv_groups]
    e_c = [jnp.where(mask, jnp.exp(s - jnp.max(s, axis=0, keepdims=True)), 0.0) for s in s_c]
    p_c = [e / jnp.maximum(jnp.sum(e, axis=0, keepdims=True), 1e-30) for e in e_c]
    o_cmp = [jnp.dot(vcmp_t[g], p_c[g].astype(BF16), preferred_element_type=F32) for g in kv_groups]
    psum = [p[:, 0:tq] + p[:, tq:2 * tq] + p[:, 2 * tq:3 * tq] + p[:, 3 * tq:4 * tq] for p in p_c]
    bias = [_select_blocks(psum[g], t0, n_cmp, n_sel, True) for g in kv_groups]
    q_sel = [_stack_heads(q_heads[g], [f + bias[g] for f in feats[g]]) for g in kv_groups]

    last = t0 // KEY_CHUNK
    first_w = jnp.maximum(t0 - WINDOW, 0) // KEY_CHUNK
    init = (jnp.full((1, Q), NEG_MASK, F32), jnp.zeros((1, Q), F32), jnp.zeros((HD, Q), F32))

    def scores(c, k_aug, q, g, valid):
        off = pl.multiple_of(c * KEY_CHUNK, KEY_CHUNK)
        s_t = lax.dot_general(k_aug[g, pl.ds(off, KEY_CHUNK), :], q, NT, preferred_element_type=F32)
        return s_t if valid is None else jnp.where(valid(off + key), s_t, NEG_MASK)

    def values(c, v_t, g):
        return v_t[g, :, pl.ds(pl.multiple_of(c * KEY_CHUNK, KEY_CHUNK), KEY_CHUNK)]

    def causal(kpos):
        return kpos <= t_cols

    def band(kpos):
        return (kpos <= t_cols) & (t_cols - kpos <= WINDOW)

    def early(c, sel):
        s = [scores(c, ks_aug, q_sel[g], g, None) for g in kv_groups]
        return tuple(_online_step_t(sel[g], s[g], values(c, vs_t, g)) for g in kv_groups)

    def late(c, sts):
        sel, win = sts
        s_sel = [scores(c, ks_aug, q_sel[g], g, causal) for g in kv_groups]
        s_win = [scores(c, kw_aug, q_plain[g], g, band) for g in kv_groups]
        sel = tuple(_online_step_t(sel[g], s_sel[g], values(c, vs_t, g)) for g in kv_groups)
        win = tuple(_online_step_t(win[g], s_win[g], values(c, vw_t, g)) for g in kv_groups)
        return sel, win

    inits = (init,) * NSA_KV_HEADS
    sel = lax.fori_loop(0, first_w, early, inits)
    sel, win = lax.fori_loop(first_w, last + 1, late, (sel, inits))

    for g in kv_groups:
        o_s = sel[g][2] / sel[g][1]
        o_w = win[g][2] / win[g][1]
        for r in range(G):
            h = g * G + r
            cols = slice(r * tq, (r + 1) * tq)
            c = 8 + h
            o_t = (gates_t[c:c + 1, :] * o_cmp[g][:, cols] + gates_t[c + 8:c + 9, :] * o_s[:, cols]
                   + gates_t[c + 16:c + 17, :] * o_w[:, cols])
            o_ref[:, h * HD:(h + 1) * HD] = o_t.T


def nsa_prompt(zb, zs, kv_cmp, kv_slc, kv_win, cmp_pos_w, cmp_proj, *, layer, batch, seq, col_q, out_rows=None):
    assert seq % KEY_CHUNK == 0 and seq // SEL_BLOCK <= SEL_COLS
    nq = seq // KEY_TILE
    n_ch = seq // CMP_STRIDE
    ncp = -(-n_ch // LANES) * LANES
    qw = NSA_HEADS * NSA_HD
    pw = cmp_pos_w.reshape(2, CMP_LEN, 2 * NSA_HD)
    return pl.pallas_call(
        functools.partial(_nsa_prompt_body, seq=seq),
        out_shape=jax.ShapeDtypeStruct((out_rows or batch * seq, qw), F32),
        grid=(batch, nq),
        in_specs=[
            pl.BlockSpec((KEY_TILE, qw), lambda b, i: (b * nq + i, col_q // qw)),
            pl.BlockSpec((None, seq * KV_ROW, NSA_HD), lambda b, i: (layer, b, 0)),
            pl.BlockSpec((None, seq * KV_ROW, NSA_HD), lambda b, i: (layer, b, 0)),
            pl.BlockSpec((None, seq * KV_ROW, NSA_HD), lambda b, i: (layer, b, 0)),
            pl.BlockSpec((KEY_TILE, LANES), lambda b, i: (b * nq + i, 0)),
            pl.BlockSpec((2, CMP_LEN, 2 * NSA_HD), lambda b, i: (0, 0, 0)),
            pl.BlockSpec((2, NSA_KV_HEADS, NSA_HD, NSA_HD), lambda b, i: (0, 0, 0, 0)),
        ],
        out_specs=pl.BlockSpec((KEY_TILE, qw), lambda b, i: (b * nq + i, 0)),
        scratch_shapes=[
            pltpu.VMEM((NSA_KV_HEADS, seq, 2 * NSA_HD), BF16),
            pltpu.VMEM((NSA_KV_HEADS, NSA_HD, seq), BF16),
            pltpu.VMEM((NSA_KV_HEADS, seq, 2 * NSA_HD), BF16),
            pltpu.VMEM((NSA_KV_HEADS, NSA_HD, seq), BF16),
            pltpu.VMEM((NSA_KV_HEADS, ncp, 2 * NSA_HD), BF16),
            pltpu.VMEM((NSA_KV_HEADS, NSA_HD, ncp), BF16),
            pltpu.VMEM((n_ch + 8, NSA_HD), F32),
        ],
        compiler_params=_params("parallel", "arbitrary"),
        name="nsa_prompt",
    )(zb, kv_cmp, kv_slc, kv_win, zs, pw, cmp_proj)


def _nsa_sample_body(pt_ref, q_ref, ksn_ref, kwn_ref, gate_ref, wprev_ref, pw_ref, proj_ref, *rest,
                     ts, past, n_pages):
    del pt_ref
    cmp_pages = rest[:n_pages]
    slc_pages = rest[n_pages:2 * n_pages]
    o_ref = rest[2 * n_pages + 1]
    ks_aug, vs, kw_aug, vw, kcmp_aug, vcmp, acc_scr = rest[2 * n_pages + 2:]
    G, HD = NSA_GROUP, NSA_HD
    page = slc_pages[0].shape[0] // 4
    kp = ks_aug.shape[1]
    wprev = wprev_ref.shape[0] // 4
    wp = kw_aug.shape[1]
    win_pos0 = past - wprev
    n_ch = (past + ts) // CMP_STRIDE
    n_cmp = n_ch - 1
    n_sel = -(-(past + ts) // SEL_BLOCK)
    ncp = kcmp_aug.shape[1]
    ch_per_page = page // CMP_STRIDE

    @pl.when(pl.program_id(0) == 0)
    def _constants():
        pos = lax.broadcasted_iota(jnp.int32, (kp, LANES), 0)
        f_sel = _key_features(pos, True).astype(BF16)
        posw = win_pos0 + lax.broadcasted_iota(jnp.int32, (wp, LANES), 0)
        f_win = _key_features(posw, False).astype(BF16)
        for g in range(NSA_KV_HEADS):
            ks_aug[g, :, HD:2 * HD] = f_sel
            kw_aug[g, :, HD:2 * HD] = f_win
        kcmp_aug[...] = jnp.zeros(kcmp_aug.shape, BF16)
        vcmp[...] = jnp.zeros(vcmp.shape, BF16)

    def with_tail(new_rows):
        return jnp.concatenate([new_rows, jnp.zeros((KEY_TILE - ts, HD), F32)], axis=0).astype(BF16)

    for g in range(NSA_KV_HEADS):
        for p in range(n_pages):
            rows = slice(p * page, (p + 1) * page)
            ks_aug[g, rows, 0:HD] = slc_pages[p][pl.ds(g, page, stride=4), :].astype(BF16)
            vs[g, rows, :] = slc_pages[p][pl.ds(2 + g, page, stride=4), :].astype(BF16)
        ks_aug[g, past:past + KEY_TILE, 0:HD] = with_tail(ksn_ref[pl.ds(g, ts, stride=KV_ROW), :])
        vs[g, past:past + KEY_TILE, :] = with_tail(ksn_ref[pl.ds(2 + g, ts, stride=KV_ROW), :])
        kw_aug[g, 0:wprev, 0:HD] = wprev_ref[pl.ds(g, wprev, stride=4), :].astype(BF16)
        vw[g, 0:wprev, :] = wprev_ref[pl.ds(2 + g, wprev, stride=4), :].astype(BF16)
        kw_aug[g, wprev:wprev + KEY_TILE, 0:HD] = with_tail(kwn_ref[pl.ds(g, ts, stride=KV_ROW), :])
        vw[g, wprev:wprev + KEY_TILE, :] = with_tail(kwn_ref[pl.ds(2 + g, ts, stride=KV_ROW), :])

    for kv in range(2):
        for g in range(NSA_KV_HEADS):
            c = kv * NSA_KV_HEADS + g
            cols = slice(g * HD, (g + 1) * HD)
            w0 = jnp.concatenate([pw_ref[kv, 0:CMP_STRIDE, cols]] * ch_per_page, axis=0)
            w1 = jnp.concatenate([pw_ref[kv, CMP_STRIDE:CMP_LEN, cols]] * ch_per_page, axis=0)
            for p in range(n_pages):
                x = cmp_pages[p][pl.ds(c, page, stride=4), :]
                if p + 1 < n_pages:
                    nxt = cmp_pages[p + 1][pl.ds(c, CMP_STRIDE, stride=4), :] * w1[0:CMP_STRIDE]
                else:
                    nxt = jnp.zeros((CMP_STRIDE, HD), F32)
                z = x * w0 + jnp.concatenate([(x * w1)[CMP_STRIDE:], nxt], axis=0)
                acc_scr[c, p * ch_per_page:(p + 1) * ch_per_page, :] = jnp.sum(
                    z.reshape(ch_per_page, CMP_STRIDE, HD), axis=1)
            _finish_compress(acc_scr[c, 0:n_ch, :], proj_ref, kcmp_aug, vcmp, kv, g, n_ch, ncp, False)

    R = G * ts
    gates = jax.nn.sigmoid(gate_ref[...])
    scale = HD ** -0.5

    def t_of(shape):
        return past + jnp.bitwise_and(lax.broadcasted_iota(jnp.int32, shape, 0), ts - 1)

    def softmax_pv(s, v):
        m = jnp.max(s, axis=1, keepdims=True)
        e = jnp.exp(s - m)
        return jnp.dot(e.astype(BF16), v, preferred_element_type=F32) / jnp.sum(e, axis=1, keepdims=True)

    kv_groups = range(NSA_KV_HEADS)
    q_heads = [[(q_ref[:, (g * G + r) * HD:(g * G + r + 1) * HD] * scale).astype(BF16) for r in range(G)]
               for g in kv_groups]
    feats = [[_query_features((ts, LANES), g * G + r) for r in range(G)] for g in kv_groups]
    q_plain = [_stack_heads(q_heads[g], feats[g]) for g in kv_groups]

    idx = lax.broadcasted_iota(jnp.int32, (R, wp), 1)
    dist = t_of((R, wp)) - (win_pos0 + idx)
    win_ok = (idx < wprev + ts) & (dist >= 0) & (dist <= WINDOW)
    s_win = [jnp.where(win_ok, lax.dot_general(q_plain[g], kw_aug[g], NT, preferred_element_type=F32), NEG_MASK)
             for g in kv_groups]
    n = lax.broadcasted_iota(jnp.int32, (R, ncp), 1)
    cmp_ok = (n * CMP_STRIDE + (CMP_LEN - 1) <= t_of((R, ncp))) & (n < n_cmp)
    s_cmp = [lax.dot_general(q_plain[g], kcmp_aug[g], NT, preferred_element_type=F32) for g in kv_groups]
    p_c = [_masked_softmax(s, cmp_ok) for s in s_cmp]
    o_cmp = [jnp.dot(p_c[g].astype(BF16), vcmp[g], preferred_element_type=F32) for g in kv_groups]
    o_w = [softmax_pv(s_win[g], vw[g]) for g in kv_groups]
    psum = [jnp.concatenate([p[0:ts] + p[ts:2 * ts] + p[2 * ts:3 * ts] + p[3 * ts:4 * ts],
                             jnp.zeros((LANES - ts, ncp), F32)], axis=0) for p in p_c]
    bias = [_select_blocks(psum[g], past, n_cmp, n_sel, False)[0:ts] for g in kv_groups]
    q_sel = [_stack_heads(q_heads[g], [f + bias[g] for f in feats[g]]) for g in kv_groups]
    sel_ok = lax.broadcasted_iota(jnp.int32, (R, kp), 1) <= t_of((R, kp))
    s_sel = [jnp.where(sel_ok, lax.dot_general(q_sel[g], ks_aug[g], NT, preferred_element_type=F32), NEG_MASK)
             for g in kv_groups]
    o_s = [softmax_pv(s_sel[g], vs[g]) for g in kv_groups]
    for g in kv_groups:
        _write_gated(o_ref, gates, g, ts, o_cmp[g], o_s[g], o_w[g])


def nsa_sample(zb, zs, kv_slc, kv_win, cache_cmp, cache_slc, state_win, page_table, cmp_pos_w, cmp_proj, y_init, *,
               layer, row0, batch, ts, col_q):
    depth, n_pool, page = cache_cmp.shape[:3]
    n_pages = page_table.shape[1]
    past = n_pages * page
    wprev = state_win.shape[2]
    assert ts & (ts - 1) == 0 and ts <= KEY_TILE and row0 % ts == 0
    assert past % KEY_TILE == 0 and (past + ts) // CMP_STRIDE == past // CMP_STRIDE
    assert page % CMP_STRIDE == 0 and -(-(past + ts) // SEL_BLOCK) <= SEL_COLS and wprev % 16 == 0
    qw = NSA_HEADS * NSA_HD
    n_ch = past // CMP_STRIDE
    ncp = -(-n_ch // LANES) * LANES
    r0 = row0 // ts
    pw = cmp_pos_w.reshape(2, CMP_LEN, 2 * NSA_HD)
    cmp_view = cache_cmp.reshape(depth, n_pool, page * 4, NSA_HD)
    slc_view = cache_slc.reshape(depth, n_pool, page * 4, NSA_HD)
    win_view = state_win.reshape(depth, batch, wprev * 4, NSA_HD)

    def page_map(p):
        return lambda b, pt: (layer, pt[b * n_pages + p], 0, 0)

    in_specs = [
        pl.BlockSpec((ts, qw), lambda b, pt: (r0 + b, col_q // qw)),
        pl.BlockSpec((None, ts * KV_ROW, NSA_HD), lambda b, pt: (layer, b, 0)),
        pl.BlockSpec((None, ts * KV_ROW, NSA_HD), lambda b, pt: (layer, b, 0)),
        pl.BlockSpec((ts, LANES), lambda b, pt: (r0 + b, 0)),
        pl.BlockSpec((None, None, wprev * 4, NSA_HD), lambda b, pt: (layer, b, 0, 0)),
        pl.BlockSpec((2, CMP_LEN, 2 * NSA_HD), lambda b, pt: (0, 0, 0)),
        pl.BlockSpec((2, NSA_KV_HEADS, NSA_HD, NSA_HD), lambda b, pt: (0, 0, 0, 0)),
    ]
    in_specs += [pl.BlockSpec((None, None, page * 4, NSA_HD), page_map(p % n_pages)) for p in range(2 * n_pages)]
    in_specs.append(pl.BlockSpec(memory_space=pl.ANY))
    return pl.pallas_call(
        functools.partial(_nsa_sample_body, ts=ts, past=past, n_pages=n_pages),
        out_shape=jax.ShapeDtypeStruct(y_init.shape, F32),
        grid_spec=pltpu.PrefetchScalarGridSpec(
            num_scalar_prefetch=1,
            grid=(batch,),
            in_specs=in_specs,
            out_specs=pl.BlockSpec((ts, qw), lambda b, pt: (r0 + b, 0)),
            scratch_shapes=[
                pltpu.VMEM((NSA_KV_HEADS, past + KEY_TILE, 2 * NSA_HD), BF16),
                pltpu.VMEM((NSA_KV_HEADS, past + KEY_TILE, NSA_HD), BF16),
                pltpu.VMEM((NSA_KV_HEADS, wprev + KEY_TILE, 2 * NSA_HD), BF16),
                pltpu.VMEM((NSA_KV_HEADS, wprev + KEY_TILE, NSA_HD), BF16),
                pltpu.VMEM((NSA_KV_HEADS, ncp, 2 * NSA_HD), BF16),
                pltpu.VMEM((NSA_KV_HEADS, ncp, NSA_HD), BF16),
                pltpu.VMEM((2 * NSA_KV_HEADS, n_ch, NSA_HD), F32),
            ],
        ),
        input_output_aliases={8 + 2 * n_pages: 0},
        compiler_params=_params("arbitrary"),
        name="nsa_sample",
    )(page_table.reshape(-1), zb, kv_slc, kv_win, zs, win_view, pw, cmp_proj,
      *([cmp_view] * n_pages), *([slc_view] * n_pages), y_init)


def _window_state_body(old_ref, new_ref, o_ref):
    keep = o_ref.shape[1] - new_ref.shape[1]
    o_ref[:, 0:keep, :] = old_ref[:, old_ref.shape[1] - keep:, :]
    o_ref[:, keep:, :] = new_ref[...]


def window_state(state_win, kv_win_new, ts):
    depth, batch, w_old = state_win.shape[:3]
    keep = min(WINDOW, w_old + ts) - ts
    assert keep >= 0 and (keep * KV_ROW) % 8 == 0 and (ts * KV_ROW) % 8 == 0
    bt = math.gcd(batch, 4)
    out = pl.pallas_call(
        _window_state_body,
        out_shape=jax.ShapeDtypeStruct((depth, batch, (keep + ts) * KV_ROW, NSA_HD), F32),
        grid=(depth, batch // bt),
        in_specs=[
            pl.BlockSpec((None, bt, w_old * KV_ROW, NSA_HD), lambda l, b: (l, b, 0, 0)),
            pl.BlockSpec((None, bt, ts * KV_ROW, NSA_HD), lambda l, b: (l, b, 0, 0)),
        ],
        out_specs=pl.BlockSpec((None, bt, (keep + ts) * KV_ROW, NSA_HD), lambda l, b: (l, b, 0, 0)),
        compiler_params=_params("parallel", "parallel"),
        name="window_state",
    )(state_win.reshape(depth, batch, w_old * KV_ROW, NSA_HD), kv_win_new.reshape(depth, batch, ts * KV_ROW, NSA_HD))
    return out.reshape(depth, batch, keep + ts, *state_win.shape[3:])


ZB_NQ, ZB_POOL, ZB_MQ, ZB_MK, ZB_MV, ZB_MO, ZB_END = (0, 1024, 1536, 2048, 2560, 3072, 3584)
W_POOL, W_MI, W_NQ, W_CMP, W_NG, W_END = 0, 2560, 2568, 3592, 5128, 5152


def _split_w_in(w_in_l):
    w = w_in_l.astype(BF16)
    big = jnp.concatenate([w[:, W_NQ:W_CMP], w[:, W_POOL:W_MI], w[:, W_CMP:W_NG]], axis=1)
    small = jnp.concatenate([w[:, W_MI:W_NQ], w[:, W_NG:W_END]], axis=1)
    return big, jnp.pad(small, ((0, 0), (0, LANES - small.shape[1])))


def kernel(x_prompt, x_sample, cache_kv_cmp, cache_kv_slc, state_kv_win, state_pool, state_mlstm_C, state_mlstm_n, state_mlstm_m, page_table, ffn1_norm, ffn1_w_gate, ffn1_w_up, ffn1_w_down, mix_norm, w_in, w_out, pool_w, pool_scale, mlstm_if_bias, mlstm_norm, nsa_cmp_pos_w, nsa_cmp_proj, ffn2_norm, ffn2_w_gate, ffn2_w_up, ffn2_w_down, final_norm):
    bp, tp, d = x_prompt.shape
    bs, ts, _ = x_sample.shape
    depth = w_in.shape[0]
    mp, ms = bp * tp, bs * ts
    m_all = mp + ms
    past_len = page_table.shape[1] * cache_kv_cmp.shape[2]
    pd = pool_scale.shape[1]
    kv_row = (2, NSA_KV_HEADS, NSA_HD)
    zeros = lambda *s: jnp.zeros(s, F32)
    ffn1 = [w.astype(BF16) for w in (ffn1_w_gate, ffn1_w_up, ffn1_w_down)]
    ffn2 = [w.astype(BF16) for w in (ffn2_w_gate, ffn2_w_up, ffn2_w_down)]
    w_out_b = w_out.astype(BF16)
    outs = [[] for _ in range(6)]
    kv_bufs = p_c = s_c = None
    for l in range(depth):
        if l == 0:
            x = ffn_half_step(x_prompt.reshape(mp, d), ffn1_norm[l], *ffn1, l, out_rows=m_all)
            x = ffn_half_step(x_sample.reshape(ms, d), ffn1_norm[l], *ffn1, l, out_row0=mp, y_init=x)
        else:
            x = ffn_half_step(x, ffn1_norm[l], *ffn1, l)
        w_big, w_small = _split_w_in(w_in[l])
        zb, zs, kv_bufs = mix_project(x, mix_norm[l], w_big, w_small, l, depth, mp, kv_bufs)
        cmp_p, slc_p, win_p, cmp_s, slc_s, win_s = kv_bufs
        z_pool = zb[:, ZB_POOL:ZB_MQ]

        y_pool = pool_prompt(zb, pool_w[l], pool_scale[l], batch=bp, seq=tp, col=ZB_POOL, out_rows=m_all)
        y_m, p_c, p_n, p_m = mlstm_mix(
            zb, zs, mlstm_if_bias[l], mlstm_norm[l], zeros(depth, bp, MLSTM_HEADS, MLSTM_HD, MLSTM_HD),
            zeros(bp, MLSTM_HEADS, MLSTM_HD), zeros(bp, MLSTM_HEADS), row0=0, batch=bp, seq=tp, col_q=ZB_MQ,
            layer=l, depth=depth, c_all=p_c, out_rows=m_all)
        y_nsa = nsa_prompt(zb, zs, cmp_p, slc_p, win_p, nsa_cmp_pos_w[l], nsa_cmp_proj[l], layer=l, batch=bp,
                           seq=tp, col_q=ZB_NQ, out_rows=m_all)

        pool_full = jnp.concatenate([zeros(bs, POOL_HALO - POOL_BUF, pd), state_pool[l],
                                     z_pool[mp:].reshape(bs, ts, pd)], axis=1)
        y_pool = pool_sample(pool_full, pool_w[l], pool_scale[l], y_pool, pos0=past_len, row0=mp)
        y_m, s_c, s_n, s_m = mlstm_mix(
            zb, zs, mlstm_if_bias[l], mlstm_norm[l], state_mlstm_C, state_mlstm_n[l], state_mlstm_m[l],
            row0=mp, batch=bs, seq=ts, col_q=ZB_MQ, layer=l, depth=depth, c_all=s_c, y_init=y_m)
        y_nsa = nsa_sample(zb, zs, slc_s, win_s, cache_kv_cmp, cache_kv_slc, state_kv_win, page_table,
                           nsa_cmp_pos_w[l], nsa_cmp_proj[l], y_nsa, layer=l, row0=mp, batch=bs, ts=ts,
                           col_q=ZB_NQ)

        x = out_project(x, y_pool, y_m, y_nsa, w_out_b, l)
        if l < depth - 1:
            x = ffn_half_step(x, ffn2_norm[l], *ffn2, l)
        else:
            y_prompt = ffn_half_step(x, ffn2_norm[l], *ffn2, l, gf=final_norm, rows=mp).reshape(bp, tp, d)
            y_sample = ffn_half_step(x, ffn2_norm[l], *ffn2, l, gf=final_norm, in_row0=mp, rows=ms).reshape(bs, ts, d)

        pool_p = jnp.concatenate([zeros(bp, POOL_BUF, pd), z_pool[:mp].reshape(bp, tp, pd)], axis=1)
        layer_out = (pool_p[:, -POOL_BUF:], p_n, p_m, pool_full[:, -POOL_BUF:], s_n, s_m)
        for acc, a in zip(outs, layer_out):
            acc.append(a)
    p_pool, p_n, p_m, s_pool, s_n, s_m = [jnp.stack(a) for a in outs]
    p_kv = [b.reshape(depth, bp, tp, *kv_row) for b in (cmp_p, slc_p, win_p)]
    s_kv = [b.reshape(depth, bs, ts, *kv_row) for b in (cmp_s, slc_s, win_s)]
    s_kv_win = window_state(state_kv_win, win_s, ts)
    return (y_prompt, y_sample, p_kv[0], p_kv[1], p_kv[2][:, :, tp - min(WINDOW, tp):], p_pool, p_c, p_n, p_m,
            s_kv[0], s_kv[1], s_kv_win, s_pool, s_c, s_n, s_m)
```

```python
import functools
import math

import jax
import jax.numpy as jnp
from jax import lax
from jax.experimental import pallas as pl
from jax.experimental.pallas import tpu as pltpu

F32 = jnp.float32
BF16 = jnp.bfloat16
EPS = 1e-6

VMEM_LIMIT_BYTES = 56 * 1024 * 1024
LANES = 128

POOL_WINDOWS = (2, 4, 8, 16)
POOL_BUF = 15
MLSTM_HEADS = 4
MLSTM_HD = 128
MLSTM_CHUNK = 64
NSA_HD = 128
NSA_HEADS = 8
NSA_KV_HEADS = 2
NSA_GROUP = 4
CMP_LEN = 32
CMP_STRIDE = 16
SEL_BLOCK = 64
SEL_TOPN = 16
WINDOW = 512
FORCE_BONUS = 1.0e4

NEG_MASK = -1.0e30
NT = (((1,), (1,)), ((), ()))
TN = (((0,), (0,)), ((), ()))


def _pick_tile(n, pref):
    t = pref
    while t > 8 and n % t:
        t //= 2
    assert n % t == 0, (n, pref)
    return t


def _params(*sem):
    return pltpu.CompilerParams(dimension_semantics=sem, vmem_limit_bytes=VMEM_LIMIT_BYTES)


def _drop_alias_ref(body, index, *refs):
    return body(*refs[:index], *refs[index + 1:])


def _rms_rows(x, g):
    ms = jnp.mean(x * x, axis=-1, keepdims=True)
    return x * lax.rsqrt(ms + EPS) * g


def _ffn_body(x_ref, g_ref, wg_ref, wu_ref, wd_ref, gf_ref, o_ref, n_scr, *, final_norm):
    f = pl.program_id(1)

    @pl.when(f == 0)
    def _():
        x = x_ref[...]
        n_scr[...] = _rms_rows(x, g_ref[...]).astype(BF16)
        o_ref[...] = x

    n = n_scr[...]
    hg = jnp.dot(n, wg_ref[...], preferred_element_type=F32)
    hu = jnp.dot(n, wu_ref[...], preferred_element_type=F32)
    h = (hg * jax.nn.sigmoid(hg) * hu).astype(BF16)
    o_ref[...] += 0.5 * jnp.dot(h, wd_ref[...], preferred_element_type=F32)

    if final_norm:
        @pl.when(f == pl.num_programs(1) - 1)
        def _():
            o_ref[...] = _rms_rows(o_ref[...], gf_ref[...])


def ffn_half_step(x, g, wg, wu, wd, layer, gf=None, *, in_row0=0, rows=None, out_row0=0, y_init=None):
    d = x.shape[1]
    rows = x.shape[0] if rows is None else rows
    fdim = wg.shape[2]
    tm = _pick_tile(math.gcd(math.gcd(rows, in_row0), out_row0), 512)
    tf = _pick_tile(fdim, 512)
    final_norm = gf is not None
    if gf is None:
        gf = g
    i0, o0 = in_row0 // tm, out_row0 // tm
    body = functools.partial(_ffn_body, final_norm=final_norm)
    extra_specs, extra_args, aliases = [], [], {}
    if y_init is not None:
        body = functools.partial(_drop_alias_ref, body, 6)
        extra_specs, extra_args, aliases = [pl.BlockSpec(memory_space=pl.ANY)], [y_init], {6: 0}
    return pl.pallas_call(
        body,
        out_shape=jax.ShapeDtypeStruct((rows, d) if y_init is None else y_init.shape, F32),
        grid=(rows // tm, fdim // tf),
        in_specs=[
            pl.BlockSpec((tm, d), lambda i, f: (i0 + i, 0)),
            pl.BlockSpec((1, d), lambda i, f: (0, 0)),
            pl.BlockSpec((None, d, tf), lambda i, f: (layer, 0, f)),
            pl.BlockSpec((None, d, tf), lambda i, f: (layer, 0, f)),
            pl.BlockSpec((None, tf, d), lambda i, f: (layer, f, 0)),
            pl.BlockSpec((1, d), lambda i, f: (0, 0)),
        ] + extra_specs,
        out_specs=pl.BlockSpec((tm, d), lambda i, f: (o0 + i, 0)),
        scratch_shapes=[pltpu.VMEM((tm, d), BF16)],
        input_output_aliases=aliases,
        compiler_params=_params("parallel", "arbitrary"),
        name="ffn_half_step",
    )(x, g.reshape(1, d), wg, wu, wd, gf.reshape(1, d), *extra_args)


KV_SLABS = 3
KV_ROW = 4


def _inproj_body(x_ref, g_ref, w_ref, ws_ref, *rest, main_cols, n_prompt_tiles, n_aliased):
    zb_ref, zs_ref, *kv_refs, n_scr = rest[n_aliased:]
    i, j = pl.program_id(0), pl.program_id(1)
    tm, tn = x_ref.shape[0], w_ref.shape[1]
    slab_w = KV_ROW * LANES

    @pl.when(j == 0)
    def _():
        n_scr[...] = _rms_rows(x_ref[...], g_ref[...]).astype(BF16)
        zs_ref[...] = jnp.dot(n_scr[...], ws_ref[...], preferred_element_type=F32)

    z = jnp.dot(n_scr[...], w_ref[...], preferred_element_type=F32)

    @pl.when(j * tn < main_cols)
    def _():
        zb_ref[...] = z

    def scatter(ref, off):
        for c in range(KV_ROW):
            ref[pl.ds(c, tm, stride=KV_ROW), :] = z[:, off + c * LANES:off + (c + 1) * LANES]

    for k in range(KV_SLABS):
        jt, off = divmod(main_cols + k * slab_w, tn)
        pl.when((j == jt) & (i < n_prompt_tiles))(functools.partial(scatter, kv_refs[k], off))
        pl.when((j == jt) & (i >= n_prompt_tiles))(functools.partial(scatter, kv_refs[KV_SLABS + k], off))


def mix_project(x, g, w, w_small, layer, mp, kv_bufs):
    m, d = x.shape
    n = w.shape[1]
    tn = _pick_tile(n, 1024)
    ms = m - mp
    tm = _pick_tile(math.gcd(mp, ms), 512)
    main_cols = n - KV_SLABS * KV_ROW * LANES
    n_main = -(-main_cols // tn)
    npt = mp // tm
    out_shape = [jax.ShapeDtypeStruct((m, n_main * tn), F32), jax.ShapeDtypeStruct((m, LANES), F32)]
    out_shape += [jax.ShapeDtypeStruct(b.shape, F32) for b in kv_bufs]
    out_specs = [pl.BlockSpec((tm, tn), lambda i, j: (i, jnp.minimum(j, n_main - 1))),
                 pl.BlockSpec((tm, LANES), lambda i, j: (i, 0))]
    out_specs += [pl.BlockSpec((None, tm * KV_ROW, LANES), lambda i, j: (layer, jnp.minimum(i, npt - 1), 0))] * KV_SLABS
    out_specs += [pl.BlockSpec((None, tm * KV_ROW, LANES), lambda i, j: (layer, jnp.maximum(i - npt, 0), 0))] * KV_SLABS
    in_specs = [
        pl.BlockSpec((tm, d), lambda i, j: (i, 0)),
        pl.BlockSpec((1, d), lambda i, j: (0, 0)),
        pl.BlockSpec((d, tn), lambda i, j: (0, j)),
        pl.BlockSpec((d, LANES), lambda i, j: (0, 0)),
    ]
    args = [x, g.reshape(1, d), w, w_small]
    in_specs += [pl.BlockSpec(memory_space=pl.ANY)] * len(kv_bufs)
    aliases = {len(args) + k: 2 + k for k in range(len(kv_bufs))}
    args += list(kv_bufs)
    zb, zs, *bufs = pl.pallas_call(
        functools.partial(_inproj_body, main_cols=main_cols, n_prompt_tiles=npt, n_aliased=len(aliases)),
        out_shape=out_shape,
        grid=(m // tm, n // tn),
        in_specs=in_specs,
        out_specs=out_specs,
        scratch_shapes=[pltpu.VMEM((tm, d), BF16)],
        input_output_aliases=aliases,
        compiler_params=_params("arbitrary", "arbitrary"),
        name="mix_project",
    )(*args)
    return zb, zs, bufs


def _outproj_body(x_ref, ya_ref, yb_ref, yc_ref, wa_ref, wb_ref, wc_ref, o_ref):
    acc = x_ref[...]
    acc += jnp.dot(ya_ref[...].astype(BF16), wa_ref[...], preferred_element_type=F32)
    acc += jnp.dot(yb_ref[...].astype(BF16), wb_ref[...], preferred_element_type=F32)
    acc += jnp.dot(yc_ref[...].astype(BF16), wc_ref[...], preferred_element_type=F32)
    o_ref[...] = acc


def out_project(x, y_pool, y_mlstm, y_nsa, w_out, layer):
    m, d = x.shape
    da, db, dc = y_pool.shape[1], y_mlstm.shape[1], y_nsa.shape[1]
    assert da == db and dc % da == 0
    tm = _pick_tile(m, 512)
    tn = d
    return pl.pallas_call(
        _outproj_body,
        out_shape=jax.ShapeDtypeStruct((m, d), F32),
        grid=(m // tm, d // tn),
        in_specs=[
            pl.BlockSpec((tm, tn), lambda i, j: (i, j)),
            pl.BlockSpec((tm, da), lambda i, j: (i, 0)),
            pl.BlockSpec((tm, db), lambda i, j: (i, 0)),
            pl.BlockSpec((tm, dc), lambda i, j: (i, 0)),
            pl.BlockSpec((None, da, tn), lambda i, j: (layer, 0, j)),
            pl.BlockSpec((None, db, tn), lambda i, j: (layer, 1, j)),
            pl.BlockSpec((None, dc, tn), lambda i, j: (layer, (da + db) // dc, j)),
        ],
        out_specs=pl.BlockSpec((tm, tn), lambda i, j: (i, j)),
        compiler_params=_params("parallel", "arbitrary"),
        name="out_project",
    )(x, y_pool, y_mlstm, y_nsa, w_out, w_out, w_out)


POOL_HALO = 16


def _pool_group(load, g, n_avail, w_ref, sc_ref):
    w = POOL_WINDOWS[g]
    z = load(0)
    acc = z
    for j in range(1, w):
        acc = acc + load(j)
    d = acc / jnp.minimum(n_avail, w).astype(F32) - z
    lead = d.shape[:-1]
    gd = d.shape[-1]
    y = jnp.dot(d.reshape(-1, gd).astype(BF16), w_ref[g].astype(BF16), preferred_element_type=F32)
    return (y * sc_ref[:, g * gd:(g + 1) * gd]).reshape(*lead, gd)


def _pool_prompt_body(z_ref, w_ref, sc_ref, o_ref, full_scr, *, chunk):
    seq, pd = z_ref.shape
    gd = pd // len(POOL_WINDOWS)
    full_scr[0:POOL_HALO, :] = jnp.zeros((POOL_HALO, pd), F32)
    full_scr[POOL_HALO:POOL_HALO + seq, :] = z_ref[...]
    for c in range(seq // chunk):
        n_avail = c * chunk + 1 + lax.broadcasted_iota(jnp.int32, (chunk, gd), 0)
        for g in range(len(POOL_WINDOWS)):
            load = lambda j: full_scr[pl.ds(POOL_HALO + c * chunk - j, chunk), g * gd:(g + 1) * gd]
            o_ref[c * chunk:(c + 1) * chunk, g * gd:(g + 1) * gd] = _pool_group(load, g, n_avail, w_ref, sc_ref)


def pool_prompt(zb, pool_w, pool_scale, y_init, *, batch, seq, col):
    pd = pool_scale.shape[0]
    chunk = _pick_tile(seq, 256)
    return pl.pallas_call(
        functools.partial(_drop_alias_ref, functools.partial(_pool_prompt_body, chunk=chunk), 3),
        out_shape=jax.ShapeDtypeStruct(y_init.shape, F32),
        grid=(batch,),
        in_specs=[
            pl.BlockSpec((seq, pd), lambda b: (b, col // pd)),
            pl.BlockSpec(pool_w.shape, lambda b: (0, 0, 0)),
            pl.BlockSpec((1, pd), lambda b: (0, 0)),
            pl.BlockSpec(memory_space=pl.ANY),
        ],
        out_specs=pl.BlockSpec((seq, pd), lambda b: (b, 0)),
        scratch_shapes=[pltpu.VMEM((POOL_HALO + seq, pd), F32)],
        input_output_aliases={3: 0},
        compiler_params=_params("parallel"),
        name="pool_prompt",
    )(zb, pool_w, pool_scale.reshape(1, pd), y_init)


def _pool_sample_body(full_ref, w_ref, sc_ref, o_ref, *, pos0):
    bt, rows, pd = full_ref.shape
    ts = rows - POOL_HALO
    gd = pd // len(POOL_WINDOWS)
    n_avail = pos0 + 1 + lax.broadcasted_iota(jnp.int32, (bt, ts, gd), 1)
    for g in range(len(POOL_WINDOWS)):
        load = lambda j: full_ref[:, pl.ds(POOL_HALO - j, ts), g * gd:(g + 1) * gd]
        o_ref[:, g * gd:(g + 1) * gd] = _pool_group(load, g, n_avail, w_ref, sc_ref).reshape(bt * ts, gd)


def pool_sample(full, pool_w, pool_scale, y_init, *, pos0, row0):
    batch, rows, pd = full.shape
    ts = rows - POOL_HALO
    bt = math.gcd(batch, 32)
    assert ts % 8 == 0 and row0 % (bt * ts) == 0
    r0 = row0 // (bt * ts)
    return pl.pallas_call(
        functools.partial(_drop_alias_ref, functools.partial(_pool_sample_body, pos0=pos0), 3),
        out_shape=jax.ShapeDtypeStruct(y_init.shape, F32),
        grid=(batch // bt,),
        in_specs=[
            pl.BlockSpec((bt, rows, pd), lambda b: (b, 0, 0)),
            pl.BlockSpec(pool_w.shape, lambda b: (0, 0, 0)),
            pl.BlockSpec((1, pd), lambda b: (0, 0)),
            pl.BlockSpec(memory_space=pl.ANY),
        ],
        out_specs=pl.BlockSpec((bt * ts, pd), lambda b: (r0 + b, 0)),
        input_output_aliases={3: 0},
        compiler_params=_params("parallel"),
        name="pool_sample",
    )(full, pool_w, pool_scale.reshape(1, pd), y_init)


def _log_sigmoid(x):
    return jnp.minimum(x, 0.0) - jnp.log1p(jnp.exp(-jnp.abs(x)))


def _mlstm_body(q_ref, k_ref, v_ref, og_ref, g_ref, bias_ref, gn_ref, c0_ref, n0_ref, m0_ref,
                y_ref, c_ref, n_ref, m_ref, *, L):
    nseq = q_ref.shape[0] // L
    H, D = MLSTM_HEADS, MLSTM_HD
    hi = lax.Precision.HIGHEST

    @pl.when(pl.program_id(1) == 0)
    def _():
        c_ref[...] = c0_ref[...]
        n_ref[...] = n0_ref[...]
        m_ref[...] = m0_ref[...]

    sel = (lax.broadcasted_iota(jnp.int32, (8, LANES), 0) == lax.broadcasted_iota(jnp.int32, (8, LANES), 1)).astype(F32)
    li = lax.broadcasted_iota(jnp.int32, (L, L), 0)
    si = lax.broadcasted_iota(jnp.int32, (L, L), 1)
    causal = li >= si
    lane = lax.broadcasted_iota(jnp.int32, (1, LANES), 1)
    chains = [(s, h) for s in range(nseq) for h in range(H)]

    gates = []
    for s in range(nseq):
        rows = slice(s * L, (s + 1) * L)
        gz = g_ref[rows, :] + bias_ref[...]
        gz_rows = lax.dot_general(sel, gz, NT, precision=hi, preferred_element_type=F32)
        b_cols = jnp.dot(causal.astype(F32), _log_sigmoid(gz), precision=hi, preferred_element_type=F32)
        b_rows = jnp.dot(_log_sigmoid(gz_rows), (li <= si).astype(F32), precision=hi, preferred_element_type=F32)
        gates.append((gz, gz_rows, b_cols, b_rows, m_ref[s]))

    qk, qc, state = {}, {}, {}
    for s, h in chains:
        rows, cols = slice(s * L, (s + 1) * L), slice(h * D, (h + 1) * D)
        qh = q_ref[rows, cols]
        kh = k_ref[rows, cols] * (D ** -0.5)
        ch = c_ref[s, h]
        nh = n_ref[s, h:h + 1, :]
        qb, kb = qh.astype(BF16), kh.astype(BF16)
        qk[s, h] = lax.dot_general(qb, kb, NT, preferred_element_type=F32)
        qc[s, h] = lax.dot_general(qb, ch.astype(BF16), NT, preferred_element_type=F32)
        state[s, h] = (qh, kh, kb, ch, nh)

    sm, stats = {}, {}
    for s, h in chains:
        gz, gz_rows, b_cols, b_rows, m_all = gates[s]
        bc = b_cols[:, H + h:H + h + 1]
        ic = gz[:, h:h + 1]
        br = b_rows[H + h:H + h + 1, :]
        ir = gz_rows[h:h + 1, :]
        m_prev = m_all[:, h:h + 1]
        dmat = jnp.where(causal, bc - br + ir, NEG_MASK)
        inter = bc + m_prev
        m_t = jnp.maximum(inter, jnp.max(dmat, axis=1, keepdims=True))
        sm[s, h] = qk[s, h] * jnp.exp(dmat - m_t)
        m_new = m_t[L - 1:L, :]
        b_last = bc[L - 1:L, :]
        stats[s, h] = (jnp.exp(inter - m_t), m_t, m_new, jnp.exp(b_last + m_prev - m_new),
                       jnp.exp(b_last - bc + ic - m_new))

    num, c_new = {}, {}
    for s, h in chains:
        rows, cols = slice(s * L, (s + 1) * L), slice(h * D, (h + 1) * D)
        a_inter, _, _, decay, w_col = stats[s, h]
        qh, kh, kb, ch, nh = state[s, h]
        vh = v_ref[rows, cols]
        num[s, h] = jnp.dot(sm[s, h].astype(BF16), vh.astype(BF16), preferred_element_type=F32) + a_inter * qc[s, h]
        c_new[s, h] = decay * ch + lax.dot_general((vh * w_col).astype(BF16), kb, TN, preferred_element_type=F32)

    m_out = [gates[s][4] for s in range(nseq)]
    for s, h in chains:
        rows, cols = slice(s * L, (s + 1) * L), slice(h * D, (h + 1) * D)
        a_inter, m_t, m_new, decay, w_col = stats[s, h]
        qh, kh, kb, ch, nh = state[s, h]
        den = jnp.sum(sm[s, h], axis=1, keepdims=True) + a_inter * jnp.sum(qh * nh, axis=1, keepdims=True)
        den = jnp.maximum(jnp.abs(den), jnp.exp(-m_t))
        hh = num[s, h] / den
        mu = jnp.mean(hh, axis=1, keepdims=True)
        var = jnp.mean(jnp.square(hh - mu), axis=1, keepdims=True)
        hn = (hh - mu) * lax.rsqrt(var + EPS) * gn_ref[:, cols]
        y_ref[rows, cols] = jax.nn.sigmoid(og_ref[rows, cols]) * hn
        c_ref[s, h] = c_new[s, h]
        n_ref[s, h:h + 1, :] = decay * nh + jnp.sum(kh * w_col, axis=0, keepdims=True)
        m_out[s] = jnp.where(lane == h, m_new, m_out[s])
    for s in range(nseq):
        m_ref[s] = m_out[s]


def mlstm_mix(zb, zs, if_bias, mnorm, c0, n0, m0, c_all, y_init, *, row0, batch, seq, col_q, layer):
    H, D = MLSTM_HEADS, MLSTM_HD
    dim = H * D
    L = math.gcd(seq, MLSTM_CHUNK)
    nc = seq // L
    nseq = math.gcd(batch, min(4, MLSTM_CHUNK // L)) if nc == 1 else 1
    rows = nseq * L
    assert L % 8 == 0 and row0 % rows == 0 and col_q % dim == 0
    r0 = row0 // rows
    cq = col_q // dim
    bias = jnp.pad(if_bias, (0, LANES - if_bias.shape[0])).reshape(1, LANES)
    m0p = jnp.pad(m0, ((0, 0), (0, LANES - H))).reshape(batch, 1, LANES)
    row = lambda b, c: r0 + b * nc + c
    in_specs = [
        pl.BlockSpec((rows, dim), lambda b, c: (row(b, c), cq)),
        pl.BlockSpec((rows, dim), lambda b, c: (row(b, c), cq + 1)),
        pl.BlockSpec((rows, dim), lambda b, c: (row(b, c), cq + 2)),
        pl.BlockSpec((rows, dim), lambda b, c: (row(b, c), cq + 3)),
        pl.BlockSpec((rows, LANES), lambda b, c: (row(b, c), 0)),
        pl.BlockSpec((1, LANES), lambda b, c: (0, 0)),
        pl.BlockSpec((1, dim), lambda b, c: (0, 0)),
        pl.BlockSpec((None, nseq, H, D, D), lambda b, c: (layer, b, 0, 0, 0)),
        pl.BlockSpec((nseq, H, D), lambda b, c: (b, 0, 0)),
        pl.BlockSpec((nseq, 1, LANES), lambda b, c: (b, 0, 0)),
    ]
    args = [zb, zb, zb, zb, zs, bias, mnorm.reshape(1, dim), c0, n0, m0p]
    aliases = {}
    body = functools.partial(_mlstm_body, L=L)
    for out_index, init in ((1, c_all), (0, y_init)):
        in_specs.append(pl.BlockSpec(memory_space=pl.ANY))
        args.append(init)
        aliases[len(args) - 1] = out_index
        body = functools.partial(_drop_alias_ref, body, len(args) - 1)
    y, c_out, n_out, m_out = pl.pallas_call(
        body,
        out_shape=(jax.ShapeDtypeStruct(y_init.shape, F32),
                   jax.ShapeDtypeStruct(c_all.shape, F32),
                   jax.ShapeDtypeStruct((batch, H, D), F32),
                   jax.ShapeDtypeStruct((batch, 1, LANES), F32)),
        grid=(batch // nseq, nc),
        in_specs=in_specs,
        out_specs=(
            pl.BlockSpec((rows, dim), lambda b, c: (r0 + b * nc + c, 0)),
            pl.BlockSpec((None, nseq, H, D, D), lambda b, c: (layer, b, 0, 0, 0)),
            pl.BlockSpec((nseq, H, D), lambda b, c: (b, 0, 0)),
            pl.BlockSpec((nseq, 1, LANES), lambda b, c: (b, 0, 0)),
        ),
        input_output_aliases=aliases,
        compiler_params=_params("parallel", "arbitrary"),
        name="mlstm_mix",
    )(*args)
    return y, c_out, n_out, m_out[:, 0, :H]


KEY_TILE = 128
KEY_CHUNK = 256
SEL_COLS = 64
POS_HI, POS_LO = SEL_COLS, SEL_COLS + 1
NEG_SEL = -1.0e9


def _slope(h):
    return 2.0 ** (-(8.0 / NSA_HEADS) * (h + 1))


def _key_features(pos, onehot):
    lane = lax.broadcasted_iota(jnp.int32, pos.shape, 1)
    hi = lax.shift_right_logical(pos, 6)
    lo = jnp.bitwise_and(pos, SEL_BLOCK - 1)
    f = jnp.where(lane == POS_HI, hi.astype(F32), jnp.where(lane == POS_LO, lo.astype(F32), 0.0))
    if onehot:
        f = jnp.where(lane == hi, 1.0, f)
    return f


def _query_features(shape, h):
    lane = lax.broadcasted_iota(jnp.int32, shape, 1)
    return jnp.where(lane == POS_HI, SEL_BLOCK * _slope(h), jnp.where(lane == POS_LO, _slope(h), 0.0))


def _compress_block_rows(load, pw_ref, kv, g):
    cols = slice(g * NSA_HD, (g + 1) * NSA_HD)
    a0 = a1 = None
    for j in range(CMP_STRIDE):
        rows = load(j)
        t0 = rows * pw_ref[kv, j:j + 1, cols]
        t1 = rows * pw_ref[kv, CMP_STRIDE + j:CMP_STRIDE + j + 1, cols]
        a0 = t0 if a0 is None else a0 + t0
        a1 = t1 if a1 is None else a1 + t1
    return a0, a1


def _finish_compress(acc, proj_ref, kcmp_aug, vcmp, kv, g, n_ch, ncp, v_transposed):
    c = jnp.dot(acc.astype(BF16), proj_ref[kv, g].astype(BF16), preferred_element_type=F32)
    if kv == 0:
        kcmp_aug[g, 0:n_ch, 0:NSA_HD] = c.astype(BF16)
        n = lax.broadcasted_iota(jnp.int32, (ncp, LANES), 0)
        kcmp_aug[g, :, NSA_HD:2 * NSA_HD] = _key_features(n * CMP_STRIDE + (CMP_LEN - 1), False).astype(BF16)
    elif v_transposed:
        if ncp > n_ch:
            c = jnp.concatenate([c, jnp.zeros((ncp - n_ch, NSA_HD), F32)], axis=0)
        for blk in range(ncp // LANES):
            vcmp[g, :, blk * LANES:(blk + 1) * LANES] = c[blk * LANES:(blk + 1) * LANES].T.astype(BF16)
    else:
        vcmp[g, 0:n_ch, :] = c.astype(BF16)


def _masked_softmax(s, mask):
    s = jnp.where(mask, s, NEG_MASK)
    m = jnp.max(s, axis=1, keepdims=True)
    e = jnp.where(mask, jnp.exp(s - m), 0.0)
    return e / jnp.maximum(jnp.sum(e, axis=1, keepdims=True), 1e-30)


def _select_blocks(psum, t0, n_cmp, n_sel, queries_on_lanes):
    ncp = psum.shape[0] if queries_on_lanes else psum.shape[1]
    nsp = -(-n_sel // 8) * 8
    j = lax.broadcasted_iota(jnp.int32, (nsp, ncp), 0)
    n = lax.broadcasted_iota(jnp.int32, (nsp, ncp), 1)
    cover = ((n * CMP_STRIDE < j * SEL_BLOCK + SEL_BLOCK) & (n * CMP_STRIDE + CMP_LEN > j * SEL_BLOCK)
             & (n < n_cmp)).astype(F32)
    if queries_on_lanes:
        imp = jnp.dot(cover, psum, precision=lax.Precision.HIGHEST, preferred_element_type=F32)
    else:
        imp = lax.dot_general(cover, psum, NT, precision=lax.Precision.HIGHEST, preferred_element_type=F32)
    jq = lax.broadcasted_iota(jnp.int32, (nsp, LANES), 0)
    t = t0 + lax.broadcasted_iota(jnp.int32, (nsp, LANES), 1)
    cur = lax.shift_right_logical(t, 6)
    forced = (jq == 0) | (jq == cur) | (jq == cur - 1)
    valid = (jq * SEL_BLOCK <= t) & (jq < n_sel)
    score = jnp.where(valid, imp + jnp.where(forced, FORCE_BONUS, 0.0), -jnp.inf)
    jf = jq.astype(F32)
    sel = jnp.zeros((nsp, LANES), F32)
    for _ in range(min(SEL_TOPN, n_sel)):
        mx = jnp.max(score, axis=0, keepdims=True)
        first = jnp.min(jnp.where(score == mx, jf, 1.0e9), axis=0, keepdims=True)
        pick = jf == first
        sel = jnp.where(pick, 1.0, sel)
        score = jnp.where(pick, -jnp.inf, score)
    bias = jnp.where((sel > 0.5) | (jq >= n_sel), 0.0, NEG_SEL)
    bias = jnp.concatenate([bias, jnp.zeros((LANES - nsp, LANES), F32)], axis=0)
    return bias.T


def _online_step_t(state, s_t, v_t):
    m, l, acc = state
    m_new = jnp.maximum(m, jnp.max(s_t, axis=0, keepdims=True))
    alpha = jnp.exp(m - m_new)
    p = jnp.exp(s_t - m_new)
    l = alpha * l + jnp.sum(p, axis=0, keepdims=True)
    acc = alpha * acc + jnp.dot(v_t, p.astype(BF16), preferred_element_type=F32)
    return m_new, l, acc


def _stack_heads(q_heads, feats):
    return jnp.concatenate(
        [jnp.concatenate([q, f.astype(BF16)], axis=1) for q, f in zip(q_heads, feats)], axis=0)


def _write_gated(o_ref, gates, g, rows, o_cmp, o_s, o_w):
    for r in range(NSA_GROUP):
        h = g * NSA_GROUP + r
        sl = slice(r * rows, (r + 1) * rows)
        c = 8 + h
        o = gates[:, c:c + 1] * o_cmp[sl] + gates[:, c + 8:c + 9] * o_s[sl] + gates[:, c + 16:c + 17] * o_w[sl]
        o_ref[:, h * NSA_HD:(h + 1) * NSA_HD] = o


def _nsa_prompt_body(q_ref, kc_ref, ks_ref, kw_ref, gate_ref, pw_ref, proj_ref, o_ref,
                     ks_aug, vs_t, kw_aug, vw_t, kcmp_aug, vcmp_t, a1_scr, *, seq):
    i = pl.program_id(1)
    tq = KEY_TILE
    n_ch = seq // CMP_STRIDE
    n_cmp = n_ch - 1
    n_sel = seq // SEL_BLOCK
    ncp = kcmp_aug.shape[1]
    G, HD = NSA_GROUP, NSA_HD
    Q = G * tq

    @pl.when(i == 0)
    def _build():
        pos = lax.broadcasted_iota(jnp.int32, (seq, LANES), 0)
        f_sel = _key_features(pos, True).astype(BF16)
        f_win = _key_features(pos, False).astype(BF16)
        for g in range(NSA_KV_HEADS):
            ks_aug[g, :, 0:HD] = ks_ref[pl.ds(g, seq, stride=KV_ROW), :].astype(BF16)
            ks_aug[g, :, HD:2 * HD] = f_sel
            kw_aug[g, :, 0:HD] = kw_ref[pl.ds(g, seq, stride=KV_ROW), :].astype(BF16)
            kw_aug[g, :, HD:2 * HD] = f_win
            for kt in range(seq // KEY_TILE):
                rows = slice(kt * KEY_TILE, (kt + 1) * KEY_TILE)
                src_rows = pl.ds(kt * KEY_TILE * KV_ROW + 2 + g, KEY_TILE, stride=KV_ROW)
                vs_t[g, :, rows] = ks_ref[src_rows, :].T.astype(BF16)
                vw_t[g, :, rows] = kw_ref[src_rows, :].T.astype(BF16)
        kcmp_aug[...] = jnp.zeros(kcmp_aug.shape, BF16)
        a1_scr[n_ch:n_ch + 8, :] = jnp.zeros((8, HD), F32)
        for kv in range(2):
            for g in range(NSA_KV_HEADS):
                c = kv * NSA_KV_HEADS + g
                a0, a1 = _compress_block_rows(
                    lambda j: kc_ref[pl.ds(j * KV_ROW + c, n_ch, stride=KV_ROW * CMP_STRIDE), :], pw_ref, kv, g)
                a1_scr[0:n_ch, :] = a1
                _finish_compress(a0 + a1_scr[pl.ds(1, n_ch), :], proj_ref, kcmp_aug, vcmp_t, kv, g, n_ch, ncp, True)

    t0 = i * tq
    key = lax.broadcasted_iota(jnp.int32, (KEY_CHUNK, Q), 0)
    t_cols = t0 + jnp.bitwise_and(lax.broadcasted_iota(jnp.int32, (KEY_CHUNK, Q), 1), tq - 1)
    gates_t = jax.nn.sigmoid(gate_ref[...]).T
    scale = HD ** -0.5

    n = lax.broadcasted_iota(jnp.int32, (ncp, Q), 0)
    t_c = t0 + jnp.bitwise_and(lax.broadcasted_iota(jnp.int32, (ncp, Q), 1), tq - 1)
    mask = (n * CMP_STRIDE + (CMP_LEN - 1) <= t_c) & (n < n_cmp)
    kv_groups = range(NSA_KV_HEADS)
    q_heads = [[(q_ref[:, (g * G + r) * HD:(g * G + r + 1) * HD] * scale).astype(BF16) for r in range(G)]
               for g in kv_groups]
    feats = [[_query_features((tq, LANES), g * G + r) for r in range(G)] for g in kv_groups]
    q_plain = [_stack_heads(q_heads[g], feats[g]) for g in kv_groups]
    s_c = [jnp.where(mask, lax.dot_general(kcmp_aug[g], q_plain[g], NT, preferred_element_type=F32), NEG_MASK)
           for g in kv_groups]
    e_c = [jnp.where(mask, jnp.exp(s - jnp.max(s, axis=0, keepdims=True)), 0.0) for s in s_c]
    p_c = [e / jnp.maximum(jnp.sum(e, axis=0, keepdims=True), 1e-30) for e in e_c]
    o_cmp = [jnp.dot(vcmp_t[g], p_c[g].astype(BF16), preferred_element_type=F32) for g in kv_groups]
    psum = [p[:, 0:tq] + p[:, tq:2 * tq] + p[:, 2 * tq:3 * tq] + p[:, 3 * tq:4 * tq] for p in p_c]
    bias = [_select_blocks(psum[g], t0, n_cmp, n_sel, True) for g in kv_groups]
    q_sel = [_stack_heads(q_heads[g], [f + bias[g] for f in feats[g]]) for g in kv_groups]

    last = t0 // KEY_CHUNK
    first_w = jnp.maximum(t0 - WINDOW, 0) // KEY_CHUNK
    init = (jnp.full((1, Q), NEG_MASK, F32), jnp.zeros((1, Q), F32), jnp.zeros((HD, Q), F32))

    def scores(c, k_aug, q, g, valid):
        off = pl.multiple_of(c * KEY_CHUNK, KEY_CHUNK)
        s_t = lax.dot_general(k_aug[g, pl.ds(off, KEY_CHUNK), :], q, NT, preferred_element_type=F32)
        return s_t if valid is None else jnp.where(valid(off + key), s_t, NEG_MASK)

    def values(c, v_t, g):
        return v_t[g, :, pl.ds(pl.multiple_of(c * KEY_CHUNK, KEY_CHUNK), KEY_CHUNK)]

    def causal(kpos):
        return kpos <= t_cols

    def band(kpos):
        return (kpos <= t_cols) & (t_cols - kpos <= WINDOW)

    def early(c, sel):
        s = [scores(c, ks_aug, q_sel[g], g, None) for g in kv_groups]
        return tuple(_online_step_t(sel[g], s[g], values(c, vs_t, g)) for g in kv_groups)

    def late(c, sts):
        sel, win = sts
        s_sel = [scores(c, ks_aug, q_sel[g], g, causal) for g in kv_groups]
        s_win = [scores(c, kw_aug, q_plain[g], g, band) for g in kv_groups]
        sel = tuple(_online_step_t(sel[g], s_sel[g], values(c, vs_t, g)) for g in kv_groups)
        win = tuple(_online_step_t(win[g], s_win[g], values(c, vw_t, g)) for g in kv_groups)
        return sel, win

    inits = (init,) * NSA_KV_HEADS
    sel = lax.fori_loop(0, first_w, early, inits)
    sel, win = lax.fori_loop(first_w, last + 1, late, (sel, inits))

    for g in kv_groups:
        o_s = sel[g][2] / sel[g][1]
        o_w = win[g][2] / win[g][1]
        for r in range(G):
            h = g * G + r
            cols = slice(r * tq, (r + 1) * tq)
            c = 8 + h
            o_t = (gates_t[c:c + 1, :] * o_cmp[g][:, cols] + gates_t[c + 8:c + 9, :] * o_s[:, cols]
                   + gates_t[c + 16:c + 17, :] * o_w[:, cols])
            o_ref[:, h * HD:(h + 1) * HD] = o_t.T


def nsa_prompt(zb, zs, kv_cmp, kv_slc, kv_win, cmp_pos_w, cmp_proj, y_init, *, layer, batch, seq, col_q):
    assert seq % KEY_CHUNK == 0 and seq // SEL_BLOCK <= SEL_COLS
    nq = seq // KEY_TILE
    n_ch = seq // CMP_STRIDE
    ncp = -(-n_ch // LANES) * LANES
    qw = NSA_HEADS * NSA_HD
    pw = cmp_pos_w.reshape(2, CMP_LEN, 2 * NSA_HD)
    return pl.pallas_call(
        functools.partial(_drop_alias_ref, functools.partial(_nsa_prompt_body, seq=seq), 7),
        out_shape=jax.ShapeDtypeStruct(y_init.shape, F32),
        grid=(batch, nq),
        in_specs=[
            pl.BlockSpec((KEY_TILE, qw), lambda b, i: (b * nq + i, col_q // qw)),
            pl.BlockSpec((None, seq * KV_ROW, NSA_HD), lambda b, i: (layer, b, 0)),
            pl.BlockSpec((None, seq * KV_ROW, NSA_HD), lambda b, i: (layer, b, 0)),
            pl.BlockSpec((None, seq * KV_ROW, NSA_HD), lambda b, i: (layer, b, 0)),
            pl.BlockSpec((KEY_TILE, LANES), lambda b, i: (b * nq + i, 0)),
            pl.BlockSpec((2, CMP_LEN, 2 * NSA_HD), lambda b, i: (0, 0, 0)),
            pl.BlockSpec((2, NSA_KV_HEADS, NSA_HD, NSA_HD), lambda b, i: (0, 0, 0, 0)),
            pl.BlockSpec(memory_space=pl.ANY),
        ],
        out_specs=pl.BlockSpec((KEY_TILE, qw), lambda b, i: (b * nq + i, 0)),
        scratch_shapes=[
            pltpu.VMEM((NSA_KV_HEADS, seq, 2 * NSA_HD), BF16),
            pltpu.VMEM((NSA_KV_HEADS, NSA_HD, seq), BF16),
            pltpu.VMEM((NSA_KV_HEADS, seq, 2 * NSA_HD), BF16),
            pltpu.VMEM((NSA_KV_HEADS, NSA_HD, seq), BF16),
            pltpu.VMEM((NSA_KV_HEADS, ncp, 2 * NSA_HD), BF16),
            pltpu.VMEM((NSA_KV_HEADS, NSA_HD, ncp), BF16),
            pltpu.VMEM((n_ch + 8, NSA_HD), F32),
        ],
        input_output_aliases={7: 0},
        compiler_params=_params("parallel", "arbitrary"),
        name="nsa_prompt",
    )(zb, kv_cmp, kv_slc, kv_win, zs, pw, cmp_proj, y_init)


def _nsa_sample_body(pt_ref, q_ref, ksn_ref, kwn_ref, gate_ref, wprev_ref, pw_ref, proj_ref, *rest,
                     ts, past, n_pages):
    del pt_ref
    cmp_pages = rest[:n_pages]
    slc_pages = rest[n_pages:2 * n_pages]
    o_ref = rest[2 * n_pages + 1]
    ks_aug, vs, kw_aug, vw, kcmp_aug, vcmp, acc_scr = rest[2 * n_pages + 2:]
    G, HD = NSA_GROUP, NSA_HD
    page = slc_pages[0].shape[0] // 4
    kp = ks_aug.shape[1]
    wprev = wprev_ref.shape[0] // 4
    wp = kw_aug.shape[1]
    win_pos0 = past - wprev
    n_ch = (past + ts) // CMP_STRIDE
    n_cmp = n_ch - 1
    n_sel = -(-(past + ts) // SEL_BLOCK)
    ncp = kcmp_aug.shape[1]
    ch_per_page = page // CMP_STRIDE

    @pl.when(pl.program_id(0) == 0)
    def _constants():
        pos = lax.broadcasted_iota(jnp.int32, (kp, LANES), 0)
        f_sel = _key_features(pos, True).astype(BF16)
        posw = win_pos0 + lax.broadcasted_iota(jnp.int32, (wp, LANES), 0)
        f_win = _key_features(posw, False).astype(BF16)
        for g in range(NSA_KV_HEADS):
            ks_aug[g, :, HD:2 * HD] = f_sel
            kw_aug[g, :, HD:2 * HD] = f_win
        kcmp_aug[...] = jnp.zeros(kcmp_aug.shape, BF16)
        vcmp[...] = jnp.zeros(vcmp.shape, BF16)

    def with_tail(new_rows):
        return jnp.concatenate([new_rows, jnp.zeros((KEY_TILE - ts, HD), F32)], axis=0).astype(BF16)

    for g in range(NSA_KV_HEADS):
        for p in range(n_pages):
            rows = slice(p * page, (p + 1) * page)
            ks_aug[g, rows, 0:HD] = slc_pages[p][pl.ds(g, page, stride=4), :].astype(BF16)
            vs[g, rows, :] = slc_pages[p][pl.ds(2 + g, page, stride=4), :].astype(BF16)
        ks_aug[g, past:past + KEY_TILE, 0:HD] = with_tail(ksn_ref[pl.ds(g, ts, stride=KV_ROW), :])
        vs[g, past:past + KEY_TILE, :] = with_tail(ksn_ref[pl.ds(2 + g, ts, stride=KV_ROW), :])
        kw_aug[g, 0:wprev, 0:HD] = wprev_ref[pl.ds(g, wprev, stride=4), :].astype(BF16)
        vw[g, 0:wprev, :] = wprev_ref[pl.ds(2 + g, wprev, stride=4), :].astype(BF16)
        kw_aug[g, wprev:wprev + KEY_TILE, 0:HD] = with_tail(kwn_ref[pl.ds(g, ts, stride=KV_ROW), :])
        vw[g, wprev:wprev + KEY_TILE, :] = with_tail(kwn_ref[pl.ds(2 + g, ts, stride=KV_ROW), :])

    for kv in range(2):
        for g in range(NSA_KV_HEADS):
            c = kv * NSA_KV_HEADS + g
            cols = slice(g * HD, (g + 1) * HD)
            w0 = jnp.concatenate([pw_ref[kv, 0:CMP_STRIDE, cols]] * ch_per_page, axis=0)
            w1 = jnp.concatenate([pw_ref[kv, CMP_STRIDE:CMP_LEN, cols]] * ch_per_page, axis=0)
            for p in range(n_pages):
                x = cmp_pages[p][pl.ds(c, page, stride=4), :]
                if p + 1 < n_pages:
                    nxt = cmp_pages[p + 1][pl.ds(c, CMP_STRIDE, stride=4), :] * w1[0:CMP_STRIDE]
                else:
                    nxt = jnp.zeros((CMP_STRIDE, HD), F32)
                z = x * w0 + jnp.concatenate([(x * w1)[CMP_STRIDE:], nxt], axis=0)
                acc_scr[c, p * ch_per_page:(p + 1) * ch_per_page, :] = jnp.sum(
                    z.reshape(ch_per_page, CMP_STRIDE, HD), axis=1)
            _finish_compress(acc_scr[c, 0:n_ch, :], proj_ref, kcmp_aug, vcmp, kv, g, n_ch, ncp, False)

    R = G * ts
    gates = jax.nn.sigmoid(gate_ref[...])
    scale = HD ** -0.5

    def t_of(shape):
        return past + jnp.bitwise_and(lax.broadcasted_iota(jnp.int32, shape, 0), ts - 1)

    def softmax_pv(s, v):
        m = jnp.max(s, axis=1, keepdims=True)
        e = jnp.exp(s - m)
        return jnp.dot(e.astype(BF16), v, preferred_element_type=F32) / jnp.sum(e, axis=1, keepdims=True)

    kv_groups = range(NSA_KV_HEADS)
    q_heads = [[(q_ref[:, (g * G + r) * HD:(g * G + r + 1) * HD] * scale).astype(BF16) for r in range(G)]
               for g in kv_groups]
    feats = [[_query_features((ts, LANES), g * G + r) for r in range(G)] for g in kv_groups]
    q_plain = [_stack_heads(q_heads[g], feats[g]) for g in kv_groups]

    idx = lax.broadcasted_iota(jnp.int32, (R, wp), 1)
    dist = t_of((R, wp)) - (win_pos0 + idx)
    win_ok = (idx < wprev + ts) & (dist >= 0) & (dist <= WINDOW)
    s_win = [jnp.where(win_ok, lax.dot_general(q_plain[g], kw_aug[g], NT, preferred_element_type=F32), NEG_MASK)
             for g in kv_groups]
    n = lax.broadcasted_iota(jnp.int32, (R, ncp), 1)
    cmp_ok = (n * CMP_STRIDE + (CMP_LEN - 1) <= t_of((R, ncp))) & (n < n_cmp)
    s_cmp = [lax.dot_general(q_plain[g], kcmp_aug[g], NT, preferred_element_type=F32) for g in kv_groups]
    p_c = [_masked_softmax(s, cmp_ok) for s in s_cmp]
    o_cmp = [jnp.dot(p_c[g].astype(BF16), vcmp[g], preferred_element_type=F32) for g in kv_groups]
    o_w = [softmax_pv(s_win[g], vw[g]) for g in kv_groups]
    psum = [jnp.concatenate([p[0:ts] + p[ts:2 * ts] + p[2 * ts:3 * ts] + p[3 * ts:4 * ts],
                             jnp.zeros((LANES - ts, ncp), F32)], axis=0) for p in p_c]
    bias = [_select_blocks(psum[g], past, n_cmp, n_sel, False)[0:ts] for g in kv_groups]
    q_sel = [_stack_heads(q_heads[g], [f + bias[g] for f in feats[g]]) for g in kv_groups]
    sel_ok = lax.broadcasted_iota(jnp.int32, (R, kp), 1) <= t_of((R, kp))
    s_sel = [jnp.where(sel_ok, lax.dot_general(q_sel[g], ks_aug[g], NT, preferred_element_type=F32), NEG_MASK)
             for g in kv_groups]
    o_s = [softmax_pv(s_sel[g], vs[g]) for g in kv_groups]
    for g in kv_groups:
        _write_gated(o_ref, gates, g, ts, o_cmp[g], o_s[g], o_w[g])


def nsa_sample(zb, zs, kv_slc, kv_win, cache_cmp, cache_slc, state_win, page_table, cmp_pos_w, cmp_proj, y_init, *,
               layer, row0, batch, ts, col_q):
    depth, n_pool, page = cache_cmp.shape[:3]
    n_pages = page_table.shape[1]
    past = n_pages * page
    wprev = state_win.shape[2]
    assert ts & (ts - 1) == 0 and ts <= KEY_TILE and row0 % ts == 0
    assert past % KEY_TILE == 0 and (past + ts) // CMP_STRIDE == past // CMP_STRIDE
    assert page % CMP_STRIDE == 0 and -(-(past + ts) // SEL_BLOCK) <= SEL_COLS and wprev % 16 == 0
    qw = NSA_HEADS * NSA_HD
    n_ch = past // CMP_STRIDE
    ncp = -(-n_ch // LANES) * LANES
    r0 = row0 // ts
    pw = cmp_pos_w.reshape(2, CMP_LEN, 2 * NSA_HD)
    cmp_view = cache_cmp.reshape(depth, n_pool, page * 4, NSA_HD)
    slc_view = cache_slc.reshape(depth, n_pool, page * 4, NSA_HD)
    win_view = state_win.reshape(depth, batch, wprev * 4, NSA_HD)

    def page_map(p):
        return lambda b, pt: (layer, pt[b * n_pages + p], 0, 0)

    in_specs = [
        pl.BlockSpec((ts, qw), lambda b, pt: (r0 + b, col_q // qw)),
        pl.BlockSpec((None, ts * KV_ROW, NSA_HD), lambda b, pt: (layer, b, 0)),
        pl.BlockSpec((None, ts * KV_ROW, NSA_HD), lambda b, pt: (layer, b, 0)),
        pl.BlockSpec((ts, LANES), lambda b, pt: (r0 + b, 0)),
        pl.BlockSpec((None, None, wprev * 4, NSA_HD), lambda b, pt: (layer, b, 0, 0)),
        pl.BlockSpec((2, CMP_LEN, 2 * NSA_HD), lambda b, pt: (0, 0, 0)),
        pl.BlockSpec((2, NSA_KV_HEADS, NSA_HD, NSA_HD), lambda b, pt: (0, 0, 0, 0)),
    ]
    in_specs += [pl.BlockSpec((None, None, page * 4, NSA_HD), page_map(p % n_pages)) for p in range(2 * n_pages)]
    in_specs.append(pl.BlockSpec(memory_space=pl.ANY))
    return pl.pallas_call(
        functools.partial(_nsa_sample_body, ts=ts, past=past, n_pages=n_pages),
        out_shape=jax.ShapeDtypeStruct(y_init.shape, F32),
        grid_spec=pltpu.PrefetchScalarGridSpec(
            num_scalar_prefetch=1,
            grid=(batch,),
            in_specs=in_specs,
            out_specs=pl.BlockSpec((ts, qw), lambda b, pt: (r0 + b, 0)),
            scratch_shapes=[
                pltpu.VMEM((NSA_KV_HEADS, past + KEY_TILE, 2 * NSA_HD), BF16),
                pltpu.VMEM((NSA_KV_HEADS, past + KEY_TILE, NSA_HD), BF16),
                pltpu.VMEM((NSA_KV_HEADS, wprev + KEY_TILE, 2 * NSA_HD), BF16),
                pltpu.VMEM((NSA_KV_HEADS, wprev + KEY_TILE, NSA_HD), BF16),
                pltpu.VMEM((NSA_KV_HEADS, ncp, 2 * NSA_HD), BF16),
                pltpu.VMEM((NSA_KV_HEADS, ncp, NSA_HD), BF16),
                pltpu.VMEM((2 * NSA_KV_HEADS, n_ch, NSA_HD), F32),
            ],
        ),
        input_output_aliases={8 + 2 * n_pages: 0},
        compiler_params=_params("arbitrary"),
        name="nsa_sample",
    )(page_table.reshape(-1), zb, kv_slc, kv_win, zs, win_view, pw, cmp_proj,
      *([cmp_view] * n_pages), *([slc_view] * n_pages), y_init)


def _window_state_body(old_ref, new_ref, o_ref):
    keep = o_ref.shape[1] - new_ref.shape[1]
    o_ref[:, 0:keep, :] = old_ref[:, old_ref.shape[1] - keep:, :]
    o_ref[:, keep:, :] = new_ref[...]


def window_state(state_win, kv_win_new, ts):
    depth, batch, w_old = state_win.shape[:3]
    keep = min(WINDOW, w_old + ts) - ts
    assert keep >= 0 and (keep * KV_ROW) % 8 == 0 and (ts * KV_ROW) % 8 == 0
    bt = math.gcd(batch, 4)
    out = pl.pallas_call(
        _window_state_body,
        out_shape=jax.ShapeDtypeStruct((depth, batch, (keep + ts) * KV_ROW, NSA_HD), F32),
        grid=(depth, batch // bt),
        in_specs=[
            pl.BlockSpec((None, bt, w_old * KV_ROW, NSA_HD), lambda l, b: (l, b, 0, 0)),
            pl.BlockSpec((None, bt, ts * KV_ROW, NSA_HD), lambda l, b: (l, b, 0, 0)),
        ],
        out_specs=pl.BlockSpec((None, bt, (keep + ts) * KV_ROW, NSA_HD), lambda l, b: (l, b, 0, 0)),
        compiler_params=_params("parallel", "parallel"),
        name="window_state",
    )(state_win.reshape(depth, batch, w_old * KV_ROW, NSA_HD), kv_win_new.reshape(depth, batch, ts * KV_ROW, NSA_HD))
    return out.reshape(depth, batch, keep + ts, *state_win.shape[3:])


ZB_NQ, ZB_POOL, ZB_MQ, ZB_MK, ZB_MV, ZB_MO, ZB_END = (0, 1024, 1536, 2048, 2560, 3072, 3584)
W_POOL, W_MI, W_NQ, W_CMP, W_NG, W_END = 0, 2560, 2568, 3592, 5128, 5152


def _split_w_in(w_in_l):
    w = w_in_l.astype(BF16)
    big = jnp.concatenate([w[:, W_NQ:W_CMP], w[:, W_POOL:W_MI], w[:, W_CMP:W_NG]], axis=1)
    small = jnp.concatenate([w[:, W_MI:W_NQ], w[:, W_NG:W_END]], axis=1)
    return big, jnp.pad(small, ((0, 0), (0, LANES - small.shape[1])))


def kernel(x_prompt, x_sample, cache_kv_cmp, cache_kv_slc, state_kv_win, state_pool, state_mlstm_C, state_mlstm_n, state_mlstm_m, page_table, ffn1_norm, ffn1_w_gate, ffn1_w_up, ffn1_w_down, mix_norm, w_in, w_out, pool_w, pool_scale, mlstm_if_bias, mlstm_norm, nsa_cmp_pos_w, nsa_cmp_proj, ffn2_norm, ffn2_w_gate, ffn2_w_up, ffn2_w_down, final_norm):
    bp, tp, d = x_prompt.shape
    bs, ts, _ = x_sample.shape
    depth = w_in.shape[0]
    mp, ms = bp * tp, bs * ts
    m_all = mp + ms
    past_len = page_table.shape[1] * cache_kv_cmp.shape[2]
    pd = pool_scale.shape[1]
    kv_row = (2, NSA_KV_HEADS, NSA_HD)
    zeros = lambda *s: jnp.zeros(s, F32)
    ffn1 = [w.astype(BF16) for w in (ffn1_w_gate, ffn1_w_up, ffn1_w_down)]
    ffn2 = [w.astype(BF16) for w in (ffn2_w_gate, ffn2_w_up, ffn2_w_down)]
    w_out_b = w_out.astype(BF16)
    outs = [[] for _ in range(6)]
    kv_bufs = ([zeros(depth, mp * KV_ROW, LANES) for _ in range(KV_SLABS)]
               + [zeros(depth, ms * KV_ROW, LANES) for _ in range(KV_SLABS)])
    p_c = zeros(depth, bp, MLSTM_HEADS, MLSTM_HD, MLSTM_HD)
    s_c = zeros(depth, bs, MLSTM_HEADS, MLSTM_HD, MLSTM_HD)
    for l in range(depth):
        if l == 0:
            x = ffn_half_step(x_prompt.reshape(mp, d), ffn1_norm[l], *ffn1, l, y_init=zeros(m_all, d))
            x = ffn_half_step(x_sample.reshape(ms, d), ffn1_norm[l], *ffn1, l, out_row0=mp, y_init=x)
        else:
            x = ffn_half_step(x, ffn1_norm[l], *ffn1, l)
        w_big, w_small = _split_w_in(w_in[l])
        zb, zs, kv_bufs = mix_project(x, mix_norm[l], w_big, w_small, l, mp, kv_bufs)
        cmp_p, slc_p, win_p, cmp_s, slc_s, win_s = kv_bufs
        z_pool = zb[:, ZB_POOL:ZB_MQ]

        y_pool = pool_prompt(zb, pool_w[l], pool_scale[l], zeros(m_all, pd), batch=bp, seq=tp, col=ZB_POOL)
        y_m, p_c, p_n, p_m = mlstm_mix(
            zb, zs, mlstm_if_bias[l], mlstm_norm[l], zeros(depth, bp, MLSTM_HEADS, MLSTM_HD, MLSTM_HD),
            zeros(bp, MLSTM_HEADS, MLSTM_HD), zeros(bp, MLSTM_HEADS), p_c, zeros(m_all, MLSTM_HEADS * MLSTM_HD),
            row0=0, batch=bp, seq=tp, col_q=ZB_MQ, layer=l)
        y_nsa = nsa_prompt(zb, zs, cmp_p, slc_p, win_p, nsa_cmp_pos_w[l], nsa_cmp_proj[l],
                           zeros(m_all, NSA_HEADS * NSA_HD), layer=l, batch=bp, seq=tp, col_q=ZB_NQ)

        pool_full = jnp.concatenate([zeros(bs, POOL_HALO - POOL_BUF, pd), state_pool[l],
                                     z_pool[mp:].reshape(bs, ts, pd)], axis=1)
        y_pool = pool_sample(pool_full, pool_w[l], pool_scale[l], y_pool, pos0=past_len, row0=mp)
        y_m, s_c, s_n, s_m = mlstm_mix(
            zb, zs, mlstm_if_bias[l], mlstm_norm[l], state_mlstm_C, state_mlstm_n[l], state_mlstm_m[l], s_c, y_m,
            row0=mp, batch=bs, seq=ts, col_q=ZB_MQ, layer=l)
        y_nsa = nsa_sample(zb, zs, slc_s, win_s, cache_kv_cmp, cache_kv_slc, state_kv_win, page_table,
                           nsa_cmp_pos_w[l], nsa_cmp_proj[l], y_nsa, layer=l, row0=mp, batch=bs, ts=ts,
                           col_q=ZB_NQ)

        x = out_project(x, y_pool, y_m, y_nsa, w_out_b, l)
        if l < depth - 1:
            x = ffn_half_step(x, ffn2_norm[l], *ffn2, l)
        else:
            y_prompt = ffn_half_step(x, ffn2_norm[l], *ffn2, l, gf=final_norm, rows=mp).reshape(bp, tp, d)
            y_sample = ffn_half_step(x, ffn2_norm[l], *ffn2, l, gf=final_norm, in_row0=mp, rows=ms).reshape(bs, ts, d)

        pool_p = jnp.concatenate([zeros(bp, POOL_BUF, pd), z_pool[:mp].reshape(bp, tp, pd)], axis=1)
        layer_out = (pool_p[:, -POOL_BUF:], p_n, p_m, pool_full[:, -POOL_BUF:], s_n, s_m)
        for acc, a in zip(outs, layer_out):
            acc.append(a)
    p_pool, p_n, p_m, s_pool, s_n, s_m = [jnp.stack(a) for a in outs]
    p_kv = [b.reshape(depth, bp, tp, *kv_row) for b in (cmp_p, slc_p, win_p)]
    s_kv = [b.reshape(depth, bs, ts, *kv_row) for b in (cmp_s, slc_s, win_s)]
    s_kv_win = window_state(state_kv_win, win_s, ts)
    return (y_prompt, y_sample, p_kv[0], p_kv[1], p_kv[2][:, :, tp - min(WINDOW, tp):], p_pool, p_c, p_n, p_m,
            s_kv[0], s_kv[1], s_kv_win, s_pool, s_c, s_n, s_m)
```

```python
import functools
import math

import jax
import jax.numpy as jnp
from jax import lax
from jax.experimental import pallas as pl
from jax.experimental.pallas import tpu as pltpu

F32 = jnp.float32
BF16 = jnp.bfloat16
EPS = 1e-6

VMEM_LIMIT_BYTES = 58 * 1024 * 1024
LANES = 128

POOL_WINDOWS = (2, 4, 8, 16)
POOL_BUF = 15
MLSTM_HEADS = 4
MLSTM_HD = 128
MLSTM_CHUNK = 64
NSA_HD = 128
NSA_HEADS = 8
NSA_KV_HEADS = 2
NSA_GROUP = 4
CMP_LEN = 32
CMP_STRIDE = 16
SEL_BLOCK = 64
SEL_TOPN = 16
WINDOW = 512
FORCE_BONUS = 1.0e4

NEG_MASK = -1.0e30
NT = (((1,), (1,)), ((), ()))
TN = (((0,), (0,)), ((), ()))


def _pick_tile(n, pref):
    t = pref
    while t > 8 and n % t:
        t //= 2
    assert n % t == 0, (n, pref)
    return t


def _params(*sem):
    return pltpu.CompilerParams(dimension_semantics=sem, vmem_limit_bytes=VMEM_LIMIT_BYTES)


def _drop_alias_ref(body, index, *refs):
    return body(*refs[:index], *refs[index + 1:])


def _rms_rows(x, g):
    ms = jnp.mean(x * x, axis=-1, keepdims=True)
    return x * lax.rsqrt(ms + EPS) * g


def _ffn_body(x_ref, g_ref, wg_ref, wu_ref, wd_ref, gf_ref, o_ref, n_scr, *, final_norm):
    f = pl.program_id(1)

    @pl.when(f == 0)
    def _():
        x = x_ref[...]
        n_scr[...] = _rms_rows(x, g_ref[...]).astype(BF16)
        o_ref[...] = x

    n = n_scr[...]
    hg = jnp.dot(n, wg_ref[...], preferred_element_type=F32)
    hu = jnp.dot(n, wu_ref[...], preferred_element_type=F32)
    h = (hg * jax.nn.sigmoid(hg) * hu).astype(BF16)
    o_ref[...] += 0.5 * jnp.dot(h, wd_ref[...], preferred_element_type=F32)

    if final_norm:
        @pl.when(f == pl.num_programs(1) - 1)
        def _():
            o_ref[...] = _rms_rows(o_ref[...], gf_ref[...])


def ffn_half_step(x, g, wg, wu, wd, layer, gf=None, *, in_row0=0, rows=None, out_row0=0, y_init=None):
    d = x.shape[1]
    rows = x.shape[0] if rows is None else rows
    fdim = wg.shape[2]
    tm = _pick_tile(math.gcd(math.gcd(rows, in_row0), out_row0), 512)
    tf = _pick_tile(fdim, 512)
    final_norm = gf is not None
    if gf is None:
        gf = g
    i0, o0 = in_row0 // tm, out_row0 // tm
    body = functools.partial(_ffn_body, final_norm=final_norm)
    extra_specs, extra_args, aliases = [], [], {}
    if y_init is not None:
        body = functools.partial(_drop_alias_ref, body, 6)
        extra_specs, extra_args, aliases = [pl.BlockSpec(memory_space=pl.ANY)], [y_init], {6: 0}
    return pl.pallas_call(
        body,
        out_shape=jax.ShapeDtypeStruct((rows, d) if y_init is None else y_init.shape, F32),
        grid=(rows // tm, fdim // tf),
        in_specs=[
            pl.BlockSpec((tm, d), lambda i, f: (i0 + i, 0)),
            pl.BlockSpec((1, d), lambda i, f: (0, 0)),
            pl.BlockSpec((None, d, tf), lambda i, f: (layer, 0, f)),
            pl.BlockSpec((None, d, tf), lambda i, f: (layer, 0, f)),
            pl.BlockSpec((None, tf, d), lambda i, f: (layer, f, 0)),
            pl.BlockSpec((1, d), lambda i, f: (0, 0)),
        ] + extra_specs,
        out_specs=pl.BlockSpec((tm, d), lambda i, f: (o0 + i, 0)),
        scratch_shapes=[pltpu.VMEM((tm, d), BF16)],
        input_output_aliases=aliases,
        compiler_params=_params("parallel", "arbitrary"),
        name="ffn_half_step",
    )(x, g.reshape(1, d), wg, wu, wd, gf.reshape(1, d), *extra_args)


KV_SLABS = 3
KV_ROW = 4


def _inproj_body(x_ref, g_ref, w_ref, ws_ref, *rest, main_cols, n_prompt_tiles, n_aliased):
    zb_ref, zs_ref, *kv_refs, n_scr = rest[n_aliased:]
    i, j = pl.program_id(0), pl.program_id(1)
    tm, tn = x_ref.shape[0], w_ref.shape[1]
    slab_w = KV_ROW * LANES

    @pl.when(j == 0)
    def _():
        n_scr[...] = _rms_rows(x_ref[...], g_ref[...]).astype(BF16)
        zs_ref[...] = jnp.dot(n_scr[...], ws_ref[...], preferred_element_type=F32)

    z = jnp.dot(n_scr[...], w_ref[...], preferred_element_type=F32)

    @pl.when(j * tn < main_cols)
    def _():
        zb_ref[...] = z

    def scatter(ref, off):
        for c in range(KV_ROW):
            ref[pl.ds(c, tm, stride=KV_ROW), :] = z[:, off + c * LANES:off + (c + 1) * LANES]

    for k in range(KV_SLABS):
        jt, off = divmod(main_cols + k * slab_w, tn)
        pl.when((j == jt) & (i < n_prompt_tiles))(functools.partial(scatter, kv_refs[k], off))
        pl.when((j == jt) & (i >= n_prompt_tiles))(functools.partial(scatter, kv_refs[KV_SLABS + k], off))


def mix_project(x, g, w, w_small, layer, mp, kv_bufs):
    m, d = x.shape
    n = w.shape[1]
    tn = _pick_tile(n, 1024)
    ms = m - mp
    tm = _pick_tile(math.gcd(mp, ms), 1024)
    main_cols = n - KV_SLABS * KV_ROW * LANES
    n_main = -(-main_cols // tn)
    npt = mp // tm
    out_shape = [jax.ShapeDtypeStruct((m, n_main * tn), F32), jax.ShapeDtypeStruct((m, LANES), F32)]
    out_shape += [jax.ShapeDtypeStruct(b.shape, F32) for b in kv_bufs]
    once = pl.Buffered(1)
    out_specs = [pl.BlockSpec((tm, tn), lambda i, j: (i, jnp.minimum(j, n_main - 1))),
                 pl.BlockSpec((tm, LANES), lambda i, j: (i, 0), pipeline_mode=once)]
    out_specs += [pl.BlockSpec((None, tm * KV_ROW, LANES), lambda i, j: (layer, jnp.minimum(i, npt - 1), 0),
                               pipeline_mode=once)] * KV_SLABS
    out_specs += [pl.BlockSpec((None, tm * KV_ROW, LANES), lambda i, j: (layer, jnp.maximum(i - npt, 0), 0),
                               pipeline_mode=once)] * KV_SLABS
    in_specs = [
        pl.BlockSpec((tm, d), lambda i, j: (i, 0)),
        pl.BlockSpec((1, d), lambda i, j: (0, 0)),
        pl.BlockSpec((d, tn), lambda i, j: (0, j)),
        pl.BlockSpec((d, LANES), lambda i, j: (0, 0)),
    ]
    args = [x, g.reshape(1, d), w, w_small]
    in_specs += [pl.BlockSpec(memory_space=pl.ANY)] * len(kv_bufs)
    aliases = {len(args) + k: 2 + k for k in range(len(kv_bufs))}
    args += list(kv_bufs)
    zb, zs, *bufs = pl.pallas_call(
        functools.partial(_inproj_body, main_cols=main_cols, n_prompt_tiles=npt, n_aliased=len(aliases)),
        out_shape=out_shape,
        grid=(m // tm, n // tn),
        in_specs=in_specs,
        out_specs=out_specs,
        scratch_shapes=[pltpu.VMEM((tm, d), BF16)],
        input_output_aliases=aliases,
        compiler_params=_params("arbitrary", "arbitrary"),
        name="mix_project",
    )(*args)
    return zb, zs, bufs


def _outproj_body(x_ref, ya_ref, yb_ref, yc_ref, wa_ref, wb_ref, wc_ref, o_ref):
    acc = x_ref[...]
    acc += jnp.dot(ya_ref[...].astype(BF16), wa_ref[...], preferred_element_type=F32)
    acc += jnp.dot(yb_ref[...].astype(BF16), wb_ref[...], preferred_element_type=F32)
    acc += jnp.dot(yc_ref[...].astype(BF16), wc_ref[...], preferred_element_type=F32)
    o_ref[...] = acc


def out_project(x, y_pool, y_mlstm, y_nsa, w_out, layer):
    m, d = x.shape
    da, db, dc = y_pool.shape[1], y_mlstm.shape[1], y_nsa.shape[1]
    assert da == db and dc % da == 0
    tm = _pick_tile(m, 512)
    tn = d
    return pl.pallas_call(
        _outproj_body,
        out_shape=jax.ShapeDtypeStruct((m, d), F32),
        grid=(m // tm, d // tn),
        in_specs=[
            pl.BlockSpec((tm, tn), lambda i, j: (i, j)),
            pl.BlockSpec((tm, da), lambda i, j: (i, 0)),
            pl.BlockSpec((tm, db), lambda i, j: (i, 0)),
            pl.BlockSpec((tm, dc), lambda i, j: (i, 0)),
            pl.BlockSpec((None, da, tn), lambda i, j: (layer, 0, j)),
            pl.BlockSpec((None, db, tn), lambda i, j: (layer, 1, j)),
            pl.BlockSpec((None, dc, tn), lambda i, j: (layer, (da + db) // dc, j)),
        ],
        out_specs=pl.BlockSpec((tm, tn), lambda i, j: (i, j)),
        compiler_params=_params("parallel", "arbitrary"),
        name="out_project",
    )(x, y_pool, y_mlstm, y_nsa, w_out, w_out, w_out)


POOL_HALO = 16


def _pool_group(load, g, n_avail, w_ref, sc_ref):
    w = POOL_WINDOWS[g]
    z = load(0)
    acc = z
    for j in range(1, w):
        acc = acc + load(j)
    d = acc / jnp.minimum(n_avail, w).astype(F32) - z
    lead = d.shape[:-1]
    gd = d.shape[-1]
    y = jnp.dot(d.reshape(-1, gd).astype(BF16), w_ref[g].astype(BF16), preferred_element_type=F32)
    return (y * sc_ref[:, g * gd:(g + 1) * gd]).reshape(*lead, gd)


def _pool_prompt_body(z_ref, w_ref, sc_ref, o_ref, full_scr, *, chunk):
    seq, pd = z_ref.shape
    gd = pd // len(POOL_WINDOWS)
    full_scr[0:POOL_HALO, :] = jnp.zeros((POOL_HALO, pd), F32)
    full_scr[POOL_HALO:POOL_HALO + seq, :] = z_ref[...]
    for c in range(seq // chunk):
        n_avail = c * chunk + 1 + lax.broadcasted_iota(jnp.int32, (chunk, gd), 0)
        for g in range(len(POOL_WINDOWS)):
            load = lambda j: full_scr[pl.ds(POOL_HALO + c * chunk - j, chunk), g * gd:(g + 1) * gd]
            o_ref[c * chunk:(c + 1) * chunk, g * gd:(g + 1) * gd] = _pool_group(load, g, n_avail, w_ref, sc_ref)


def pool_prompt(zb, pool_w, pool_scale, y_init, *, batch, seq, col):
    pd = pool_scale.shape[0]
    chunk = _pick_tile(seq, 256)
    return pl.pallas_call(
        functools.partial(_drop_alias_ref, functools.partial(_pool_prompt_body, chunk=chunk), 3),
        out_shape=jax.ShapeDtypeStruct(y_init.shape, F32),
        grid=(batch,),
        in_specs=[
            pl.BlockSpec((seq, pd), lambda b: (b, col // pd)),
            pl.BlockSpec(pool_w.shape, lambda b: (0, 0, 0)),
            pl.BlockSpec((1, pd), lambda b: (0, 0)),
            pl.BlockSpec(memory_space=pl.ANY),
        ],
        out_specs=pl.BlockSpec((seq, pd), lambda b: (b, 0)),
        scratch_shapes=[pltpu.VMEM((POOL_HALO + seq, pd), F32)],
        input_output_aliases={3: 0},
        compiler_params=_params("parallel"),
        name="pool_prompt",
    )(zb, pool_w, pool_scale.reshape(1, pd), y_init)


def _pool_sample_body(full_ref, w_ref, sc_ref, o_ref, *, pos0):
    bt, rows, pd = full_ref.shape
    ts = rows - POOL_HALO
    gd = pd // len(POOL_WINDOWS)
    n_avail = pos0 + 1 + lax.broadcasted_iota(jnp.int32, (bt, ts, gd), 1)
    for g in range(len(POOL_WINDOWS)):
        load = lambda j: full_ref[:, pl.ds(POOL_HALO - j, ts), g * gd:(g + 1) * gd]
        o_ref[:, g * gd:(g + 1) * gd] = _pool_group(load, g, n_avail, w_ref, sc_ref).reshape(bt * ts, gd)


def pool_sample(full, pool_w, pool_scale, y_init, *, pos0, row0):
    batch, rows, pd = full.shape
    ts = rows - POOL_HALO
    bt = math.gcd(batch, 32)
    assert ts % 8 == 0 and row0 % (bt * ts) == 0
    r0 = row0 // (bt * ts)
    return pl.pallas_call(
        functools.partial(_drop_alias_ref, functools.partial(_pool_sample_body, pos0=pos0), 3),
        out_shape=jax.ShapeDtypeStruct(y_init.shape, F32),
        grid=(batch // bt,),
        in_specs=[
            pl.BlockSpec((bt, rows, pd), lambda b: (b, 0, 0)),
            pl.BlockSpec(pool_w.shape, lambda b: (0, 0, 0)),
            pl.BlockSpec((1, pd), lambda b: (0, 0)),
            pl.BlockSpec(memory_space=pl.ANY),
        ],
        out_specs=pl.BlockSpec((bt * ts, pd), lambda b: (r0 + b, 0)),
        input_output_aliases={3: 0},
        compiler_params=_params("parallel"),
        name="pool_sample",
    )(full, pool_w, pool_scale.reshape(1, pd), y_init)


def _log_sigmoid(x):
    return jnp.minimum(x, 0.0) - jnp.log1p(jnp.exp(-jnp.abs(x)))


def _mlstm_body(q_ref, k_ref, v_ref, og_ref, g_ref, bias_ref, gn_ref, c0_ref, n0_ref, m0_ref,
                y_ref, c_ref, n_ref, m_ref, *, L):
    nseq = q_ref.shape[0] // L
    H, D = MLSTM_HEADS, MLSTM_HD
    hi = lax.Precision.HIGHEST

    @pl.when(pl.program_id(1) == 0)
    def _():
        c_ref[...] = c0_ref[...]
        n_ref[...] = n0_ref[...]
        m_ref[...] = m0_ref[...]

    sel = (lax.broadcasted_iota(jnp.int32, (8, LANES), 0) == lax.broadcasted_iota(jnp.int32, (8, LANES), 1)).astype(F32)
    li = lax.broadcasted_iota(jnp.int32, (L, L), 0)
    si = lax.broadcasted_iota(jnp.int32, (L, L), 1)
    causal = li >= si
    lane = lax.broadcasted_iota(jnp.int32, (1, LANES), 1)
    chains = [(s, h) for s in range(nseq) for h in range(H)]

    gates = []
    for s in range(nseq):
        rows = slice(s * L, (s + 1) * L)
        gz = g_ref[rows, :] + bias_ref[...]
        gz_rows = lax.dot_general(sel, gz, NT, precision=hi, preferred_element_type=F32)
        b_cols = jnp.dot(causal.astype(F32), _log_sigmoid(gz), precision=hi, preferred_element_type=F32)
        b_rows = jnp.dot(_log_sigmoid(gz_rows), (li <= si).astype(F32), precision=hi, preferred_element_type=F32)
        gates.append((gz, gz_rows, b_cols, b_rows, m_ref[s]))

    qk, qc, state = {}, {}, {}
    for s, h in chains:
        rows, cols = slice(s * L, (s + 1) * L), slice(h * D, (h + 1) * D)
        qh = q_ref[rows, cols]
        kh = k_ref[rows, cols] * (D ** -0.5)
        ch = c_ref[s, h]
        nh = n_ref[s, h:h + 1, :]
        qb, kb = qh.astype(BF16), kh.astype(BF16)
        qk[s, h] = lax.dot_general(qb, kb, NT, preferred_element_type=F32)
        qc[s, h] = lax.dot_general(qb, ch.astype(BF16), NT, preferred_element_type=F32)
        state[s, h] = (qh, kh, kb, ch, nh)

    sm, stats = {}, {}
    for s, h in chains:
        gz, gz_rows, b_cols, b_rows, m_all = gates[s]
        bc = b_cols[:, H + h:H + h + 1]
        ic = gz[:, h:h + 1]
        br = b_rows[H + h:H + h + 1, :]
        ir = gz_rows[h:h + 1, :]
        m_prev = m_all[:, h:h + 1]
        dmat = jnp.where(causal, bc - br + ir, NEG_MASK)
        inter = bc + m_prev
        m_t = jnp.maximum(inter, jnp.max(dmat, axis=1, keepdims=True))
        sm[s, h] = qk[s, h] * jnp.exp(dmat - m_t)
        m_new = m_t[L - 1:L, :]
        b_last = bc[L - 1:L, :]
        stats[s, h] = (jnp.exp(inter - m_t), m_t, m_new, jnp.exp(b_last + m_prev - m_new),
                       jnp.exp(b_last - bc + ic - m_new))

    num, c_new = {}, {}
    for s, h in chains:
        rows, cols = slice(s * L, (s + 1) * L), slice(h * D, (h + 1) * D)
        a_inter, _, _, decay, w_col = stats[s, h]
        qh, kh, kb, ch, nh = state[s, h]
        vh = v_ref[rows, cols]
        num[s, h] = jnp.dot(sm[s, h].astype(BF16), vh.astype(BF16), preferred_element_type=F32) + a_inter * qc[s, h]
        c_new[s, h] = decay * ch + lax.dot_general((vh * w_col).astype(BF16), kb, TN, preferred_element_type=F32)

    m_out = [gates[s][4] for s in range(nseq)]
    for s, h in chains:
        rows, cols = slice(s * L, (s + 1) * L), slice(h * D, (h + 1) * D)
        a_inter, m_t, m_new, decay, w_col = stats[s, h]
        qh, kh, kb, ch, nh = state[s, h]
        den = jnp.sum(sm[s, h], axis=1, keepdims=True) + a_inter * jnp.sum(qh * nh, axis=1, keepdims=True)
        den = jnp.maximum(jnp.abs(den), jnp.exp(-m_t))
        hh = num[s, h] / den
        mu = jnp.mean(hh, axis=1, keepdims=True)
        var = jnp.mean(jnp.square(hh - mu), axis=1, keepdims=True)
        hn = (hh - mu) * lax.rsqrt(var + EPS) * gn_ref[:, cols]
        y_ref[rows, cols] = jax.nn.sigmoid(og_ref[rows, cols]) * hn
        c_ref[s, h] = c_new[s, h]
        n_ref[s, h:h + 1, :] = decay * nh + jnp.sum(kh * w_col, axis=0, keepdims=True)
        m_out[s] = jnp.where(lane == h, m_new, m_out[s])
    for s in range(nseq):
        m_ref[s] = m_out[s]


def mlstm_mix(zb, zs, if_bias, mnorm, c0, n0, m0, c_all, y_init, *, row0, batch, seq, col_q, layer):
    H, D = MLSTM_HEADS, MLSTM_HD
    dim = H * D
    L = math.gcd(seq, MLSTM_CHUNK)
    nc = seq // L
    nseq = math.gcd(batch, min(4, MLSTM_CHUNK // L)) if nc == 1 else 1
    rows = nseq * L
    assert L % 8 == 0 and row0 % rows == 0 and col_q % dim == 0
    r0 = row0 // rows
    cq = col_q // dim
    bias = jnp.pad(if_bias, (0, LANES - if_bias.shape[0])).reshape(1, LANES)
    m0p = jnp.pad(m0, ((0, 0), (0, LANES - H))).reshape(batch, 1, LANES)
    row = lambda b, c: r0 + b * nc + c
    in_specs = [
        pl.BlockSpec((rows, dim), lambda b, c: (row(b, c), cq)),
        pl.BlockSpec((rows, dim), lambda b, c: (row(b, c), cq + 1)),
        pl.BlockSpec((rows, dim), lambda b, c: (row(b, c), cq + 2)),
        pl.BlockSpec((rows, dim), lambda b, c: (row(b, c), cq + 3)),
        pl.BlockSpec((rows, LANES), lambda b, c: (row(b, c), 0)),
        pl.BlockSpec((1, LANES), lambda b, c: (0, 0)),
        pl.BlockSpec((1, dim), lambda b, c: (0, 0)),
        pl.BlockSpec((None, nseq, H, D, D), lambda b, c: (layer, b, 0, 0, 0)),
        pl.BlockSpec((nseq, H, D), lambda b, c: (b, 0, 0)),
        pl.BlockSpec((nseq, 1, LANES), lambda b, c: (b, 0, 0)),
    ]
    args = [zb, zb, zb, zb, zs, bias, mnorm.reshape(1, dim), c0, n0, m0p]
    aliases = {}
    body = functools.partial(_mlstm_body, L=L)
    for out_index, init in ((1, c_all), (0, y_init)):
        in_specs.append(pl.BlockSpec(memory_space=pl.ANY))
        args.append(init)
        aliases[len(args) - 1] = out_index
        body = functools.partial(_drop_alias_ref, body, len(args) - 1)
    y, c_out, n_out, m_out = pl.pallas_call(
        body,
        out_shape=(jax.ShapeDtypeStruct(y_init.shape, F32),
                   jax.ShapeDtypeStruct(c_all.shape, F32),
                   jax.ShapeDtypeStruct((batch, H, D), F32),
                   jax.ShapeDtypeStruct((batch, 1, LANES), F32)),
        grid=(batch // nseq, nc),
        in_specs=in_specs,
        out_specs=(
            pl.BlockSpec((rows, dim), lambda b, c: (r0 + b * nc + c, 0)),
            pl.BlockSpec((None, nseq, H, D, D), lambda b, c: (layer, b, 0, 0, 0)),
            pl.BlockSpec((nseq, H, D), lambda b, c: (b, 0, 0)),
            pl.BlockSpec((nseq, 1, LANES), lambda b, c: (b, 0, 0)),
        ),
        input_output_aliases=aliases,
        compiler_params=_params("parallel", "arbitrary"),
        name="mlstm_mix",
    )(*args)
    return y, c_out, n_out, m_out[:, 0, :H]


KEY_TILE = 128
KEY_CHUNK = 256
SEL_COLS = 64
POS_HI, POS_LO = SEL_COLS, SEL_COLS + 1
NEG_SEL = -1.0e9


def _slope(h):
    return 2.0 ** (-(8.0 / NSA_HEADS) * (h + 1))


def _key_features(pos, onehot):
    lane = lax.broadcasted_iota(jnp.int32, pos.shape, 1)
    hi = lax.shift_right_logical(pos, 6)
    lo = jnp.bitwise_and(pos, SEL_BLOCK - 1)
    f = jnp.where(lane == POS_HI, hi.astype(F32), jnp.where(lane == POS_LO, lo.astype(F32), 0.0))
    if onehot:
        f = jnp.where(lane == hi, 1.0, f)
    return f


def _query_features(shape, h):
    lane = lax.broadcasted_iota(jnp.int32, shape, 1)
    return jnp.where(lane == POS_HI, SEL_BLOCK * _slope(h), jnp.where(lane == POS_LO, _slope(h), 0.0))


def _compress_block_rows(load, pw_ref, kv, g):
    cols = slice(g * NSA_HD, (g + 1) * NSA_HD)
    a0 = a1 = None
    for j in range(CMP_STRIDE):
        rows = load(j)
        t0 = rows * pw_ref[kv, j:j + 1, cols]
        t1 = rows * pw_ref[kv, CMP_STRIDE + j:CMP_STRIDE + j + 1, cols]
        a0 = t0 if a0 is None else a0 + t0
        a1 = t1 if a1 is None else a1 + t1
    return a0, a1


def _finish_compress(acc, proj_ref, kcmp_aug, vcmp, kv, g, n_ch, ncp, v_transposed):
    c = jnp.dot(acc.astype(BF16), proj_ref[kv, g].astype(BF16), preferred_element_type=F32)
    if kv == 0:
        kcmp_aug[g, 0:n_ch, 0:NSA_HD] = c.astype(BF16)
        n = lax.broadcasted_iota(jnp.int32, (ncp, LANES), 0)
        kcmp_aug[g, :, NSA_HD:2 * NSA_HD] = _key_features(n * CMP_STRIDE + (CMP_LEN - 1), False).astype(BF16)
    elif v_transposed:
        if ncp > n_ch:
            c = jnp.concatenate([c, jnp.zeros((ncp - n_ch, NSA_HD), F32)], axis=0)
        for blk in range(ncp // LANES):
            vcmp[g, :, blk * LANES:(blk + 1) * LANES] = c[blk * LANES:(blk + 1) * LANES].T.astype(BF16)
    else:
        vcmp[g, 0:n_ch, :] = c.astype(BF16)


def _masked_softmax(s, mask):
    s = jnp.where(mask, s, NEG_MASK)
    m = jnp.max(s, axis=1, keepdims=True)
    e = jnp.where(mask, jnp.exp(s - m), 0.0)
    return e / jnp.maximum(jnp.sum(e, axis=1, keepdims=True), 1e-30)


def _select_blocks(psum, t0, n_cmp, n_sel, queries_on_lanes):
    ncp = psum.shape[0] if queries_on_lanes else psum.shape[1]
    nsp = -(-n_sel // 8) * 8
    j = lax.broadcasted_iota(jnp.int32, (nsp, ncp), 0)
    n = lax.broadcasted_iota(jnp.int32, (nsp, ncp), 1)
    cover = ((n * CMP_STRIDE < j * SEL_BLOCK + SEL_BLOCK) & (n * CMP_STRIDE + CMP_LEN > j * SEL_BLOCK)
             & (n < n_cmp)).astype(F32)
    if queries_on_lanes:
        imp = jnp.dot(cover, psum, precision=lax.Precision.HIGHEST, preferred_element_type=F32)
    else:
        imp = lax.dot_general(cover, psum, NT, precision=lax.Precision.HIGHEST, preferred_element_type=F32)
    jq = lax.broadcasted_iota(jnp.int32, (nsp, LANES), 0)
    t = t0 + lax.broadcasted_iota(jnp.int32, (nsp, LANES), 1)
    cur = lax.shift_right_logical(t, 6)
    forced = (jq == 0) | (jq == cur) | (jq == cur - 1)
    valid = (jq * SEL_BLOCK <= t) & (jq < n_sel)
    score = jnp.where(valid, imp + jnp.where(forced, FORCE_BONUS, 0.0), -jnp.inf)
    jf = jq.astype(F32)
    sel = jnp.zeros((nsp, LANES), F32)
    for _ in range(min(SEL_TOPN, n_sel)):
        mx = jnp.max(score, axis=0, keepdims=True)
        first = jnp.min(jnp.where(score == mx, jf, 1.0e9), axis=0, keepdims=True)
        pick = jf == first
        sel = jnp.where(pick, 1.0, sel)
        score = jnp.where(pick, -jnp.inf, score)
    bias = jnp.where((sel > 0.5) | (jq >= n_sel), 0.0, NEG_SEL)
    bias = jnp.concatenate([bias, jnp.zeros((LANES - nsp, LANES), F32)], axis=0)
    return bias.T


def _online_step_t(state, s_t, v_t):
    m, l, acc = state
    m_new = jnp.maximum(m, jnp.max(s_t, axis=0, keepdims=True))
    alpha = jnp.exp(m - m_new)
    p = jnp.exp(s_t - m_new)
    l = alpha * l + jnp.sum(p, axis=0, keepdims=True)
    acc = alpha * acc + jnp.dot(v_t, p.astype(BF16), preferred_element_type=F32)
    return m_new, l, acc


def _stack_heads(q_heads, feats):
    return jnp.concatenate(
        [jnp.concatenate([q, f.astype(BF16)], axis=1) for q, f in zip(q_heads, feats)], axis=0)


def _write_gated(o_ref, gates, g, rows, o_cmp, o_s, o_w):
    for r in range(NSA_GROUP):
        h = g * NSA_GROUP + r
        sl = slice(r * rows, (r + 1) * rows)
        c = 8 + h
        o = gates[:, c:c + 1] * o_cmp[sl] + gates[:, c + 8:c + 9] * o_s[sl] + gates[:, c + 16:c + 17] * o_w[sl]
        o_ref[:, h * NSA_HD:(h + 1) * NSA_HD] = o


def _nsa_prompt_body(q_ref, kc_ref, ks_ref, kw_ref, gate_ref, pw_ref, proj_ref, o_ref,
                     ks_aug, vs_t, kw_aug, vw_t, kcmp_aug, vcmp_t, a1_scr, *, seq):
    i = pl.program_id(1)
    tq = KEY_TILE
    n_ch = seq // CMP_STRIDE
    n_cmp = n_ch - 1
    n_sel = seq // SEL_BLOCK
    ncp = kcmp_aug.shape[1]
    G, HD = NSA_GROUP, NSA_HD
    Q = G * tq

    @pl.when(i == 0)
    def _build():
        pos = lax.broadcasted_iota(jnp.int32, (seq, LANES), 0)
        f_sel = _key_features(pos, True).astype(BF16)
        f_win = _key_features(pos, False).astype(BF16)
        for g in range(NSA_KV_HEADS):
            ks_aug[g, :, 0:HD] = ks_ref[pl.ds(g, seq, stride=KV_ROW), :].astype(BF16)
            ks_aug[g, :, HD:2 * HD] = f_sel
            kw_aug[g, :, 0:HD] = kw_ref[pl.ds(g, seq, stride=KV_ROW), :].astype(BF16)
            kw_aug[g, :, HD:2 * HD] = f_win
            for kt in range(seq // KEY_TILE):
                rows = slice(kt * KEY_TILE, (kt + 1) * KEY_TILE)
                src_rows = pl.ds(kt * KEY_TILE * KV_ROW + 2 + g, KEY_TILE, stride=KV_ROW)
                vs_t[g, :, rows] = ks_ref[src_rows, :].T.astype(BF16)
                vw_t[g, :, rows] = kw_ref[src_rows, :].T.astype(BF16)
        kcmp_aug[...] = jnp.zeros(kcmp_aug.shape, BF16)
        a1_scr[n_ch:n_ch + 8, :] = jnp.zeros((8, HD), F32)
        for kv in range(2):
            for g in range(NSA_KV_HEADS):
                c = kv * NSA_KV_HEADS + g
                a0, a1 = _compress_block_rows(
                    lambda j: kc_ref[pl.ds(j * KV_ROW + c, n_ch, stride=KV_ROW * CMP_STRIDE), :], pw_ref, kv, g)
                a1_scr[0:n_ch, :] = a1
                _finish_compress(a0 + a1_scr[pl.ds(1, n_ch), :], proj_ref, kcmp_aug, vcmp_t, kv, g, n_ch, ncp, True)

    t0 = i * tq
    key = lax.broadcasted_iota(jnp.int32, (KEY_CHUNK, Q), 0)
    t_cols = t0 + jnp.bitwise_and(lax.broadcasted_iota(jnp.int32, (KEY_CHUNK, Q), 1), tq - 1)
    gates_t = jax.nn.sigmoid(gate_ref[...]).T
    scale = HD ** -0.5

    n = lax.broadcasted_iota(jnp.int32, (ncp, Q), 0)
    t_c = t0 + jnp.bitwise_and(lax.broadcasted_iota(jnp.int32, (ncp, Q), 1), tq - 1)
    mask = (n * CMP_STRIDE + (CMP_LEN - 1) <= t_c) & (n < n_cmp)
    kv_groups = range(NSA_KV_HEADS)
    q_heads = [[(q_ref[:, (g * G + r) * HD:(g * G + r + 1) * HD] * scale).astype(BF16) for r in range(G)]
               for g in kv_groups]
    feats = [[_query_features((tq, LANES), g * G + r) for r in range(G)] for g in kv_groups]
    q_plain = [_stack_heads(q_heads[g], feats[g]) for g in kv_groups]
    s_c = [jnp.where(mask, lax.dot_general(kcmp_aug[g], q_plain[g], NT, preferred_element_type=F32), NEG_MASK)
           for g in kv_groups]
    e_c = [jnp.where(mask, jnp.exp(s - jnp.max(s, axis=0, keepdims=True)), 0.0) for s in s_c]
    p_c = [e / jnp.maximum(jnp.sum(e, axis=0, keepdims=True), 1e-30) for e in e_c]
    o_cmp = [jnp.dot(vcmp_t[g], p_c[g].astype(BF16), preferred_element_type=F32) for g in kv_groups]
    psum = [p[:, 0:tq] + p[:, tq:2 * tq] + p[:, 2 * tq:3 * tq] + p[:, 3 * tq:4 * tq] for p in p_c]
    bias = [_select_blocks(psum[g], t0, n_cmp, n_sel, True) for g in kv_groups]
    q_sel = [_stack_heads(q_heads[g], [f + bias[g] for f in feats[g]]) for g in kv_groups]

    last = t0 // KEY_CHUNK
    first_w = jnp.maximum(t0 - WINDOW, 0) // KEY_CHUNK
    init = (jnp.full((1, Q), NEG_MASK, F32), jnp.zeros((1, Q), F32), jnp.zeros((HD, Q), F32))

    def scores(c, k_aug, q, g, valid):
        off = pl.multiple_of(c * KEY_CHUNK, KEY_CHUNK)
        s_t = lax.dot_general(k_aug[g, pl.ds(off, KEY_CHUNK), :], q, NT, preferred_element_type=F32)
        return s_t if valid is None else jnp.where(valid(off + key), s_t, NEG_MASK)

    def values(c, v_t, g):
        return v_t[g, :, pl.ds(pl.multiple_of(c * KEY_CHUNK, KEY_CHUNK), KEY_CHUNK)]

    def causal(kpos):
        return kpos <= t_cols

    def band(kpos):
        return (kpos <= t_cols) & (t_cols - kpos <= WINDOW)

    def early(c, sel):
        s = [scores(c, ks_aug, q_sel[g], g, None) for g in kv_groups]
        return tuple(_online_step_t(sel[g], s[g], values(c, vs_t, g)) for g in kv_groups)

    def late(c, sts):
        sel, win = sts
        s_sel = [scores(c, ks_aug, q_sel[g], g, causal) for g in kv_groups]
        s_win = [scores(c, kw_aug, q_plain[g], g, band) for g in kv_groups]
        sel = tuple(_online_step_t(sel[g], s_sel[g], values(c, vs_t, g)) for g in kv_groups)
        win = tuple(_online_step_t(win[g], s_win[g], values(c, vw_t, g)) for g in kv_groups)
        return sel, win

    inits = (init,) * NSA_KV_HEADS
    sel = lax.fori_loop(0, first_w, early, inits)
    sel, win = lax.fori_loop(first_w, last + 1, late, (sel, inits))

    for g in kv_groups:
        o_s = sel[g][2] / sel[g][1]
        o_w = win[g][2] / win[g][1]
        for r in range(G):
            h = g * G + r
            cols = slice(r * tq, (r + 1) * tq)
            c = 8 + h
            o_t = (gates_t[c:c + 1, :] * o_cmp[g][:, cols] + gates_t[c + 8:c + 9, :] * o_s[:, cols]
                   + gates_t[c + 16:c + 17, :] * o_w[:, cols])
            o_ref[:, h * HD:(h + 1) * HD] = o_t.T


def nsa_prompt(zb, zs, kv_cmp, kv_slc, kv_win, cmp_pos_w, cmp_proj, y_init, *, layer, batch, seq, col_q):
    assert seq % KEY_CHUNK == 0 and seq // SEL_BLOCK <= SEL_COLS
    nq = seq // KEY_TILE
    n_ch = seq // CMP_STRIDE
    ncp = -(-n_ch // LANES) * LANES
    qw = NSA_HEADS * NSA_HD
    pw = cmp_pos_w.reshape(2, CMP_LEN, 2 * NSA_HD)
    return pl.pallas_call(
        functools.partial(_drop_alias_ref, functools.partial(_nsa_prompt_body, seq=seq), 7),
        out_shape=jax.ShapeDtypeStruct(y_init.shape, F32),
        grid=(batch, nq),
        in_specs=[
            pl.BlockSpec((KEY_TILE, qw), lambda b, i: (b * nq + i, col_q // qw)),
            pl.BlockSpec((None, seq * KV_ROW, NSA_HD), lambda b, i: (layer, b, 0)),
            pl.BlockSpec((None, seq * KV_ROW, NSA_HD), lambda b, i: (layer, b, 0)),
            pl.BlockSpec((None, seq * KV_ROW, NSA_HD), lambda b, i: (layer, b, 0)),
            pl.BlockSpec((KEY_TILE, LANES), lambda b, i: (b * nq + i, 0)),
            pl.BlockSpec((2, CMP_LEN, 2 * NSA_HD), lambda b, i: (0, 0, 0)),
            pl.BlockSpec((2, NSA_KV_HEADS, NSA_HD, NSA_HD), lambda b, i: (0, 0, 0, 0)),
            pl.BlockSpec(memory_space=pl.ANY),
        ],
        out_specs=pl.BlockSpec((KEY_TILE, qw), lambda b, i: (b * nq + i, 0)),
        scratch_shapes=[
            pltpu.VMEM((NSA_KV_HEADS, seq, 2 * NSA_HD), BF16),
            pltpu.VMEM((NSA_KV_HEADS, NSA_HD, seq), BF16),
            pltpu.VMEM((NSA_KV_HEADS, seq, 2 * NSA_HD), BF16),
            pltpu.VMEM((NSA_KV_HEADS, NSA_HD, seq), BF16),
            pltpu.VMEM((NSA_KV_HEADS, ncp, 2 * NSA_HD), BF16),
            pltpu.VMEM((NSA_KV_HEADS, NSA_HD, ncp), BF16),
            pltpu.VMEM((n_ch + 8, NSA_HD), F32),
        ],
        input_output_aliases={7: 0},
        compiler_params=_params("parallel", "arbitrary"),
        name="nsa_prompt",
    )(zb, kv_cmp, kv_slc, kv_win, zs, pw, cmp_proj, y_init)


def _nsa_sample_body(pt_ref, q_ref, ksn_ref, kwn_ref, gate_ref, wprev_ref, pw_ref, proj_ref, *rest,
                     ts, past, n_pages):
    del pt_ref
    cmp_pages = rest[:n_pages]
    slc_pages = rest[n_pages:2 * n_pages]
    o_ref = rest[2 * n_pages + 1]
    ks_aug, vs, kw_aug, vw, kcmp_aug, vcmp, acc_scr = rest[2 * n_pages + 2:]
    G, HD = NSA_GROUP, NSA_HD
    page = slc_pages[0].shape[0] // 4
    kp = ks_aug.shape[1]
    wprev = wprev_ref.shape[0] // 4
    wp = kw_aug.shape[1]
    win_pos0 = past - wprev
    n_ch = (past + ts) // CMP_STRIDE
    n_cmp = n_ch - 1
    n_sel = -(-(past + ts) // SEL_BLOCK)
    ncp = kcmp_aug.shape[1]
    ch_per_page = page // CMP_STRIDE

    @pl.when(pl.program_id(0) == 0)
    def _constants():
        pos = lax.broadcasted_iota(jnp.int32, (kp, LANES), 0)
        f_sel = _key_features(pos, True).astype(BF16)
        posw = win_pos0 + lax.broadcasted_iota(jnp.int32, (wp, LANES), 0)
        f_win = _key_features(posw, False).astype(BF16)
        for g in range(NSA_KV_HEADS):
            ks_aug[g, :, HD:2 * HD] = f_sel
            kw_aug[g, :, HD:2 * HD] = f_win
        kcmp_aug[...] = jnp.zeros(kcmp_aug.shape, BF16)
        vcmp[...] = jnp.zeros(vcmp.shape, BF16)

    def with_tail(new_rows):
        return jnp.concatenate([new_rows, jnp.zeros((KEY_TILE - ts, HD), F32)], axis=0).astype(BF16)

    for g in range(NSA_KV_HEADS):
        for p in range(n_pages):
            rows = slice(p * page, (p + 1) * page)
            ks_aug[g, rows, 0:HD] = slc_pages[p][pl.ds(g, page, stride=4), :].astype(BF16)
            vs[g, rows, :] = slc_pages[p][pl.ds(2 + g, page, stride=4), :].astype(BF16)
        ks_aug[g, past:past + KEY_TILE, 0:HD] = with_tail(ksn_ref[pl.ds(g, ts, stride=KV_ROW), :])
        vs[g, past:past + KEY_TILE, :] = with_tail(ksn_ref[pl.ds(2 + g, ts, stride=KV_ROW), :])
        kw_aug[g, 0:wprev, 0:HD] = wprev_ref[pl.ds(g, wprev, stride=4), :].astype(BF16)
        vw[g, 0:wprev, :] = wprev_ref[pl.ds(2 + g, wprev, stride=4), :].astype(BF16)
        kw_aug[g, wprev:wprev + KEY_TILE, 0:HD] = with_tail(kwn_ref[pl.ds(g, ts, stride=KV_ROW), :])
        vw[g, wprev:wprev + KEY_TILE, :] = with_tail(kwn_ref[pl.ds(2 + g, ts, stride=KV_ROW), :])

    for kv in range(2):
        for g in range(NSA_KV_HEADS):
            c = kv * NSA_KV_HEADS + g
            cols = slice(g * HD, (g + 1) * HD)
            w0 = jnp.concatenate([pw_ref[kv, 0:CMP_STRIDE, cols]] * ch_per_page, axis=0)
            w1 = jnp.concatenate([pw_ref[kv, CMP_STRIDE:CMP_LEN, cols]] * ch_per_page, axis=0)
            for p in range(n_pages):
                x = cmp_pages[p][pl.ds(c, page, stride=4), :]
                if p + 1 < n_pages:
                    nxt = cmp_pages[p + 1][pl.ds(c, CMP_STRIDE, stride=4), :] * w1[0:CMP_STRIDE]
                else:
                    nxt = jnp.zeros((CMP_STRIDE, HD), F32)
                z = x * w0 + jnp.concatenate([(x * w1)[CMP_STRIDE:], nxt], axis=0)
                acc_scr[c, p * ch_per_page:(p + 1) * ch_per_page, :] = jnp.sum(
                    z.reshape(ch_per_page, CMP_STRIDE, HD), axis=1)
            _finish_compress(acc_scr[c, 0:n_ch, :], proj_ref, kcmp_aug, vcmp, kv, g, n_ch, ncp, False)

    R = G * ts
    gates = jax.nn.sigmoid(gate_ref[...])
    scale = HD ** -0.5

    def t_of(shape):
        return past + jnp.bitwise_and(lax.broadcasted_iota(jnp.int32, shape, 0), ts - 1)

    def softmax_pv(s, v):
        m = jnp.max(s, axis=1, keepdims=True)
        e = jnp.exp(s - m)
        return jnp.dot(e.astype(BF16), v, preferred_element_type=F32) / jnp.sum(e, axis=1, keepdims=True)

    kv_groups = range(NSA_KV_HEADS)
    q_heads = [[(q_ref[:, (g * G + r) * HD:(g * G + r + 1) * HD] * scale).astype(BF16) for r in range(G)]
               for g in kv_groups]
    feats = [[_query_features((ts, LANES), g * G + r) for r in range(G)] for g in kv_groups]
    q_plain = [_stack_heads(q_heads[g], feats[g]) for g in kv_groups]

    idx = lax.broadcasted_iota(jnp.int32, (R, wp), 1)
    dist = t_of((R, wp)) - (win_pos0 + idx)
    win_ok = (idx < wprev + ts) & (dist >= 0) & (dist <= WINDOW)
    s_win = [jnp.where(win_ok, lax.dot_general(q_plain[g], kw_aug[g], NT, preferred_element_type=F32), NEG_MASK)
             for g in kv_groups]
    n = lax.broadcasted_iota(jnp.int32, (R, ncp), 1)
    cmp_ok = (n * CMP_STRIDE + (CMP_LEN - 1) <= t_of((R, ncp))) & (n < n_cmp)
    s_cmp = [lax.dot_general(q_plain[g], kcmp_aug[g], NT, preferred_element_type=F32) for g in kv_groups]
    p_c = [_masked_softmax(s, cmp_ok) for s in s_cmp]
    o_cmp = [jnp.dot(p_c[g].astype(BF16), vcmp[g], preferred_element_type=F32) for g in kv_groups]
    o_w = [softmax_pv(s_win[g], vw[g]) for g in kv_groups]
    psum = [jnp.concatenate([p[0:ts] + p[ts:2 * ts] + p[2 * ts:3 * ts] + p[3 * ts:4 * ts],
                             jnp.zeros((LANES - ts, ncp), F32)], axis=0) for p in p_c]
    bias = [_select_blocks(psum[g], past, n_cmp, n_sel, False)[0:ts] for g in kv_groups]
    q_sel = [_stack_heads(q_heads[g], [f + bias[g] for f in feats[g]]) for g in kv_groups]
    sel_ok = lax.broadcasted_iota(jnp.int32, (R, kp), 1) <= t_of((R, kp))
    s_sel = [jnp.where(sel_ok, lax.dot_general(q_sel[g], ks_aug[g], NT, preferred_element_type=F32), NEG_MASK)
             for g in kv_groups]
    o_s = [softmax_pv(s_sel[g], vs[g]) for g in kv_groups]
    for g in kv_groups:
        _write_gated(o_ref, gates, g, ts, o_cmp[g], o_s[g], o_w[g])


def nsa_sample(zb, zs, kv_slc, kv_win, cache_cmp, cache_slc, state_win, page_table, cmp_pos_w, cmp_proj, y_init, *,
               layer, row0, batch, ts, col_q):
    depth, n_pool, page = cache_cmp.shape[:3]
    n_pages = page_table.shape[1]
    past = n_pages * page
    wprev = state_win.shape[2]
    assert ts & (ts - 1) == 0 and ts <= KEY_TILE and row0 % ts == 0
    assert past % KEY_TILE == 0 and (past + ts) // CMP_STRIDE == past // CMP_STRIDE
    assert page % CMP_STRIDE == 0 and -(-(past + ts) // SEL_BLOCK) <= SEL_COLS and wprev % 16 == 0
    qw = NSA_HEADS * NSA_HD
    n_ch = past // CMP_STRIDE
    ncp = -(-n_ch // LANES) * LANES
    r0 = row0 // ts
    pw = cmp_pos_w.reshape(2, CMP_LEN, 2 * NSA_HD)
    cmp_view = cache_cmp.reshape(depth, n_pool, page * 4, NSA_HD)
    slc_view = cache_slc.reshape(depth, n_pool, page * 4, NSA_HD)
    win_view = state_win.reshape(depth, batch, wprev * 4, NSA_HD)

    def page_map(p):
        return lambda b, pt: (layer, pt[b * n_pages + p], 0, 0)

    in_specs = [
        pl.BlockSpec((ts, qw), lambda b, pt: (r0 + b, col_q // qw)),
        pl.BlockSpec((None, ts * KV_ROW, NSA_HD), lambda b, pt: (layer, b, 0)),
        pl.BlockSpec((None, ts * KV_ROW, NSA_HD), lambda b, pt: (layer, b, 0)),
        pl.BlockSpec((ts, LANES), lambda b, pt: (r0 + b, 0)),
        pl.BlockSpec((None, None, wprev * 4, NSA_HD), lambda b, pt: (layer, b, 0, 0)),
        pl.BlockSpec((2, CMP_LEN, 2 * NSA_HD), lambda b, pt: (0, 0, 0)),
        pl.BlockSpec((2, NSA_KV_HEADS, NSA_HD, NSA_HD), lambda b, pt: (0, 0, 0, 0)),
    ]
    in_specs += [pl.BlockSpec((None, None, page * 4, NSA_HD), page_map(p % n_pages)) for p in range(2 * n_pages)]
    in_specs.append(pl.BlockSpec(memory_space=pl.ANY))
    return pl.pallas_call(
        functools.partial(_nsa_sample_body, ts=ts, past=past, n_pages=n_pages),
        out_shape=jax.ShapeDtypeStruct(y_init.shape, F32),
        grid_spec=pltpu.PrefetchScalarGridSpec(
            num_scalar_prefetch=1,
            grid=(batch,),
            in_specs=in_specs,
            out_specs=pl.BlockSpec((ts, qw), lambda b, pt: (r0 + b, 0)),
            scratch_shapes=[
                pltpu.VMEM((NSA_KV_HEADS, past + KEY_TILE, 2 * NSA_HD), BF16),
                pltpu.VMEM((NSA_KV_HEADS, past + KEY_TILE, NSA_HD), BF16),
                pltpu.VMEM((NSA_KV_HEADS, wprev + KEY_TILE, 2 * NSA_HD), BF16),
                pltpu.VMEM((NSA_KV_HEADS, wprev + KEY_TILE, NSA_HD), BF16),
                pltpu.VMEM((NSA_KV_HEADS, ncp, 2 * NSA_HD), BF16),
                pltpu.VMEM((NSA_KV_HEADS, ncp, NSA_HD), BF16),
                pltpu.VMEM((2 * NSA_KV_HEADS, n_ch, NSA_HD), F32),
            ],
        ),
        input_output_aliases={8 + 2 * n_pages: 0},
        compiler_params=_params("arbitrary"),
        name="nsa_sample",
    )(page_table.reshape(-1), zb, kv_slc, kv_win, zs, win_view, pw, cmp_proj,
      *([cmp_view] * n_pages), *([slc_view] * n_pages), y_init)


def _window_state_body(old_ref, new_ref, o_ref):
    keep = o_ref.shape[1] - new_ref.shape[1]
    o_ref[:, 0:keep, :] = old_ref[:, old_ref.shape[1] - keep:, :]
    o_ref[:, keep:, :] = new_ref[...]


def window_state(state_win, kv_win_new, ts):
    depth, batch, w_old = state_win.shape[:3]
    keep = min(WINDOW, w_old + ts) - ts
    assert keep >= 0 and (keep * KV_ROW) % 8 == 0 and (ts * KV_ROW) % 8 == 0
    bt = math.gcd(batch, 4)
    out = pl.pallas_call(
        _window_state_body,
        out_shape=jax.ShapeDtypeStruct((depth, batch, (keep + ts) * KV_ROW, NSA_HD), F32),
        grid=(depth, batch // bt),
        in_specs=[
            pl.BlockSpec((None, bt, w_old * KV_ROW, NSA_HD), lambda l, b: (l, b, 0, 0)),
            pl.BlockSpec((None, bt, ts * KV_ROW, NSA_HD), lambda l, b: (l, b, 0, 0)),
        ],
        out_specs=pl.BlockSpec((None, bt, (keep + ts) * KV_ROW, NSA_HD), lambda l, b: (l, b, 0, 0)),
        compiler_params=_params("parallel", "parallel"),
        name="window_state",
    )(state_win.reshape(depth, batch, w_old * KV_ROW, NSA_HD), kv_win_new.reshape(depth, batch, ts * KV_ROW, NSA_HD))
    return out.reshape(depth, batch, keep + ts, *state_win.shape[3:])


ZB_NQ, ZB_POOL, ZB_MQ, ZB_MK, ZB_MV, ZB_MO, ZB_END = (0, 1024, 1536, 2048, 2560, 3072, 3584)
W_POOL, W_MI, W_NQ, W_CMP, W_NG, W_END = 0, 2560, 2568, 3592, 5128, 5152


def _split_w_in(w_in_l):
    w = w_in_l.astype(BF16)
    big = jnp.concatenate([w[:, W_NQ:W_CMP], w[:, W_POOL:W_MI], w[:, W_CMP:W_NG]], axis=1)
    small = jnp.concatenate([w[:, W_MI:W_NQ], w[:, W_NG:W_END]], axis=1)
    return big, jnp.pad(small, ((0, 0), (0, LANES - small.shape[1])))


def kernel(x_prompt, x_sample, cache_kv_cmp, cache_kv_slc, state_kv_win, state_pool, state_mlstm_C, state_mlstm_n, state_mlstm_m, page_table, ffn1_norm, ffn1_w_gate, ffn1_w_up, ffn1_w_down, mix_norm, w_in, w_out, pool_w, pool_scale, mlstm_if_bias, mlstm_norm, nsa_cmp_pos_w, nsa_cmp_proj, ffn2_norm, ffn2_w_gate, ffn2_w_up, ffn2_w_down, final_norm):
    bp, tp, d = x_prompt.shape
    bs, ts, _ = x_sample.shape
    depth = w_in.shape[0]
    mp, ms = bp * tp, bs * ts
    m_all = mp + ms
    past_len = page_table.shape[1] * cache_kv_cmp.shape[2]
    pd = pool_scale.shape[1]
    kv_row = (2, NSA_KV_HEADS, NSA_HD)
    zeros = lambda *s: jnp.zeros(s, F32)
    ffn1 = [w.astype(BF16) for w in (ffn1_w_gate, ffn1_w_up, ffn1_w_down)]
    ffn2 = [w.astype(BF16) for w in (ffn2_w_gate, ffn2_w_up, ffn2_w_down)]
    w_out_b = w_out.astype(BF16)
    outs = [[] for _ in range(6)]
    kv_bufs = ([zeros(depth, mp * KV_ROW, LANES) for _ in range(KV_SLABS)]
               + [zeros(depth, ms * KV_ROW, LANES) for _ in range(KV_SLABS)])
    p_c = zeros(depth, bp, MLSTM_HEADS, MLSTM_HD, MLSTM_HD)
    s_c = zeros(depth, bs, MLSTM_HEADS, MLSTM_HD, MLSTM_HD)
    for l in range(depth):
        if l == 0:
            x = ffn_half_step(x_prompt.reshape(mp, d), ffn1_norm[l], *ffn1, l, y_init=zeros(m_all, d))
            x = ffn_half_step(x_sample.reshape(ms, d), ffn1_norm[l], *ffn1, l, out_row0=mp, y_init=x)
        else:
            x = ffn_half_step(x, ffn1_norm[l], *ffn1, l)
        w_big, w_small = _split_w_in(w_in[l])
        zb, zs, kv_bufs = mix_project(x, mix_norm[l], w_big, w_small, l, mp, kv_bufs)
        cmp_p, slc_p, win_p, cmp_s, slc_s, win_s = kv_bufs
        z_pool = zb[:, ZB_POOL:ZB_MQ]

        y_pool = pool_prompt(zb, pool_w[l], pool_scale[l], zeros(m_all, pd), batch=bp, seq=tp, col=ZB_POOL)
        y_m, p_c, p_n, p_m = mlstm_mix(
            zb, zs, mlstm_if_bias[l], mlstm_norm[l], zeros(depth, bp, MLSTM_HEADS, MLSTM_HD, MLSTM_HD),
            zeros(bp, MLSTM_HEADS, MLSTM_HD), zeros(bp, MLSTM_HEADS), p_c, zeros(m_all, MLSTM_HEADS * MLSTM_HD),
            row0=0, batch=bp, seq=tp, col_q=ZB_MQ, layer=l)
        y_nsa = nsa_prompt(zb, zs, cmp_p, slc_p, win_p, nsa_cmp_pos_w[l], nsa_cmp_proj[l],
                           zeros(m_all, NSA_HEADS * NSA_HD), layer=l, batch=bp, seq=tp, col_q=ZB_NQ)

        pool_full = jnp.concatenate([zeros(bs, POOL_HALO - POOL_BUF, pd), state_pool[l],
                                     z_pool[mp:].reshape(bs, ts, pd)], axis=1)
        y_pool = pool_sample(pool_full, pool_w[l], pool_scale[l], y_pool, pos0=past_len, row0=mp)
        y_m, s_c, s_n, s_m = mlstm_mix(
            zb, zs, mlstm_if_bias[l], mlstm_norm[l], state_mlstm_C, state_mlstm_n[l], state_mlstm_m[l], s_c, y_m,
            row0=mp, batch=bs, seq=ts, col_q=ZB_MQ, layer=l)
        y_nsa = nsa_sample(zb, zs, slc_s, win_s, cache_kv_cmp, cache_kv_slc, state_kv_win, page_table,
                           nsa_cmp_pos_w[l], nsa_cmp_proj[l], y_nsa, layer=l, row0=mp, batch=bs, ts=ts,
                           col_q=ZB_NQ)

        x = out_project(x, y_pool, y_m, y_nsa, w_out_b, l)
        if l < depth - 1:
            x = ffn_half_step(x, ffn2_norm[l], *ffn2, l)
        else:
            y_prompt = ffn_half_step(x, ffn2_norm[l], *ffn2, l, gf=final_norm, rows=mp).reshape(bp, tp, d)
            y_sample = ffn_half_step(x, ffn2_norm[l], *ffn2, l, gf=final_norm, in_row0=mp, rows=ms).reshape(bs, ts, d)

        pool_p = jnp.concatenate([zeros(bp, POOL_BUF, pd), z_pool[:mp].reshape(bp, tp, pd)], axis=1)
        layer_out = (pool_p[:, -POOL_BUF:], p_n, p_m, pool_full[:, -POOL_BUF:], s_n, s_m)
        for acc, a in zip(outs, layer_out):
            acc.append(a)
    p_pool, p_n, p_m, s_pool, s_n, s_m = [jnp.stack(a) for a in outs]
    p_kv = [b.reshape(depth, bp, tp, *kv_row) for b in (cmp_p, slc_p, win_p)]
    s_kv = [b.reshape(depth, bs, ts, *kv_row) for b in (cmp_s, slc_s, win_s)]
    s_kv_win = window_state(state_kv_win, win_s, ts)
    return (y_prompt, y_sample, p_kv[0], p_kv[1], p_kv[2][:, :, tp - min(WINDOW, tp):], p_pool, p_c, p_n, p_m,
            s_kv[0], s_kv[1], s_kv_win, s_pool, s_c, s_n, s_m)
```
